```python
import jax, jax.numpy as jnp
from jax import lax
import numpy as np

D_MODEL = 1024
BATCH = 8
SEQ = 2048
DEPTH = 2
DEC_BATCH = 8
DEC_SEQ = 64
PAST_LEN = 4096

CHUNK = 64
Q_BLOCK = 128
N_HEADS = 8
QK_NOPE = 64
QK_ROPE = 32
QK_HEAD = QK_NOPE + QK_ROPE
V_HEAD = 64
Q_LORA = 256
KV_LORA = 128
ROPE_THETA = 10000.0
ATTN_WIDTH = N_HEADS * V_HEAD
GMLP_GROUPS = 8
GMLP_GROUP_DIM = 64
GMLP_WIDTH = GMLP_GROUPS * GMLP_GROUP_DIM
GMLP_CHUNK = 128
MIX_WIDTH = ATTN_WIDTH + GMLP_WIDTH
IN_COLS = Q_LORA + KV_LORA + QK_ROPE + 2 * GMLP_WIDTH
SPLITS = (Q_LORA, Q_LORA + KV_LORA, Q_LORA + KV_LORA + QK_ROPE,
          Q_LORA + KV_LORA + QK_ROPE + GMLP_WIDTH)
D_FF = ((-(-8 * D_MODEL // 3) + 255) // 256) * 256
EPS = 1e-6
NEG_INF = -1e30

kernel_name = 'hybrid_mla_gmlp_stream_step'


def rms_norm(x, g):
    xf = x.astype(jnp.float32)
    y = xf * lax.rsqrt(jnp.mean(jnp.square(xf), axis=-1, keepdims=True) + EPS)
    return (y * g.astype(jnp.float32)).astype(x.dtype)


def layer_norm(x, g, b):
    xf = x.astype(jnp.float32)
    mu = jnp.mean(xf, axis=-1, keepdims=True)
    xc = xf - mu
    y = xc * lax.rsqrt(jnp.mean(jnp.square(xc), axis=-1, keepdims=True) + EPS)
    return (y * g.astype(jnp.float32) + b.astype(jnp.float32)).astype(x.dtype)


def rope_angles(pos):
    half = QK_ROPE // 2
    inv = ROPE_THETA ** (-jnp.arange(half, dtype=jnp.float32) / half)
    ang = pos.astype(jnp.float32)[:, None] * inv[None, :]
    return jnp.cos(ang), jnp.sin(ang)


def apply_rope(x, cos, sin):
    half = QK_ROPE // 2
    xf = x.astype(jnp.float32)
    x1, x2 = xf[..., :half], xf[..., half:]
    return jnp.concatenate([x1 * cos - x2 * sin, x1 * sin + x2 * cos], axis=-1).astype(x.dtype)


def attend_block(qb, qpos_b, k, v, kpos):
    s = jnp.einsum('bqhe,bkhe->bhqk', qb, k, preferred_element_type=jnp.float32) * (QK_HEAD ** -0.5)
    vis = (kpos[None, :] // CHUNK) <= (qpos_b[:, None] // CHUNK)
    s = jnp.where(vis[None, None], s, NEG_INF)
    p = jax.nn.softmax(s, axis=-1)
    return jnp.einsum('bhqk,bkhd->bqhd', p.astype(v.dtype), v)


def block_causal_attention(q, k, v, qpos, kpos):
    B, S, H, E = q.shape
    if S <= Q_BLOCK:
        return attend_block(q, qpos, k, v, kpos)
    nb = S // Q_BLOCK
    qb = q.reshape(B, nb, Q_BLOCK, H, E).transpose(1, 0, 2, 3, 4)
    ob = lax.map(lambda a: attend_block(a[0], a[1], k, v, kpos), (qb, qpos.reshape(nb, Q_BLOCK)))
    return ob.transpose(1, 0, 2, 3, 4).reshape(B, S, H, V_HEAD)


def spatial_gating(u, v, ln_g, ln_b, w_s, b_s):
    B, S, _ = u.shape
    L = min(S, GMLP_CHUNK)
    nc = S // L
    ug = jax.nn.gelu(u).reshape(B, S, GMLP_GROUPS, GMLP_GROUP_DIM)
    vn = layer_norm(jax.nn.gelu(v).reshape(B, S, GMLP_GROUPS, GMLP_GROUP_DIM), ln_g, ln_b)
    idx = jnp.arange(L)
    causal = (idx[:, None] // CHUNK) >= (idx[None, :] // CHUNK)
    ws = jnp.where(causal[None], w_s[:, :L, :L], 0.0)
    mixed = jnp.einsum('gij,bcjgd->bcigd', ws, vn.reshape(B, nc, L, GMLP_GROUPS, GMLP_GROUP_DIM))
    mixed = mixed + b_s[:, :L].T[None, None, :, :, None]
    out = ug * mixed.reshape(B, S, GMLP_GROUPS, GMLP_GROUP_DIM)
    return out, vn.reshape(B, S, GMLP_WIDTH)


def hybrid_layer(x, pos, ckv_past, kr_past, p):
    B, S, _ = x.shape
    h = rms_norm(x, p['g_mix'])
    z = h @ p['w_in']
    q_lat, c_kv, k_r, u, v = jnp.split(z, SPLITS, axis=-1)
    cos, sin = rope_angles(pos)
    q = (rms_norm(q_lat, p['g_q_lora']) @ p['w_q_up']).reshape(B, S, N_HEADS, QK_HEAD)
    q_nope = rms_norm(q[..., :QK_NOPE], p['g_qn'])
    q_rope = apply_rope(rms_norm(q[..., QK_NOPE:], p['g_qr']), cos[:, None, :], sin[:, None, :])
    q = jnp.concatenate([q_nope, q_rope], axis=-1)
    c_kv = rms_norm(c_kv, p['g_kv_lora'])
    k_r = apply_rope(rms_norm(k_r, p['g_kr']), cos, sin)
    if ckv_past is None:
        ckv_all, kr_all, kpos = c_kv, k_r, pos
    else:
        ckv_all = jnp.concatenate([ckv_past, c_kv], axis=1)
        kr_all = jnp.concatenate([kr_past, k_r], axis=1)
        kpos = jnp.arange(ckv_all.shape[1])
    T = ckv_all.shape[1]
    kv = (ckv_all @ p['w_kv_up']).reshape(B, T, N_HEADS, QK_NOPE + V_HEAD)
    k_nope = rms_norm(kv[..., :QK_NOPE], p['g_kn'])
    v_att = kv[..., QK_NOPE:]
    k = jnp.concatenate([k_nope, jnp.broadcast_to(kr_all[:, :, None, :], (B, T, N_HEADS, QK_ROPE))], axis=-1)
    o_att = block_causal_attention(q, k, v_att, pos, kpos)
    o_gm, v_rows = spatial_gating(u, v, p['gm_ln_g'], p['gm_ln_b'], p['gm_w_s'], p['gm_b_s'])
    merged = jnp.concatenate([rms_norm(o_att, p['g_out_att']).reshape(B, S, ATTN_WIDTH),
                              rms_norm(o_gm, p['g_out_gm']).reshape(B, S, GMLP_WIDTH)], axis=-1)
    x = x + merged @ p['w_o']
    h2 = rms_norm(x, p['g_ffn'])
    x = x + (jax.nn.silu(h2 @ p['w_gate']) * (h2 @ p['w_up'])) @ p['w_down']
    return x, c_kv, k_r, v_rows


def setup_inputs(seed: int = 0) -> dict:
    key = jax.random.key(seed)
    ks = jax.random.split(key, 26)

    def nrm(k, shape, scale):
        return scale * jax.random.normal(k, shape, jnp.float32)

    def gain(k, shape):
        return 1.0 + 0.02 * jax.random.normal(k, shape, jnp.float32)

    return {
        'x_prompt': nrm(ks[0], (BATCH, SEQ, D_MODEL), 1.0),
        'x_sample': nrm(ks[1], (DEC_BATCH, DEC_SEQ, D_MODEL), 1.0),
        'cache_ckv': nrm(ks[2], (DEPTH, DEC_BATCH, PAST_LEN, KV_LORA), 1.0),
        'cache_krope': nrm(ks[3], (DEPTH, DEC_BATCH, PAST_LEN, QK_ROPE), 1.0),
        'g_mix': gain(ks[4], (DEPTH, D_MODEL)),
        'w_in': nrm(ks[5], (DEPTH, D_MODEL, IN_COLS), D_MODEL ** -0.5),
        'g_q_lora': gain(ks[6], (DEPTH, Q_LORA)),
        'w_q_up': nrm(ks[7], (DEPTH, Q_LORA, N_HEADS * QK_HEAD), Q_LORA ** -0.5),
        'g_qn': gain(ks[8], (DEPTH, QK_NOPE)),
        'g_qr': gain(ks[9], (DEPTH, QK_ROPE)),
        'g_kv_lora': gain(ks[10], (DEPTH, KV_LORA)),
        'g_kr': gain(ks[11], (DEPTH, QK_ROPE)),
        'w_kv_up': nrm(ks[12], (DEPTH, KV_LORA, N_HEADS * (QK_NOPE + V_HEAD)), KV_LORA ** -0.5),
        'g_kn': gain(ks[13], (DEPTH, QK_NOPE)),
        'gm_ln_g': gain(ks[14], (DEPTH, GMLP_GROUPS, GMLP_GROUP_DIM)),
        'gm_ln_b': nrm(ks[15], (DEPTH, GMLP_GROUPS, GMLP_GROUP_DIM), 0.02),
        'gm_w_s': nrm(ks[16], (DEPTH, GMLP_GROUPS, GMLP_CHUNK, GMLP_CHUNK), GMLP_CHUNK ** -0.5),
        'gm_b_s': 1.0 + nrm(ks[17], (DEPTH, GMLP_GROUPS, GMLP_CHUNK), 0.1),
        'g_out_att': gain(ks[18], (DEPTH, N_HEADS, V_HEAD)),
        'g_out_gm': gain(ks[19], (DEPTH, GMLP_GROUPS, GMLP_GROUP_DIM)),
        'w_o': nrm(ks[20], (DEPTH, MIX_WIDTH, D_MODEL), MIX_WIDTH ** -0.5),
        'g_ffn': gain(ks[21], (DEPTH, D_MODEL)),
        'w_gate': nrm(ks[22], (DEPTH, D_MODEL, D_FF), D_MODEL ** -0.5),
        'w_up': nrm(ks[23], (DEPTH, D_MODEL, D_FF), D_MODEL ** -0.5),
        'w_down': nrm(ks[24], (DEPTH, D_FF, D_MODEL), D_FF ** -0.5),
    }


def reference(x_prompt, x_sample, cache_ckv, cache_krope, g_mix, w_in, g_q_lora, w_q_up,
              g_qn, g_qr, g_kv_lora, g_kr, w_kv_up, g_kn, gm_ln_g, gm_ln_b, gm_w_s, gm_b_s,
              g_out_att, g_out_gm, w_o, g_ffn, w_gate, w_up, w_down):
    past = cache_ckv.shape[2]
    pos_p = jnp.arange(x_prompt.shape[1])
    pos_s = past + jnp.arange(x_sample.shape[1])
    yp, ys = x_prompt, x_sample
    ckv_p, kr_p, ckv_s, kr_s, gv_s = [], [], [], [], []
    for l in range(DEPTH):
        p = {'g_mix': g_mix[l], 'w_in': w_in[l], 'g_q_lora': g_q_lora[l], 'w_q_up': w_q_up[l],
             'g_qn': g_qn[l], 'g_qr': g_qr[l], 'g_kv_lora': g_kv_lora[l], 'g_kr': g_kr[l],
             'w_kv_up': w_kv_up[l], 'g_kn': g_kn[l], 'gm_ln_g': gm_ln_g[l], 'gm_ln_b': gm_ln_b[l],
             'gm_w_s': gm_w_s[l], 'gm_b_s': gm_b_s[l], 'g_out_att': g_out_att[l],
             'g_out_gm': g_out_gm[l], 'w_o': w_o[l], 'g_ffn': g_ffn[l], 'w_gate': w_gate[l],
             'w_up': w_up[l], 'w_down': w_down[l]}
        yp, c, r, _ = hybrid_layer(yp, pos_p, None, None, p)
        ckv_p.append(c)
        kr_p.append(r)
        ys, c, r, gv = hybrid_layer(ys, pos_s, cache_ckv[l], cache_krope[l], p)
        ckv_s.append(c)
        kr_s.append(r)
        gv_s.append(gv)
    return (yp, ys, jnp.stack(ckv_p), jnp.stack(kr_p), jnp.stack(ckv_s), jnp.stack(kr_s), jnp.stack(gv_s))
```

```python
import functools

import numpy as np
import jax
import jax.numpy as jnp
from jax import lax
from jax.experimental import pallas as pl
from jax.experimental.pallas import tpu as pltpu

F32 = jnp.float32
BF16 = jnp.bfloat16

CHUNK = 64
N_HEADS = 8
QK_NOPE = 64
QK_ROPE = 32
QK_HEAD = QK_NOPE + QK_ROPE
V_HEAD = 64
Q_LORA = 256
KV_LORA = 128
ROPE_THETA = 10000.0
GMLP_GROUPS = 8
GMLP_GROUP_DIM = 64
GMLP_WIDTH = GMLP_GROUPS * GMLP_GROUP_DIM
GMLP_CHUNK = 128
EPS = 1e-6
NEG_INF = -1e30

LANES = 128
V7X_VMEM_BYTES = 64 * 1024 * 1024
VMEM_LIMIT_BYTES = (V7X_VMEM_BYTES * 3) // 4

HALF = QK_ROPE // 2
IN_COLS_PADDED = Q_LORA + KV_LORA + LANES + 2 * GMLP_WIDTH
COL_CKV = Q_LORA
COL_KR = Q_LORA + KV_LORA
COL_U = COL_KR + LANES
COL_V = COL_U + GMLP_WIDTH

_NT = (((1,), (1,)), ((), ()))


def _dot(a, b):
    return jnp.dot(a, b, preferred_element_type=F32)


def _dot_nt(a, b):
    return lax.dot_general(a, b, _NT, preferred_element_type=F32)


def _rms_scale(x, n):
    return lax.rsqrt(jnp.sum(x * x, axis=-1, keepdims=True) * (1.0 / n) + EPS)


def _half_sums(x, lo):
    s_lo = jnp.sum(jnp.where(lo, x, 0.0), axis=-1, keepdims=True)
    s_hi = jnp.sum(jnp.where(lo, 0.0, x), axis=-1, keepdims=True)
    return jnp.where(lo, s_lo, s_hi)


def _half_rms_scale(x, lo):
    return lax.rsqrt(_half_sums(x * x, lo) * (1.0 / 64) + EPS)


def _gelu_tanh(t):
    return t * (0.5 * (1.0 + jnp.tanh(0.7978845608028654 * (t + 0.044715 * (t * t * t)))))


def _inproj_kernel(x_ref, gmix_ref, win_ref, gql_ref, wq_ref, gq_ref, tq_ref, gkv_ref, gkr_ref, tk_ref,
                   wkk_ref, gkn_ref, wkv_ref, lng_ref, lnb_ref, ws_ref, bs_ref, ggm_ref,
                   q_out, k_out, v_out, ckv_out, kr_out, ogm_out, *vn_out, ws_mask):
    tm = x_ref.shape[0]
    x = x_ref[...]
    h = (x * _rms_scale(x, x.shape[1]) * gmix_ref[...]).astype(BF16)
    z = _dot(h, win_ref[...])

    lane = lax.broadcasted_iota(jnp.int32, (tm, LANES), 1)
    lo = lane < 64

    ql = z[:, 0:Q_LORA]
    qn = (ql * _rms_scale(ql, Q_LORA) * gql_ref[...]).astype(BF16)
    qz = _dot(qn, wq_ref[...])
    tq = tq_ref[...] * gq_ref[...]
    in_rope = jnp.logical_and(lane >= 64, lane < 96)
    for hd in range(N_HEADS):
        xh = qz[:, hd * LANES:(hd + 1) * LANES]
        x2 = xh * xh
        s_n = jnp.sum(jnp.where(lo, x2, 0.0), axis=-1, keepdims=True) * (1.0 / QK_NOPE)
        s_r = jnp.sum(jnp.where(in_rope, x2, 0.0), axis=-1, keepdims=True) * (1.0 / QK_ROPE)
        rr = jnp.where(lo, lax.rsqrt(s_n + EPS), lax.rsqrt(s_r + EPS))
        q_out[:, hd * LANES:(hd + 1) * LANES] = (xh * rr * tq).astype(BF16)

    y0 = z[:, COL_KR:COL_KR + LANES]
    s_k = jnp.sum(jnp.where(lane < QK_ROPE, y0 * y0, 0.0), axis=-1, keepdims=True) * (1.0 / QK_ROPE)
    y = y0 * lax.rsqrt(s_k + EPS) * (gkr_ref[...] * tk_ref[...])
    r64 = pltpu.roll(y, 64, 1)
    r32 = pltpu.roll(y, 32, 1)
    r96 = pltpu.roll(y, 96, 1)
    kr_out[...] = (y + r96)[:, 0:QK_ROPE]
    kr_hi = jnp.where(lo, 0.0, r64 + r32 + r96)

    c0 = z[:, COL_CKV:COL_CKV + KV_LORA]
    cn = c0 * _rms_scale(c0, KV_LORA) * gkv_ref[...]
    ckv_out[...] = cn
    cb = cn.astype(BF16)
    kz = _dot(cb, wkk_ref[...])
    for hd in range(N_HEADS):
        xh = kz[:, hd * LANES:(hd + 1) * LANES]
        kn = xh * _rms_scale(xh, QK_NOPE) * gkn_ref[...]
        k_out[:, hd * LANES:(hd + 1) * LANES] = jnp.where(lo, kn, kr_hi).astype(BF16)
    v_out[...] = _dot(cb, wkv_ref[...]).astype(BF16)

    row = lax.broadcasted_iota(jnp.int32, (GMLP_CHUNK, GMLP_CHUNK), 0) // CHUNK
    col = lax.broadcasted_iota(jnp.int32, (GMLP_CHUNK, GMLP_CHUNK), 1) // CHUNK
    keep = (row >= col) if ws_mask == "causal" else (row == col)
    lo_c = lax.broadcasted_iota(jnp.int32, (GMLP_CHUNK, LANES), 1) < 64
    for p in range(GMLP_GROUPS // 2):
        sl = slice(p * LANES, (p + 1) * LANES)
        w_pair = jnp.concatenate(
            [jnp.where(keep, ws_ref[2 * p], 0.0), jnp.where(keep, ws_ref[2 * p + 1], 0.0)], axis=0).astype(BF16)
        ug = _gelu_tanh(z[:, COL_U + p * LANES:COL_U + (p + 1) * LANES])
        vg = _gelu_tanh(z[:, COL_V + p * LANES:COL_V + (p + 1) * LANES])
        xc = vg - _half_sums(vg, lo) * (1.0 / GMLP_GROUP_DIM)
        vn = xc * _half_rms_scale(xc, lo) * lng_ref[:, sl] + lnb_ref[:, sl]
        if vn_out:
            vn_out[0][:, sl] = vn
        vnb = vn.astype(BF16)
        for c in range(tm // GMLP_CHUNK):
            rows = slice(c * GMLP_CHUNK, (c + 1) * GMLP_CHUNK)
            res = _dot(w_pair, vnb[rows])
            mixed = jnp.where(lo_c, res[0:GMLP_CHUNK], res[GMLP_CHUNK:]) + bs_ref[:, sl]
            og = ug[rows] * mixed
            ogm_out[rows, sl] = (og * _half_rms_scale(og, lo_c) * ggm_ref[:, sl]).astype(BF16)


def _const_spec(shape):
    return pl.BlockSpec(shape, lambda *_: (0,) * len(shape), pipeline_mode=pl.Buffered(1))


def _inproj(x, lw, tq, tk, ws, bs, *, tm, table_blocks, ws_mask, want_vn):
    n, d = x.shape
    grid = (n // tm,)
    tbl = lambda i: (i % table_blocks, 0)
    row = lambda i: (i, 0)
    in_specs = [
        pl.BlockSpec((tm, d), row),
        _const_spec((1, d)),
        _const_spec((d, IN_COLS_PADDED)),
        _const_spec((1, Q_LORA)),
        _const_spec((Q_LORA, N_HEADS * LANES)),
        _const_spec((1, LANES)),
        pl.BlockSpec((tm, LANES), tbl),
        _const_spec((1, KV_LORA)),
        _const_spec((1, LANES)),
        pl.BlockSpec((tm, LANES), tbl),
        _const_spec((KV_LORA, N_HEADS * LANES)),
        _const_spec((1, LANES)),
        _const_spec((KV_LORA, N_HEADS * V_HEAD)),
        _const_spec((1, GMLP_WIDTH)),
        _const_spec((1, GMLP_WIDTH)),
        _const_spec((GMLP_GROUPS, GMLP_CHUNK, GMLP_CHUNK)),
        _const_spec((GMLP_CHUNK, GMLP_WIDTH)),
        _const_spec((1, GMLP_WIDTH)),
    ]
    out_shape = [
        jax.ShapeDtypeStruct((n, N_HEADS * LANES), BF16),
        jax.ShapeDtypeStruct((n, N_HEADS * LANES), BF16),
        jax.ShapeDtypeStruct((n, N_HEADS * V_HEAD), BF16),
        jax.ShapeDtypeStruct((n, KV_LORA), F32),
        jax.ShapeDtypeStruct((n, QK_ROPE), F32),
        jax.ShapeDtypeStruct((n, GMLP_WIDTH), BF16),
    ]
    out_specs = [
        pl.BlockSpec((tm, N_HEADS * LANES), row),
        pl.BlockSpec((tm, N_HEADS * LANES), row),
        pl.BlockSpec((tm, N_HEADS * V_HEAD), row),
        pl.BlockSpec((tm, KV_LORA), row),
        pl.BlockSpec((tm, QK_ROPE), row),
        pl.BlockSpec((tm, GMLP_WIDTH), row),
    ]
    if want_vn:
        out_shape.append(jax.ShapeDtypeStruct((n, GMLP_WIDTH), F32))
        out_specs.append(pl.BlockSpec((tm, GMLP_WIDTH), row))
    return pl.pallas_call(
        functools.partial(_inproj_kernel, ws_mask=ws_mask),
        grid=grid, in_specs=in_specs, out_specs=out_specs, out_shape=out_shape,
        compiler_params=pltpu.CompilerParams(dimension_semantics=("arbitrary",),
                                             vmem_limit_bytes=VMEM_LIMIT_BYTES),
        name="inproj_vn" if want_vn else "inproj",
    )(x, lw["g_mix"], lw["w_in"], lw["g_q_lora"], lw["w_q"], lw["gq"], tq, lw["g_kv_lora"], lw["gkr"], tk,
      lw["w_kvk"], lw["gkn"], lw["w_kvv"], lw["ln_g"], lw["ln_b"], ws, bs, lw["g_out_gm"])


def _softmax_pv(q, k_diag, v_diag, bias, k_off, v_off):
    s_d = _dot_nt(q, k_diag) + bias
    m = jnp.max(s_d, axis=-1, keepdims=True)
    if k_off is not None:
        s_o = _dot_nt(q, k_off)
        m = jnp.maximum(m, jnp.max(s_o, axis=-1, keepdims=True))
    p_d = jnp.exp(s_d - m)
    l = jnp.sum(p_d, axis=-1, keepdims=True)
    acc = _dot(p_d.astype(BF16), v_diag)
    if k_off is not None:
        p_o = jnp.exp(s_o - m)
        l = l + jnp.sum(p_o, axis=-1, keepdims=True)
        acc = acc + _dot(p_o.astype(BF16), v_off)
    return acc, l


def _attn_prompt_kernel(q_ref, k_ref, v_ref, bias_ref, gout_ref, o_ref, *, tq):
    s_len = q_ref.shape[0]
    lane = lax.broadcasted_iota(jnp.int32, (tq, LANES), 1)
    lo = lane < 64
    bias = bias_ref[...]
    for qi in range(s_len // tq):
        r0 = qi * tq
        rows = slice(r0, r0 + tq)
        outs = []
        for hh in range(2):
            cols = slice(hh * LANES, (hh + 1) * LANES)
            k_off = k_ref[0:r0, cols] if qi > 0 else None
            v_off = v_ref[0:r0, :] if qi > 0 else None
            acc, l = _softmax_pv(q_ref[rows, cols], k_ref[rows, cols], v_ref[rows, :], bias, k_off, v_off)
            outs.append(acc * (1.0 / l))
        o = jnp.where(lo, outs[0], outs[1])
        o_ref[rows, :] = (o * _half_rms_scale(o, lo) * gout_ref[...]).astype(BF16)


def _attn_prompt(q, k, v, bias, gout, *, batch, seq, tq):
    n = q.shape[0]
    pairs = N_HEADS // 2
    return pl.pallas_call(
        functools.partial(_attn_prompt_kernel, tq=tq),
        grid=(batch, pairs),
        in_specs=[
            pl.BlockSpec((seq, 2 * LANES), lambda b, p: (b, p)),
            pl.BlockSpec((seq, 2 * LANES), lambda b, p: (b, p)),
            pl.BlockSpec((seq, 2 * V_HEAD), lambda b, p: (b, p)),
            pl.BlockSpec((tq, tq), lambda b, p: (0, 0)),
            pl.BlockSpec((1, 2 * V_HEAD), lambda b, p: (0, p)),
        ],
        out_specs=pl.BlockSpec((seq, 2 * V_HEAD), lambda b, p: (b, p)),
        out_shape=jax.ShapeDtypeStruct((n, N_HEADS * V_HEAD), BF16),
        compiler_params=pltpu.CompilerParams(dimension_semantics=("arbitrary", "arbitrary"),
                                             vmem_limit_bytes=VMEM_LIMIT_BYTES),
        name="attn_prompt",
    )(q, k, v, bias, gout)


def _attn_sample_kernel(q_ref, kn_ref, vn_ref, ckv_ref, krp_ref, wkk_ref, gkn_ref, wkv_ref, bias_ref, gout_ref,
                        o_ref):
    t = ckv_ref.shape[0]
    sq = q_ref.shape[0]
    cb = ckv_ref[...].astype(BF16)
    kz = _dot(cb, wkk_ref[...])
    v_past = _dot(cb, wkv_ref[...]).astype(BF16)
    lo_t = lax.broadcasted_iota(jnp.int32, (t, LANES), 1) < 64
    lo = lax.broadcasted_iota(jnp.int32, (sq, LANES), 1) < 64
    outs = []
    for hh in range(2):
        cols = slice(hh * LANES, (hh + 1) * LANES)
        xh = kz[:, cols]
        kn = (xh * _rms_scale(xh, QK_NOPE) * gkn_ref[...]).astype(BF16)
        k_past = jnp.where(lo_t, kn, krp_ref[...])
        acc, l = _softmax_pv(q_ref[:, cols], kn_ref[:, cols], vn_ref[...], bias_ref[...], k_past, v_past)
        outs.append(acc * (1.0 / l))
    o = jnp.where(lo, outs[0], outs[1])
    o_ref[...] = (o * _half_rms_scale(o, lo) * gout_ref[...]).astype(BF16)


def _attn_sample(q, k, v, cache_ckv, krp, w_kvk, gkn, w_kvv, bias, gout, *, layer, batch, seq):
    n = q.shape[0]
    past = cache_ckv.shape[2]
    pairs = N_HEADS // 2
    return pl.pallas_call(
        _attn_sample_kernel,
        grid=(batch, pairs),
        in_specs=[
            pl.BlockSpec((seq, 2 * LANES), lambda b, p: (b, p)),
            pl.BlockSpec((seq, 2 * LANES), lambda b, p: (b, p)),
            pl.BlockSpec((seq, 2 * V_HEAD), lambda b, p: (b, p)),
            pl.BlockSpec((None, None, past, KV_LORA), lambda b, p: (layer, b, 0, 0)),
            pl.BlockSpec((None, past, LANES), lambda b, p: (b, 0, 0)),
            pl.BlockSpec((KV_LORA, 2 * LANES), lambda b, p: (0, p)),
            pl.BlockSpec((1, LANES), lambda b, p: (0, 0)),
            pl.BlockSpec((KV_LORA, 2 * V_HEAD), lambda b, p: (0, p)),
            pl.BlockSpec((seq, seq), lambda b, p: (0, 0)),
            pl.BlockSpec((1, 2 * V_HEAD), lambda b, p: (0, p)),
        ],
        out_specs=pl.BlockSpec((seq, 2 * V_HEAD), lambda b, p: (b, p)),
        out_shape=jax.ShapeDtypeStruct((n, N_HEADS * V_HEAD), BF16),
        compiler_params=pltpu.CompilerParams(dimension_semantics=("arbitrary", "arbitrary"),
                                             vmem_limit_bytes=VMEM_LIMIT_BYTES),
        name="attn_sample",
    )(q, k, v, cache_ckv, krp, w_kvk, gkn, w_kvv, bias, gout)


def _ffn_kernel(x_ref, oa_ref, og_ref, wo_ref, gffn_ref, wg_ref, wu_ref, wd_ref, y_ref, *, fchunk):
    aw = oa_ref.shape[1]
    x1 = x_ref[...] + _dot(oa_ref[...], wo_ref[0:aw, :]) + _dot(og_ref[...], wo_ref[aw:, :])
    h2 = (x1 * _rms_scale(x1, x1.shape[1]) * gffn_ref[...]).astype(BF16)
    acc = None
    for c in range(0, wg_ref.shape[1], fchunk):
        g = _dot(h2, wg_ref[:, c:c + fchunk])
        u = _dot(h2, wu_ref[:, c:c + fchunk])
        a = (g * jax.nn.sigmoid(g) * u).astype(BF16)
        d = _dot(a, wd_ref[c:c + fchunk, :])
        acc = d if acc is None else acc + d
    y_ref[...] = x1 + acc


def _ffn(x, oa, og, lw, *, tm, fchunk):
    n, d = x.shape
    dff = lw["w_gate"].shape[1]
    row = lambda i: (i, 0)
    return pl.pallas_call(
        functools.partial(_ffn_kernel, fchunk=fchunk),
        grid=(n // tm,),
        in_specs=[
            pl.BlockSpec((tm, d), row),
            pl.BlockSpec((tm, oa.shape[1]), row),
            pl.BlockSpec((tm, og.shape[1]), row),
            _const_spec((oa.shape[1] + og.shape[1], d)),
            _const_spec((1, d)),
            _const_spec((d, dff)),
            _const_spec((d, dff)),
            _const_spec((dff, d)),
        ],
        out_specs=pl.BlockSpec((tm, d), row),
        out_shape=jax.ShapeDtypeStruct((n, d), F32),
        compiler_params=pltpu.CompilerParams(dimension_semantics=("arbitrary",),
                                             vmem_limit_bytes=VMEM_LIMIT_BYTES),
        name="outproj_ffn",
    )(x, oa, og, lw["w_o"], lw["g_ffn"], lw["w_gate"], lw["w_up"], lw["w_down"])


def _rope_tables(pos):
    inv = ROPE_THETA ** (-jnp.arange(HALF, dtype=F32) / HALF)
    ang = pos.astype(F32)[:, None] * inv[None, :]
    cos, sin = jnp.cos(ang), jnp.sin(ang)
    n = pos.shape[0]
    tq = jnp.concatenate([jnp.ones((n, QK_NOPE), F32), cos, cos, -sin, sin], axis=1)
    tk = jnp.concatenate([cos, cos, -sin, sin, jnp.zeros((n, LANES - 2 * QK_ROPE), F32)], axis=1)
    return tq, tk


def _swap_halves(g):
    return jnp.concatenate([g[..., HALF:], g[..., :HALF]], axis=-1)


def _layer_weights(l, g_mix, w_in, g_q_lora, w_q_up, g_qn, g_qr, g_kv_lora, g_kr, w_kv_up, g_kn,
                   gm_ln_g, gm_ln_b, g_out_att, g_out_gm, w_o, g_ffn, w_gate, w_up, w_down):
    d = w_in.shape[1]
    w = w_in[l]
    kr0 = Q_LORA + KV_LORA
    w_in_p = jnp.concatenate(
        [w[:, :kr0 + QK_ROPE], w[:, kr0 + HALF:kr0 + QK_ROPE], w[:, kr0:kr0 + HALF],
         jnp.zeros((d, LANES - 2 * QK_ROPE), F32), w[:, kr0 + QK_ROPE:]], axis=1).astype(BF16)
    wq = w_q_up[l].reshape(Q_LORA, N_HEADS, QK_HEAD)
    r1, r2 = wq[..., QK_NOPE:QK_NOPE + HALF], wq[..., QK_NOPE + HALF:]
    w_q = jnp.concatenate([wq[..., :QK_NOPE], r1, r2, r2, r1], axis=-1).reshape(Q_LORA, N_HEADS * LANES)
    gq = jnp.concatenate([g_qn[l], g_qr[l], _swap_halves(g_qr[l])]) * (QK_HEAD ** -0.5)
    gkr = jnp.concatenate([g_kr[l], _swap_halves(g_kr[l]), jnp.zeros((LANES - 2 * QK_ROPE,), F32)])
    wkv = w_kv_up[l].reshape(KV_LORA, N_HEADS, QK_NOPE + V_HEAD)
    w_kvk = jnp.concatenate([wkv[..., :QK_NOPE], jnp.zeros((KV_LORA, N_HEADS, LANES - QK_NOPE), F32)],
                            axis=-1).reshape(KV_LORA, N_HEADS * LANES)
    w_kvv = wkv[..., QK_NOPE:].reshape(KV_LORA, N_HEADS * V_HEAD)
    gkn = jnp.concatenate([g_kn[l], jnp.zeros((LANES - QK_NOPE,), F32)])
    return {
        "g_mix": g_mix[l][None, :], "w_in": w_in_p, "g_q_lora": g_q_lora[l][None, :],
        "w_q": w_q.astype(BF16), "gq": gq[None, :], "g_kv_lora": g_kv_lora[l][None, :], "gkr": gkr[None, :],
        "w_kvk": w_kvk.astype(BF16), "gkn": gkn[None, :], "w_kvv": w_kvv.astype(BF16),
        "ln_g": gm_ln_g[l].reshape(1, GMLP_WIDTH), "ln_b": gm_ln_b[l].reshape(1, GMLP_WIDTH),
        "g_out_att": g_out_att[l].reshape(1, N_HEADS * V_HEAD), "g_out_gm": g_out_gm[l].reshape(1, GMLP_WIDTH),
        "w_o": w_o[l].astype(BF16), "g_ffn": g_ffn[l][None, :],
        "w_gate": w_gate[l].astype(BF16), "w_up": w_up[l].astype(BF16), "w_down": w_down[l].astype(BF16),
    }


def _chunk_bias(qpos, kpos):
    vis = (kpos[None, :] // CHUNK) <= (qpos[:, None] // CHUNK)
    return jnp.asarray(np.where(vis, 0.0, NEG_INF), F32)


def kernel(x_prompt, x_sample, cache_ckv, cache_krope, g_mix, w_in, g_q_lora, w_q_up, g_qn, g_qr, g_kv_lora, g_kr,
           w_kv_up, g_kn, gm_ln_g, gm_ln_b, gm_w_s, gm_b_s, g_out_att, g_out_gm, w_o, g_ffn, w_gate, w_up, w_down):
    batch, seq, d = x_prompt.shape
    dbatch, dseq, _ = x_sample.shape
    depth, _, past, _ = cache_ckv.shape
    assert seq % GMLP_CHUNK == 0 and GMLP_CHUNK % dseq == 0 and (dbatch * dseq) % GMLP_CHUNK == 0

    tq_attn = 256
    tm_p = 512
    tm_s = dbatch * dseq

    tq_p, tk_p = _rope_tables(jnp.arange(seq))
    tq_s, tk_s = _rope_tables(past + jnp.arange(dseq))
    tq_s, tk_s = jnp.tile(tq_s, (dbatch, 1)), jnp.tile(tk_s, (dbatch, 1))
    bias_p = _chunk_bias(np.arange(tq_attn), np.arange(tq_attn))
    bias_s = _chunk_bias(past + np.arange(dseq), past + np.arange(dseq))
    reps = GMLP_CHUNK // dseq

    yp = x_prompt.reshape(batch * seq, d)
    ys = x_sample.reshape(dbatch * dseq, d)
    ckv_p, kr_p, ckv_s, kr_s, gv_s = [], [], [], [], []
    for l in range(depth):
        lw = _layer_weights(l, g_mix, w_in, g_q_lora, w_q_up, g_qn, g_qr, g_kv_lora, g_kr, w_kv_up, g_kn,
                            gm_ln_g, gm_ln_b, g_out_att, g_out_gm, w_o, g_ffn, w_gate, w_up, w_down)
        ws_p = gm_w_s[l]
        bs_p = jnp.repeat(gm_b_s[l].T, GMLP_GROUP_DIM, axis=1)
        ws_s = jnp.tile(gm_w_s[l][:, :dseq, :dseq], (1, reps, reps))
        bs_s = jnp.tile(jnp.repeat(gm_b_s[l][:, :dseq].T, GMLP_GROUP_DIM, axis=1), (reps, 1))
        krp = jnp.concatenate([jnp.zeros((dbatch, past, QK_NOPE), BF16), cache_krope[l].astype(BF16),
                               cache_krope[l].astype(BF16)], axis=-1)

        q, k, v, c, r, ogm = _inproj(yp, lw, tq_p, tk_p, ws_p, bs_p, tm=tm_p, table_blocks=seq // tm_p,
                                     ws_mask="causal", want_vn=False)
        oatt = _attn_prompt(q, k, v, bias_p, lw["g_out_att"], batch=batch, seq=seq, tq=tq_attn)
        yp = _ffn(yp, oatt, ogm, lw, tm=tm_p, fchunk=256)
        ckv_p.append(c.reshape(batch, seq, KV_LORA))
        kr_p.append(r.reshape(batch, seq, QK_ROPE))

        q, k, v, c, r, ogm, vn = _inproj(ys, lw, tq_s, tk_s, ws_s, bs_s, tm=tm_s, table_blocks=1,
                                         ws_mask="blockdiag", want_vn=True)
        oatt = _attn_sample(q, k, v, cache_ckv, krp, lw["w_kvk"], lw["gkn"], lw["w_kvv"], bias_s,
                            lw["g_out_att"], layer=l, batch=dbatch, seq=dseq)
        ys = _ffn(ys, oatt, ogm, lw, tm=tm_s, fchunk=256)
        ckv_s.append(c.reshape(dbatch, dseq, KV_LORA))
        kr_s.append(r.reshape(dbatch, dseq, QK_ROPE))
        gv_s.append(vn.reshape(dbatch, dseq, GMLP_WIDTH))

    return (yp.reshape(batch, seq, d), ys.reshape(dbatch, dseq, d), jnp.stack(ckv_p), jnp.stack(kr_p),
            jnp.stack(ckv_s), jnp.stack(kr_s), jnp.stack(gv_s))
```

```python
import functools

import numpy as np
import jax
import jax.numpy as jnp
from jax import lax
from jax.experimental import pallas as pl
from jax.experimental.pallas import tpu as pltpu

F32 = jnp.float32
BF16 = jnp.bfloat16

CHUNK = 64
N_HEADS = 8
QK_NOPE = 64
QK_ROPE = 32
QK_HEAD = QK_NOPE + QK_ROPE
V_HEAD = 64
Q_LORA = 256
KV_LORA = 128
ROPE_THETA = 10000.0
GMLP_GROUPS = 8
GMLP_GROUP_DIM = 64
GMLP_WIDTH = GMLP_GROUPS * GMLP_GROUP_DIM
GMLP_CHUNK = 128
EPS = 1e-6
NEG_INF = -1e30
LOG2_E = 1.4426950408889634

LANES = 128
V7X_VMEM_BYTES = 64 * 1024 * 1024
VMEM_LIMIT_BYTES = (V7X_VMEM_BYTES * 3) // 4

HALF = QK_ROPE // 2
IN_COLS_PADDED = Q_LORA + KV_LORA + LANES + 2 * GMLP_WIDTH
COL_CKV = Q_LORA
COL_KR = Q_LORA + KV_LORA
COL_U = COL_KR + LANES
COL_V = COL_U + GMLP_WIDTH

_NT = (((1,), (1,)), ((), ()))


def _dot(a, b):
    return jnp.dot(a, b, preferred_element_type=F32)


def _dot_nt(a, b):
    return lax.dot_general(a, b, _NT, preferred_element_type=F32)


def _rms_scale(x, n):
    return lax.rsqrt(jnp.sum(x * x, axis=-1, keepdims=True) * (1.0 / n) + EPS)


def _half_sums(x, lo):
    s_lo = jnp.sum(jnp.where(lo, x, 0.0), axis=-1, keepdims=True)
    s_hi = jnp.sum(jnp.where(lo, 0.0, x), axis=-1, keepdims=True)
    return jnp.where(lo, s_lo, s_hi)


def _half_rms_scale(x, lo):
    return lax.rsqrt(_half_sums(x * x, lo) * (1.0 / 64) + EPS)


def _gelu_tanh(t):
    return t * (0.5 * (1.0 + jnp.tanh(0.7978845608028654 * (t + 0.044715 * (t * t * t)))))


def _inproj_kernel(x_ref, gmix_ref, win_ref, gql_ref, wq_ref, gq_ref, tq_ref, gkv_ref, gkr_ref, tk_ref,
                   wkk_ref, gkn_ref, wkv_ref, lng_ref, lnb_ref, ws_ref, bs_ref, ggm_ref,
                   q_out, k_out, v_out, ckv_out, kr_out, ogm_out, *vn_out, ws_mask, v_transposed):
    tm = x_ref.shape[0]
    x = x_ref[...]
    h = (x * _rms_scale(x, x.shape[1]) * gmix_ref[...]).astype(BF16)
    z = _dot(h, win_ref[...])

    lane = lax.broadcasted_iota(jnp.int32, (tm, LANES), 1)
    lo = lane < 64

    ql = z[:, 0:Q_LORA]
    qn = (ql * _rms_scale(ql, Q_LORA) * gql_ref[...]).astype(BF16)
    qz = _dot(qn, wq_ref[...])
    tq = tq_ref[...] * gq_ref[...]
    in_rope = jnp.logical_and(lane >= 64, lane < 96)
    for hd in range(N_HEADS):
        xh = qz[:, hd * LANES:(hd + 1) * LANES]
        x2 = xh * xh
        s_n = jnp.sum(jnp.where(lo, x2, 0.0), axis=-1, keepdims=True) * (1.0 / QK_NOPE)
        s_r = jnp.sum(jnp.where(in_rope, x2, 0.0), axis=-1, keepdims=True) * (1.0 / QK_ROPE)
        rr = jnp.where(lo, lax.rsqrt(s_n + EPS), lax.rsqrt(s_r + EPS))
        q_out[:, hd * LANES:(hd + 1) * LANES] = (xh * rr * tq).astype(BF16)

    y0 = z[:, COL_KR:COL_KR + LANES]
    s_k = jnp.sum(jnp.where(lane < QK_ROPE, y0 * y0, 0.0), axis=-1, keepdims=True) * (1.0 / QK_ROPE)
    y = y0 * lax.rsqrt(s_k + EPS) * (gkr_ref[...] * tk_ref[...])
    r64 = pltpu.roll(y, 64, 1)
    r32 = pltpu.roll(y, 32, 1)
    r96 = pltpu.roll(y, 96, 1)
    kr_out[...] = (y + r96)[:, 0:QK_ROPE]
    kr_hi = jnp.where(lo, 0.0, r64 + r32 + r96)

    c0 = z[:, COL_CKV:COL_CKV + KV_LORA]
    cn = c0 * _rms_scale(c0, KV_LORA) * gkv_ref[...]
    ckv_out[...] = cn
    cb = cn.astype(BF16)
    kz = _dot(cb, wkk_ref[...])
    for hd in range(N_HEADS):
        xh = kz[:, hd * LANES:(hd + 1) * LANES]
        kn = xh * _rms_scale(xh, QK_NOPE) * gkn_ref[...]
        k_out[:, hd * LANES:(hd + 1) * LANES] = jnp.where(lo, kn, kr_hi).astype(BF16)
    if v_transposed:
        vt = _dot_nt(wkv_ref[...], cb)
        row = lax.broadcasted_iota(jnp.int32, vt.shape, 0)
        v_out[...] = jnp.where((row & (LANES - 1)) == V_HEAD, 1.0, vt).astype(BF16)
    else:
        v_out[...] = _dot(cb, wkv_ref[...]).astype(BF16)

    row = lax.broadcasted_iota(jnp.int32, (GMLP_CHUNK, GMLP_CHUNK), 0) // CHUNK
    col = lax.broadcasted_iota(jnp.int32, (GMLP_CHUNK, GMLP_CHUNK), 1) // CHUNK
    keep = (row >= col) if ws_mask == "causal" else (row == col)
    lo_c = lax.broadcasted_iota(jnp.int32, (GMLP_CHUNK, LANES), 1) < 64
    for p in range(GMLP_GROUPS // 2):
        sl = slice(p * LANES, (p + 1) * LANES)
        w_pair = jnp.concatenate(
            [jnp.where(keep, ws_ref[2 * p], 0.0), jnp.where(keep, ws_ref[2 * p + 1], 0.0)], axis=0).astype(BF16)
        ug = _gelu_tanh(z[:, COL_U + p * LANES:COL_U + (p + 1) * LANES])
        vg = _gelu_tanh(z[:, COL_V + p * LANES:COL_V + (p + 1) * LANES])
        xc = vg - _half_sums(vg, lo) * (1.0 / GMLP_GROUP_DIM)
        vn = xc * _half_rms_scale(xc, lo) * lng_ref[:, sl] + lnb_ref[:, sl]
        if vn_out:
            vn_out[0][:, sl] = vn
        vnb = vn.astype(BF16)
        for c in range(tm // GMLP_CHUNK):
            rows = slice(c * GMLP_CHUNK, (c + 1) * GMLP_CHUNK)
            res = _dot(w_pair, vnb[rows])
            mixed = jnp.where(lo_c, res[0:GMLP_CHUNK], res[GMLP_CHUNK:]) + bs_ref[:, sl]
            og = ug[rows] * mixed
            ogm_out[rows, sl] = (og * _half_rms_scale(og, lo_c) * ggm_ref[:, sl]).astype(BF16)


def _const_spec(shape):
    return pl.BlockSpec(shape, lambda *_: (0,) * len(shape), pipeline_mode=pl.Buffered(1))


def _inproj(x, lw, tq, tk, ws, bs, *, tm, table_blocks, ws_mask, want_vn, v_transposed):
    n, d = x.shape
    w_v = lw["w_kvv_t"] if v_transposed else lw["w_kvv"]
    grid = (n // tm,)
    tbl = lambda i: (i % table_blocks, 0)
    row = lambda i: (i, 0)
    in_specs = [
        pl.BlockSpec((tm, d), row),
        _const_spec((1, d)),
        _const_spec((d, IN_COLS_PADDED)),
        _const_spec((1, Q_LORA)),
        _const_spec((Q_LORA, N_HEADS * LANES)),
        _const_spec((1, LANES)),
        pl.BlockSpec((tm, LANES), tbl),
        _const_spec((1, KV_LORA)),
        _const_spec((1, LANES)),
        pl.BlockSpec((tm, LANES), tbl),
        _const_spec((KV_LORA, N_HEADS * LANES)),
        _const_spec((1, LANES)),
        _const_spec(w_v.shape),
        _const_spec((1, GMLP_WIDTH)),
        _const_spec((1, GMLP_WIDTH)),
        _const_spec((GMLP_GROUPS, GMLP_CHUNK, GMLP_CHUNK)),
        _const_spec((GMLP_CHUNK, GMLP_WIDTH)),
        _const_spec((1, GMLP_WIDTH)),
    ]
    out_shape = [
        jax.ShapeDtypeStruct((n, N_HEADS * LANES), BF16),
        jax.ShapeDtypeStruct((n, N_HEADS * LANES), BF16),
        (jax.ShapeDtypeStruct((N_HEADS * LANES, n), BF16) if v_transposed
         else jax.ShapeDtypeStruct((n, N_HEADS * V_HEAD), BF16)),
        jax.ShapeDtypeStruct((n, KV_LORA), F32),
        jax.ShapeDtypeStruct((n, QK_ROPE), F32),
        jax.ShapeDtypeStruct((n, GMLP_WIDTH), BF16),
    ]
    out_specs = [
        pl.BlockSpec((tm, N_HEADS * LANES), row),
        pl.BlockSpec((tm, N_HEADS * LANES), row),
        (pl.BlockSpec((N_HEADS * LANES, tm), lambda i: (0, i)) if v_transposed
         else pl.BlockSpec((tm, N_HEADS * V_HEAD), row)),
        pl.BlockSpec((tm, KV_LORA), row),
        pl.BlockSpec((tm, QK_ROPE), row),
        pl.BlockSpec((tm, GMLP_WIDTH), row),
    ]
    if want_vn:
        out_shape.append(jax.ShapeDtypeStruct((n, GMLP_WIDTH), F32))
        out_specs.append(pl.BlockSpec((tm, GMLP_WIDTH), row))
    return pl.pallas_call(
        functools.partial(_inproj_kernel, ws_mask=ws_mask, v_transposed=v_transposed),
        grid=grid, in_specs=in_specs, out_specs=out_specs, out_shape=out_shape,
        compiler_params=pltpu.CompilerParams(dimension_semantics=("arbitrary",),
                                             vmem_limit_bytes=VMEM_LIMIT_BYTES),
        name="inproj_vn" if want_vn else "inproj",
    )(x, lw["g_mix"], lw["w_in"], lw["g_q_lora"], lw["w_q"], lw["gq"], tq, lw["g_kv_lora"], lw["gkr"], tk,
      lw["w_kvk"], lw["gkn"], w_v, lw["ln_g"], lw["ln_b"], ws, bs, lw["g_out_gm"])


def _softmax_pv(q, k_diag, v_diag, bias, k_off, v_off):
    s_d = _dot_nt(q, k_diag) + bias
    m = jnp.max(s_d, axis=-1, keepdims=True)
    if k_off is not None:
        s_o = _dot_nt(q, k_off)
        m = jnp.maximum(m, jnp.max(s_o, axis=-1, keepdims=True))
    p_d = jnp.exp2(s_d - m)
    l = jnp.sum(p_d, axis=-1, keepdims=True)
    acc = _dot(p_d.astype(BF16), v_diag)
    if k_off is not None:
        p_o = jnp.exp2(s_o - m)
        l = l + jnp.sum(p_o, axis=-1, keepdims=True)
        acc = acc + _dot(p_o.astype(BF16), v_off)
    return acc, l


ATTN_STREAM_LAG = 18


def _attn_prompt_kernel(q_ref, k_ref, vt_ref, bias_ref, g_ref, o_ref, s_scr, *, tq):
    s_len = q_ref.shape[0]
    bodies = [(qi, hh) for qi in reversed(range(s_len // tq)) for hh in range(2)]
    bases = np.cumsum([0] + [qi + 1 for qi, _ in bodies])
    state = [{} for _ in bodies]

    def a_steps(bi):
        qi, hh = bodies[bi]
        cols = slice(hh * LANES, (hh + 1) * LANES)
        rows = slice(qi * tq, (qi + 1) * tq)
        st = state[bi]

        def chunk(j):
            s = _dot_nt(k_ref[j * tq:(j + 1) * tq, cols], q_ref[rows, cols])
            if j == qi:
                s = s + bias_ref[...]
            s_scr[bases[bi] + j] = s
            cm = jnp.max(s.reshape(tq // 8, 8, tq), axis=0)
            st["m_run"] = cm if j == 0 else jnp.maximum(st["m_run"], cm)

        def finish():
            st["m"] = jnp.max(st["m_run"], axis=0, keepdims=True)

        return [functools.partial(chunk, j) for j in range(qi + 1)] + [finish]

    def b_steps(bi):
        qi, hh = bodies[bi]
        st = state[bi]

        def chunk(j):
            p = jnp.exp2(s_scr[bases[bi] + j] - st["m"]).astype(BF16)
            d = _dot(vt_ref[hh * LANES:(hh + 1) * LANES, j * tq:(j + 1) * tq], p)
            st["acc"] = d if j == 0 else st["acc"] + d

        def finish():
            acc = st["acc"]
            o = acc[0:V_HEAD] * (1.0 / acc[V_HEAD:V_HEAD + 1])
            r = lax.rsqrt(jnp.sum(o * o, axis=0, keepdims=True) * (1.0 / V_HEAD) + EPS)
            st["out"] = o * r * g_ref[hh * V_HEAD:(hh + 1) * V_HEAD, :]
            if hh == 1:
                pair_t = jnp.concatenate([state[bi - 1]["out"], st["out"]], axis=0)
                o_ref[qi * tq:(qi + 1) * tq, :] = pair_t.T.astype(BF16)

        return [functools.partial(chunk, j) for j in range(qi + 1)] + [finish]

    a_all = [f for bi in range(len(bodies)) for f in a_steps(bi)]
    b_all = [f for bi in range(len(bodies)) for f in b_steps(bi)]
    for i in range(len(a_all) + ATTN_STREAM_LAG):
        if i < len(a_all):
            a_all[i]()
        if i >= ATTN_STREAM_LAG:
            b_all[i - ATTN_STREAM_LAG]()


def _attn_prompt(q, k, vt, bias_t, g_t, *, batch, seq, tq):
    n = q.shape[0]
    pairs = N_HEADS // 2
    nq = seq // tq
    assert ATTN_STREAM_LAG > nq + 1
    return pl.pallas_call(
        functools.partial(_attn_prompt_kernel, tq=tq),
        grid=(batch, pairs),
        in_specs=[
            pl.BlockSpec((seq, 2 * LANES), lambda b, p: (b, p)),
            pl.BlockSpec((seq, 2 * LANES), lambda b, p: (b, p)),
            pl.BlockSpec((2 * LANES, seq), lambda b, p: (p, b)),
            pl.BlockSpec((tq, tq), lambda b, p: (0, 0)),
            pl.BlockSpec((2 * V_HEAD, tq), lambda b, p: (p, 0)),
        ],
        out_specs=pl.BlockSpec((seq, 2 * V_HEAD), lambda b, p: (b, p)),
        out_shape=jax.ShapeDtypeStruct((n, N_HEADS * V_HEAD), BF16),
        scratch_shapes=[pltpu.VMEM((nq * (nq + 1), tq, tq), F32)],
        compiler_params=pltpu.CompilerParams(dimension_semantics=("arbitrary", "arbitrary"),
                                             vmem_limit_bytes=VMEM_LIMIT_BYTES),
        name="attn_prompt",
    )(q, k, vt, bias_t, g_t)


def _attn_sample_kernel(q_ref, kn_ref, vn_ref, ckv_ref, krp_ref, wkk_ref, gkn_ref, wkv_ref, bias_ref, gout_ref,
                        o_ref):
    t = ckv_ref.shape[0]
    sq = q_ref.shape[0]
    cb = ckv_ref[...].astype(BF16)
    kz = _dot(cb, wkk_ref[...])
    v_past = _dot(cb, wkv_ref[...]).astype(BF16)
    lo_t = lax.broadcasted_iota(jnp.int32, (t, LANES), 1) < 64
    lo = lax.broadcasted_iota(jnp.int32, (sq, LANES), 1) < 64
    outs = []
    for hh in range(2):
        cols = slice(hh * LANES, (hh + 1) * LANES)
        xh = kz[:, cols]
        kn = (xh * _rms_scale(xh, QK_NOPE) * gkn_ref[...]).astype(BF16)
        k_past = jnp.where(lo_t, kn, krp_ref[...])
        acc, l = _softmax_pv(q_ref[:, cols], kn_ref[:, cols], vn_ref[...], bias_ref[...], k_past, v_past)
        outs.append(acc * (1.0 / l))
    o = jnp.where(lo, outs[0], outs[1])
    o_ref[...] = (o * _half_rms_scale(o, lo) * gout_ref[...]).astype(BF16)


def _attn_sample(q, k, v, cache_ckv, krp, w_kvk, gkn, w_kvv, bias, gout, *, layer, batch, seq):
    n = q.shape[0]
    past = cache_ckv.shape[2]
    pairs = N_HEADS // 2
    return pl.pallas_call(
        _attn_sample_kernel,
        grid=(batch, pairs),
        in_specs=[
            pl.BlockSpec((seq, 2 * LANES), lambda b, p: (b, p)),
            pl.BlockSpec((seq, 2 * LANES), lambda b, p: (b, p)),
            pl.BlockSpec((seq, 2 * V_HEAD), lambda b, p: (b, p)),
            pl.BlockSpec((None, None, past, KV_LORA), lambda b, p: (layer, b, 0, 0)),
            pl.BlockSpec((None, past, LANES), lambda b, p: (b, 0, 0)),
            pl.BlockSpec((KV_LORA, 2 * LANES), lambda b, p: (0, p)),
            pl.BlockSpec((1, LANES), lambda b, p: (0, 0)),
            pl.BlockSpec((KV_LORA, 2 * V_HEAD), lambda b, p: (0, p)),
            pl.BlockSpec((seq, seq), lambda b, p: (0, 0)),
            pl.BlockSpec((1, 2 * V_HEAD), lambda b, p: (0, p)),
        ],
        out_specs=pl.BlockSpec((seq, 2 * V_HEAD), lambda b, p: (b, p)),
        out_shape=jax.ShapeDtypeStruct((n, N_HEADS * V_HEAD), BF16),
        compiler_params=pltpu.CompilerParams(dimension_semantics=("arbitrary", "arbitrary"),
                                             vmem_limit_bytes=VMEM_LIMIT_BYTES),
        name="attn_sample",
    )(q, k, v, cache_ckv, krp, w_kvk, gkn, w_kvv, bias, gout)


def _ffn_kernel(x_ref, oa_ref, og_ref, wo_ref, gffn_ref, wg_ref, wu_ref, wd_ref, y_ref, *, fchunk):
    aw = oa_ref.shape[1]
    x1 = x_ref[...] + _dot(oa_ref[...], wo_ref[0:aw, :]) + _dot(og_ref[...], wo_ref[aw:, :])
    h2 = (x1 * _rms_scale(x1, x1.shape[1]) * gffn_ref[...]).astype(BF16)
    acc = None
    for c in range(0, wg_ref.shape[1], fchunk):
        g = _dot(h2, wg_ref[:, c:c + fchunk])
        u = _dot(h2, wu_ref[:, c:c + fchunk])
        a = (g * jax.nn.sigmoid(g) * u).astype(BF16)
        d = _dot(a, wd_ref[c:c + fchunk, :])
        acc = d if acc is None else acc + d
    y_ref[...] = x1 + acc


def _ffn(x, oa, og, lw, *, tm, fchunk):
    n, d = x.shape
    dff = lw["w_gate"].shape[1]
    row = lambda i: (i, 0)
    return pl.pallas_call(
        functools.partial(_ffn_kernel, fchunk=fchunk),
        grid=(n // tm,),
        in_specs=[
            pl.BlockSpec((tm, d), row),
            pl.BlockSpec((tm, oa.shape[1]), row),
            pl.BlockSpec((tm, og.shape[1]), row),
            _const_spec((oa.shape[1] + og.shape[1], d)),
            _const_spec((1, d)),
            _const_spec((d, dff)),
            _const_spec((d, dff)),
            _const_spec((dff, d)),
        ],
        out_specs=pl.BlockSpec((tm, d), row),
        out_shape=jax.ShapeDtypeStruct((n, d), F32),
        compiler_params=pltpu.CompilerParams(dimension_semantics=("arbitrary",),
                                             vmem_limit_bytes=VMEM_LIMIT_BYTES),
        name="outproj_ffn",
    )(x, oa, og, lw["w_o"], lw["g_ffn"], lw["w_gate"], lw["w_up"], lw["w_down"])


def _rope_tables(pos):
    inv = ROPE_THETA ** (-jnp.arange(HALF, dtype=F32) / HALF)
    ang = pos.astype(F32)[:, None] * inv[None, :]
    cos, sin = jnp.cos(ang), jnp.sin(ang)
    n = pos.shape[0]
    tq = jnp.concatenate([jnp.ones((n, QK_NOPE), F32), cos, cos, -sin, sin], axis=1)
    tk = jnp.concatenate([cos, cos, -sin, sin, jnp.zeros((n, LANES - 2 * QK_ROPE), F32)], axis=1)
    return tq, tk


def _swap_halves(g):
    return jnp.concatenate([g[..., HALF:], g[..., :HALF]], axis=-1)


def _layer_weights(l, g_mix, w_in, g_q_lora, w_q_up, g_qn, g_qr, g_kv_lora, g_kr, w_kv_up, g_kn,
                   gm_ln_g, gm_ln_b, g_out_att, g_out_gm, w_o, g_ffn, w_gate, w_up, w_down):
    d = w_in.shape[1]
    w = w_in[l]
    kr0 = Q_LORA + KV_LORA
    w_in_p = jnp.concatenate(
        [w[:, :kr0 + QK_ROPE], w[:, kr0 + HALF:kr0 + QK_ROPE], w[:, kr0:kr0 + HALF],
         jnp.zeros((d, LANES - 2 * QK_ROPE), F32), w[:, kr0 + QK_ROPE:]], axis=1).astype(BF16)
    wq = w_q_up[l].reshape(Q_LORA, N_HEADS, QK_HEAD)
    r1, r2 = wq[..., QK_NOPE:QK_NOPE + HALF], wq[..., QK_NOPE + HALF:]
    w_q = jnp.concatenate([wq[..., :QK_NOPE], r1, r2, r2, r1], axis=-1).reshape(Q_LORA, N_HEADS * LANES)
    gq = jnp.concatenate([g_qn[l], g_qr[l], _swap_halves(g_qr[l])]) * (QK_HEAD ** -0.5 * LOG2_E)
    gkr = jnp.concatenate([g_kr[l], _swap_halves(g_kr[l]), jnp.zeros((LANES - 2 * QK_ROPE,), F32)])
    wkv = w_kv_up[l].reshape(KV_LORA, N_HEADS, QK_NOPE + V_HEAD)
    w_kvk = jnp.concatenate([wkv[..., :QK_NOPE], jnp.zeros((KV_LORA, N_HEADS, LANES - QK_NOPE), F32)],
                            axis=-1).reshape(KV_LORA, N_HEADS * LANES)
    w_kvv = wkv[..., QK_NOPE:].reshape(KV_LORA, N_HEADS * V_HEAD)
    w_kvv_t = jnp.concatenate([wkv[..., QK_NOPE:], jnp.zeros((KV_LORA, N_HEADS, LANES - V_HEAD), F32)],
                              axis=-1).reshape(KV_LORA, N_HEADS * LANES).T
    gkn = jnp.concatenate([g_kn[l], jnp.zeros((LANES - QK_NOPE,), F32)])
    return {
        "g_mix": g_mix[l][None, :], "w_in": w_in_p, "g_q_lora": g_q_lora[l][None, :],
        "w_q": w_q.astype(BF16), "gq": gq[None, :], "g_kv_lora": g_kv_lora[l][None, :], "gkr": gkr[None, :],
        "w_kvk": w_kvk.astype(BF16), "gkn": gkn[None, :], "w_kvv": w_kvv.astype(BF16),
        "w_kvv_t": w_kvv_t.astype(BF16),
        "ln_g": gm_ln_g[l].reshape(1, GMLP_WIDTH), "ln_b": gm_ln_b[l].reshape(1, GMLP_WIDTH),
        "g_out_att": g_out_att[l].reshape(1, N_HEADS * V_HEAD), "g_out_gm": g_out_gm[l].reshape(1, GMLP_WIDTH),
        "w_o": w_o[l].astype(BF16), "g_ffn": g_ffn[l][None, :],
        "w_gate": w_gate[l].astype(BF16), "w_up": w_up[l].astype(BF16), "w_down": w_down[l].astype(BF16),
    }


def _chunk_bias(qpos, kpos):
    vis = (kpos[None, :] // CHUNK) <= (qpos[:, None] // CHUNK)
    return jnp.asarray(np.where(vis, 0.0, NEG_INF), F32)


def kernel(x_prompt, x_sample, cache_ckv, cache_krope, g_mix, w_in, g_q_lora, w_q_up, g_qn, g_qr, g_kv_lora, g_kr,
           w_kv_up, g_kn, gm_ln_g, gm_ln_b, gm_w_s, gm_b_s, g_out_att, g_out_gm, w_o, g_ffn, w_gate, w_up, w_down):
    batch, seq, d = x_prompt.shape
    dbatch, dseq, _ = x_sample.shape
    depth, _, past, _ = cache_ckv.shape
    assert seq % GMLP_CHUNK == 0 and GMLP_CHUNK % dseq == 0 and (dbatch * dseq) % GMLP_CHUNK == 0

    tq_attn = 256
    tm_p = 512
    tm_s = dbatch * dseq

    tq_p, tk_p = _rope_tables(jnp.arange(seq))
    tq_s, tk_s = _rope_tables(past + jnp.arange(dseq))
    tq_s, tk_s = jnp.tile(tq_s, (dbatch, 1)), jnp.tile(tk_s, (dbatch, 1))
    bias_p = _chunk_bias(np.arange(tq_attn), np.arange(tq_attn))
    bias_s = _chunk_bias(past + np.arange(dseq), past + np.arange(dseq))
    reps = GMLP_CHUNK // dseq

    yp = x_prompt.reshape(batch * seq, d)
    ys = x_sample.reshape(dbatch * dseq, d)
    ckv_p, kr_p, ckv_s, kr_s, gv_s = [], [], [], [], []
    for l in range(depth):
        lw = _layer_weights(l, g_mix, w_in, g_q_lora, w_q_up, g_qn, g_qr, g_kv_lora, g_kr, w_kv_up, g_kn,
                            gm_ln_g, gm_ln_b, g_out_att, g_out_gm, w_o, g_ffn, w_gate, w_up, w_down)
        ws_p = gm_w_s[l]
        bs_p = jnp.repeat(gm_b_s[l].T, GMLP_GROUP_DIM, axis=1)
        ws_s = jnp.tile(gm_w_s[l][:, :dseq, :dseq], (1, reps, reps))
        bs_s = jnp.tile(jnp.repeat(gm_b_s[l][:, :dseq].T, GMLP_GROUP_DIM, axis=1), (reps, 1))
        krp = jnp.concatenate([jnp.zeros((dbatch, past, QK_NOPE), BF16), cache_krope[l].astype(BF16),
                               cache_krope[l].astype(BF16)], axis=-1)

        q, k, vt, c, r, ogm = _inproj(yp, lw, tq_p, tk_p, ws_p, bs_p, tm=tm_p, table_blocks=seq // tm_p,
                                      ws_mask="causal", want_vn=False, v_transposed=True)
        g_t = jnp.broadcast_to(lw["g_out_att"].reshape(N_HEADS * V_HEAD, 1), (N_HEADS * V_HEAD, tq_attn))
        oatt = _attn_prompt(q, k, vt, bias_p.T, g_t, batch=batch, seq=seq, tq=tq_attn)
        yp = _ffn(yp, oatt, ogm, lw, tm=tm_p, fchunk=256)
        ckv_p.append(c.reshape(batch, seq, KV_LORA))
        kr_p.append(r.reshape(batch, seq, QK_ROPE))

        q, k, v, c, r, ogm, vn = _inproj(ys, lw, tq_s, tk_s, ws_s, bs_s, tm=tm_s, table_blocks=1,
                                         ws_mask="blockdiag", want_vn=True, v_transposed=False)
        oatt = _attn_sample(q, k, v, cache_ckv, krp, lw["w_kvk"], lw["gkn"], lw["w_kvv"], bias_s,
                            lw["g_out_att"], layer=l, batch=dbatch, seq=dseq)
        ys = _ffn(ys, oatt, ogm, lw, tm=tm_s, fchunk=256)
        ckv_s.append(c.reshape(dbatch, dseq, KV_LORA))
        kr_s.append(r.reshape(dbatch, dseq, QK_ROPE))
        gv_s.append(vn.reshape(dbatch, dseq, GMLP_WIDTH))

    return (yp.reshape(batch, seq, d), ys.reshape(dbatch, dseq, d), jnp.stack(ckv_p), jnp.stack(kr_p),
            jnp.stack(ckv_s), jnp.stack(kr_s), jnp.stack(gv_s))
```

```python
import functools

import numpy as np
import jax
import jax.numpy as jnp
from jax import lax
from jax.experimental import pallas as pl
from jax.experimental.pallas import tpu as pltpu

F32 = jnp.float32
BF16 = jnp.bfloat16

CHUNK = 64
N_HEADS = 8
QK_NOPE = 64
QK_ROPE = 32
QK_HEAD = QK_NOPE + QK_ROPE
V_HEAD = 64
Q_LORA = 256
KV_LORA = 128
ROPE_THETA = 10000.0
GMLP_GROUPS = 8
GMLP_GROUP_DIM = 64
GMLP_WIDTH = GMLP_GROUPS * GMLP_GROUP_DIM
GMLP_CHUNK = 128
EPS = 1e-6
NEG_INF = -1e30
LOG2_E = 1.4426950408889634

LANES = 128
V7X_VMEM_BYTES = 64 * 1024 * 1024
VMEM_LIMIT_BYTES = (V7X_VMEM_BYTES * 3) // 4

HALF = QK_ROPE // 2
IN_COLS_PADDED = Q_LORA + KV_LORA + LANES + 2 * GMLP_WIDTH
COL_CKV = Q_LORA
COL_KR = Q_LORA + KV_LORA
COL_U = COL_KR + LANES
COL_V = COL_U + GMLP_WIDTH

_NT = (((1,), (1,)), ((), ()))


def _dot(a, b):
    return jnp.dot(a, b, preferred_element_type=F32)


def _dot_nt(a, b):
    return lax.dot_general(a, b, _NT, preferred_element_type=F32)


def _rms_scale(x, n):
    return lax.rsqrt(jnp.sum(x * x, axis=-1, keepdims=True) * (1.0 / n) + EPS)


def _half_sums(x, lo):
    s_lo = jnp.sum(jnp.where(lo, x, 0.0), axis=-1, keepdims=True)
    s_hi = jnp.sum(jnp.where(lo, 0.0, x), axis=-1, keepdims=True)
    return jnp.where(lo, s_lo, s_hi)


def _half_rms_scale(x, lo):
    return lax.rsqrt(_half_sums(x * x, lo) * (1.0 / 64) + EPS)


def _gelu_tanh(t):
    return t * (0.5 * (1.0 + jnp.tanh(0.7978845608028654 * (t + 0.044715 * (t * t * t)))))


def _inproj_kernel(x_ref, gmix_ref, win_ref, gql_ref, wq_ref, gq_ref, tq_ref, gkv_ref, gkr_ref, tk_ref,
                   wkk_ref, gkn_ref, wkv_ref, lng_ref, lnb_ref, ws_ref, bs_ref, ggm_ref,
                   q_out, k_out, v_out, ckv_out, kr_out, ogm_out, *vn_out, ws_mask, v_transposed):
    tm = x_ref.shape[0]
    x = x_ref[...]
    h = (x * _rms_scale(x, x.shape[1]) * gmix_ref[...]).astype(BF16)
    z = _dot(h, win_ref[...])

    lane = lax.broadcasted_iota(jnp.int32, (tm, LANES), 1)
    lo = lane < 64

    ql = z[:, 0:Q_LORA]
    qn = (ql * _rms_scale(ql, Q_LORA) * gql_ref[...]).astype(BF16)
    qz = _dot(qn, wq_ref[...])
    tq = tq_ref[...] * gq_ref[...]
    in_rope = jnp.logical_and(lane >= 64, lane < 96)
    for hd in range(N_HEADS):
        xh = qz[:, hd * LANES:(hd + 1) * LANES]
        x2 = xh * xh
        s_n = jnp.sum(jnp.where(lo, x2, 0.0), axis=-1, keepdims=True) * (1.0 / QK_NOPE)
        s_r = jnp.sum(jnp.where(in_rope, x2, 0.0), axis=-1, keepdims=True) * (1.0 / QK_ROPE)
        rr = jnp.where(lo, lax.rsqrt(s_n + EPS), lax.rsqrt(s_r + EPS))
        q_out[:, hd * LANES:(hd + 1) * LANES] = (xh * rr * tq).astype(BF16)

    y0 = z[:, COL_KR:COL_KR + LANES]
    s_k = jnp.sum(jnp.where(lane < QK_ROPE, y0 * y0, 0.0), axis=-1, keepdims=True) * (1.0 / QK_ROPE)
    y = y0 * lax.rsqrt(s_k + EPS) * (gkr_ref[...] * tk_ref[...])
    r64 = pltpu.roll(y, 64, 1)
    r32 = pltpu.roll(y, 32, 1)
    r96 = pltpu.roll(y, 96, 1)
    kr_out[...] = (y + r96)[:, 0:QK_ROPE]
    kr_hi = jnp.where(lo, 0.0, r64 + r32 + r96)

    c0 = z[:, COL_CKV:COL_CKV + KV_LORA]
    cn = c0 * _rms_scale(c0, KV_LORA) * gkv_ref[...]
    ckv_out[...] = cn
    cb = cn.astype(BF16)
    kz = _dot(cb, wkk_ref[...])
    for hd in range(N_HEADS):
        xh = kz[:, hd * LANES:(hd + 1) * LANES]
        kn = xh * _rms_scale(xh, QK_NOPE) * gkn_ref[...]
        k_out[:, hd * LANES:(hd + 1) * LANES] = jnp.where(lo, kn, kr_hi).astype(BF16)
    if v_transposed:
        vt = _dot_nt(wkv_ref[...], cb)
        row = lax.broadcasted_iota(jnp.int32, vt.shape, 0)
        v_out[...] = jnp.where((row & (LANES - 1)) == V_HEAD, 1.0, vt).astype(BF16)
    else:
        v_out[...] = _dot(cb, wkv_ref[...]).astype(BF16)

    row = lax.broadcasted_iota(jnp.int32, (GMLP_CHUNK, GMLP_CHUNK), 0) // CHUNK
    col = lax.broadcasted_iota(jnp.int32, (GMLP_CHUNK, GMLP_CHUNK), 1) // CHUNK
    keep = (row >= col) if ws_mask == "causal" else (row == col)
    lo_c = lax.broadcasted_iota(jnp.int32, (GMLP_CHUNK, LANES), 1) < 64
    for p in range(GMLP_GROUPS // 2):
        sl = slice(p * LANES, (p + 1) * LANES)
        w_pair = jnp.concatenate(
            [jnp.where(keep, ws_ref[2 * p], 0.0), jnp.where(keep, ws_ref[2 * p + 1], 0.0)], axis=0).astype(BF16)
        ug = _gelu_tanh(z[:, COL_U + p * LANES:COL_U + (p + 1) * LANES])
        vg = _gelu_tanh(z[:, COL_V + p * LANES:COL_V + (p + 1) * LANES])
        xc = vg - _half_sums(vg, lo) * (1.0 / GMLP_GROUP_DIM)
        vn = xc * _half_rms_scale(xc, lo) * lng_ref[:, sl] + lnb_ref[:, sl]
        if vn_out:
            vn_out[0][:, sl] = vn
        vnb = vn.astype(BF16)
        for c in range(tm // GMLP_CHUNK):
            rows = slice(c * GMLP_CHUNK, (c + 1) * GMLP_CHUNK)
            res = _dot(w_pair, vnb[rows])
            mixed = jnp.where(lo_c, res[0:GMLP_CHUNK], res[GMLP_CHUNK:]) + bs_ref[:, sl]
            og = ug[rows] * mixed
            ogm_out[rows, sl] = (og * _half_rms_scale(og, lo_c) * ggm_ref[:, sl]).astype(BF16)


def _const_spec(shape):
    return pl.BlockSpec(shape, lambda *_: (0,) * len(shape), pipeline_mode=pl.Buffered(1))


def _inproj(x, lw, tq, tk, ws, bs, *, tm, table_blocks, ws_mask, want_vn, v_transposed):
    n, d = x.shape
    w_v = lw["w_kvv_t"] if v_transposed else lw["w_kvv"]
    grid = (n // tm,)
    tbl = lambda i: (i % table_blocks, 0)
    row = lambda i: (i, 0)
    in_specs = [
        pl.BlockSpec((tm, d), row),
        _const_spec((1, d)),
        _const_spec((d, IN_COLS_PADDED)),
        _const_spec((1, Q_LORA)),
        _const_spec((Q_LORA, N_HEADS * LANES)),
        _const_spec((1, LANES)),
        pl.BlockSpec((tm, LANES), tbl),
        _const_spec((1, KV_LORA)),
        _const_spec((1, LANES)),
        pl.BlockSpec((tm, LANES), tbl),
        _const_spec((KV_LORA, N_HEADS * LANES)),
        _const_spec((1, LANES)),
        _const_spec(w_v.shape),
        _const_spec((1, GMLP_WIDTH)),
        _const_spec((1, GMLP_WIDTH)),
        _const_spec((GMLP_GROUPS, GMLP_CHUNK, GMLP_CHUNK)),
        _const_spec((GMLP_CHUNK, GMLP_WIDTH)),
        _const_spec((1, GMLP_WIDTH)),
    ]
    out_shape = [
        jax.ShapeDtypeStruct((n, N_HEADS * LANES), BF16),
        jax.ShapeDtypeStruct((n, N_HEADS * LANES), BF16),
        (jax.ShapeDtypeStruct((N_HEADS * LANES, n), BF16) if v_transposed
         else jax.ShapeDtypeStruct((n, N_HEADS * V_HEAD), BF16)),
        jax.ShapeDtypeStruct((n, KV_LORA), F32),
        jax.ShapeDtypeStruct((n, QK_ROPE), F32),
        jax.ShapeDtypeStruct((n, GMLP_WIDTH), BF16),
    ]
    out_specs = [
        pl.BlockSpec((tm, N_HEADS * LANES), row),
        pl.BlockSpec((tm, N_HEADS * LANES), row),
        (pl.BlockSpec((N_HEADS * LANES, tm), lambda i: (0, i)) if v_transposed
         else pl.BlockSpec((tm, N_HEADS * V_HEAD), row)),
        pl.BlockSpec((tm, KV_LORA), row),
        pl.BlockSpec((tm, QK_ROPE), row),
        pl.BlockSpec((tm, GMLP_WIDTH), row),
    ]
    if want_vn:
        out_shape.append(jax.ShapeDtypeStruct((n, GMLP_WIDTH), F32))
        out_specs.append(pl.BlockSpec((tm, GMLP_WIDTH), row))
    return pl.pallas_call(
        functools.partial(_inproj_kernel, ws_mask=ws_mask, v_transposed=v_transposed),
        grid=grid, in_specs=in_specs, out_specs=out_specs, out_shape=out_shape,
        compiler_params=pltpu.CompilerParams(dimension_semantics=("arbitrary",),
                                             vmem_limit_bytes=VMEM_LIMIT_BYTES),
        name="inproj_vn" if want_vn else "inproj",
    )(x, lw["g_mix"], lw["w_in"], lw["g_q_lora"], lw["w_q"], lw["gq"], tq, lw["g_kv_lora"], lw["gkr"], tk,
      lw["w_kvk"], lw["gkn"], w_v, lw["ln_g"], lw["ln_b"], ws, bs, lw["g_out_gm"])


ATTN_STREAM_LAG = 18


def _attn_prompt_kernel(q_ref, k_ref, vt_ref, bias_ref, g_ref, o_ref, s_scr, *, tq):
    s_len = q_ref.shape[0]
    bodies = [(qi, hh) for qi in reversed(range(s_len // tq)) for hh in range(2)]
    bases = np.cumsum([0] + [qi + 1 for qi, _ in bodies])
    state = [{} for _ in bodies]

    def a_steps(bi):
        qi, hh = bodies[bi]
        cols = slice(hh * LANES, (hh + 1) * LANES)
        rows = slice(qi * tq, (qi + 1) * tq)
        st = state[bi]

        def chunk(j):
            s = _dot_nt(k_ref[j * tq:(j + 1) * tq, cols], q_ref[rows, cols])
            if j == qi:
                s = s + bias_ref[...]
            s_scr[bases[bi] + j] = s
            cm = jnp.max(s.reshape(tq // 8, 8, tq), axis=0)
            st["m_run"] = cm if j == 0 else jnp.maximum(st["m_run"], cm)

        def finish():
            st["m"] = jnp.max(st["m_run"], axis=0, keepdims=True)

        return [functools.partial(chunk, j) for j in range(qi + 1)] + [finish]

    def b_steps(bi):
        qi, hh = bodies[bi]
        st = state[bi]

        def chunk(j):
            p = jnp.exp2(s_scr[bases[bi] + j] - st["m"]).astype(BF16)
            d = _dot(vt_ref[hh * LANES:(hh + 1) * LANES, j * tq:(j + 1) * tq], p)
            st["acc"] = d if j == 0 else st["acc"] + d

        def finish():
            acc = st["acc"]
            o = acc[0:V_HEAD] * (1.0 / acc[V_HEAD:V_HEAD + 1])
            r = lax.rsqrt(jnp.sum(o * o, axis=0, keepdims=True) * (1.0 / V_HEAD) + EPS)
            st["out"] = o * r * g_ref[hh * V_HEAD:(hh + 1) * V_HEAD, :]
            if hh == 1:
                pair_t = jnp.concatenate([state[bi - 1]["out"], st["out"]], axis=0)
                o_ref[qi * tq:(qi + 1) * tq, :] = pair_t.T.astype(BF16)

        return [functools.partial(chunk, j) for j in range(qi + 1)] + [finish]

    a_all = [f for bi in range(len(bodies)) for f in a_steps(bi)]
    b_all = [f for bi in range(len(bodies)) for f in b_steps(bi)]
    for i in range(len(a_all) + ATTN_STREAM_LAG):
        if i < len(a_all):
            a_all[i]()
        if i >= ATTN_STREAM_LAG:
            b_all[i - ATTN_STREAM_LAG]()


def _attn_prompt(q, k, vt, bias_t, g_t, *, batch, seq, tq):
    n = q.shape[0]
    pairs = N_HEADS // 2
    nq = seq // tq
    assert ATTN_STREAM_LAG > nq + 1
    return pl.pallas_call(
        functools.partial(_attn_prompt_kernel, tq=tq),
        grid=(batch, pairs),
        in_specs=[
            pl.BlockSpec((seq, 2 * LANES), lambda b, p: (b, p)),
            pl.BlockSpec((seq, 2 * LANES), lambda b, p: (b, p)),
            pl.BlockSpec((2 * LANES, seq), lambda b, p: (p, b)),
            pl.BlockSpec((tq, tq), lambda b, p: (0, 0)),
            pl.BlockSpec((2 * V_HEAD, tq), lambda b, p: (p, 0)),
        ],
        out_specs=pl.BlockSpec((seq, 2 * V_HEAD), lambda b, p: (b, p)),
        out_shape=jax.ShapeDtypeStruct((n, N_HEADS * V_HEAD), BF16),
        scratch_shapes=[pltpu.VMEM((nq * (nq + 1), tq, tq), F32)],
        compiler_params=pltpu.CompilerParams(dimension_semantics=("arbitrary", "arbitrary"),
                                             vmem_limit_bytes=VMEM_LIMIT_BYTES),
        name="attn_prompt",
    )(q, k, vt, bias_t, g_t)


SAMPLE_KEY_BLOCK = 1024


def _attn_sample_kernel(q_ref, cnew_ref, krnew_ref, ckv_ref, kr_ref, w2_ref, wkt_ref, wv_ref, bias_ref, gout_ref,
                        o_ref, s_scr, kr_scr, *, kblk):
    past = ckv_ref.shape[0]
    sq = q_ref.shape[0]
    q2 = jnp.concatenate([_dot(q_ref[:, h * LANES:(h + 1) * LANES], w2_ref[h]) for h in range(N_HEADS)],
                         axis=0).astype(BF16)
    qt = q2[:, 0:KV_LORA]
    qr = q2[:, KV_LORA:]
    kr_scr[...] = jnp.zeros(kr_scr.shape, F32)
    kr_scr[0:past, 0:QK_ROPE] = kr_ref[...]
    kr_scr[past:past + sq, 0:QK_ROPE] = krnew_ref[...]
    ones_blk = jnp.ones((kblk, LANES), BF16)

    def scores(cb, krb):
        kzt = _dot_nt(wkt_ref[...], cb)
        s1 = _dot_nt(qt, cb)
        s2 = _dot_nt(qr, krb)
        out = []
        for h in range(N_HEADS):
            kh = kzt[h * QK_NOPE:(h + 1) * QK_NOPE]
            r = lax.rsqrt(jnp.sum(kh * kh, axis=0, keepdims=True) * (1.0 / QK_NOPE) + EPS)
            out.append(s1[h * sq:(h + 1) * sq] * r + s2[h * sq:(h + 1) * sq])
        return jnp.concatenate(out, axis=0)

    m_run = None
    for j in range(past // kblk):
        ks = slice(j * kblk, (j + 1) * kblk)
        s = scores(ckv_ref[ks, :].astype(BF16), kr_scr[ks, :].astype(BF16))
        s_scr[:, ks] = s
        for c in range(0, kblk, LANES):
            blk = s[:, c:c + LANES]
            m_run = blk if m_run is None else jnp.maximum(m_run, blk)
    cn = jnp.concatenate([cnew_ref[...], jnp.zeros((LANES - sq, KV_LORA), F32)], axis=0).astype(BF16)
    s_new = scores(cn, kr_scr[past:past + LANES, :].astype(BF16)) + bias_ref[...]
    m = jnp.max(jnp.maximum(m_run, s_new), axis=-1, keepdims=True)
    acc = _dot(jnp.exp2(s_new - m).astype(BF16), jnp.concatenate([cn, ones_blk[0:LANES]], axis=1))
    for j in range(past // kblk):
        ks = slice(j * kblk, (j + 1) * kblk)
        p = jnp.exp2(s_scr[:, ks] - m).astype(BF16)
        acc = acc + _dot(p, jnp.concatenate([ckv_ref[ks, :].astype(BF16), ones_blk], axis=1))
    o_lat = (acc[:, 0:KV_LORA] * (1.0 / acc[:, KV_LORA:])).astype(BF16)
    o_full = _dot(o_lat, wv_ref[...])
    head = lax.broadcasted_iota(jnp.int32, (sq, N_HEADS * V_HEAD), 1) >> 6
    o = jnp.zeros((sq, N_HEADS * V_HEAD), F32)
    for h in range(N_HEADS):
        o = jnp.where(head == h, o_full[h * sq:(h + 1) * sq], o)
    lo = lax.broadcasted_iota(jnp.int32, (sq, LANES), 1) < 64
    for c in range(0, N_HEADS * V_HEAD, LANES):
        oc = o[:, c:c + LANES]
        o_ref[:, c:c + LANES] = (oc * _half_rms_scale(oc, lo) * gout_ref[:, c:c + LANES]).astype(BF16)


def _attn_sample(q, cnew, krnew, cache_ckv, cache_krope, w2, wkt, w_kvv, bias, gout, *, layer, batch, seq):
    n = q.shape[0]
    past = cache_ckv.shape[2]
    assert V_HEAD == 64 and past % SAMPLE_KEY_BLOCK == 0 and seq <= LANES
    return pl.pallas_call(
        functools.partial(_attn_sample_kernel, kblk=SAMPLE_KEY_BLOCK),
        grid=(batch,),
        in_specs=[
            pl.BlockSpec((seq, N_HEADS * LANES), lambda b: (b, 0)),
            pl.BlockSpec((seq, KV_LORA), lambda b: (b, 0)),
            pl.BlockSpec((seq, QK_ROPE), lambda b: (b, 0)),
            pl.BlockSpec((None, None, past, KV_LORA), lambda b: (layer, b, 0, 0)),
            pl.BlockSpec((None, None, past, QK_ROPE), lambda b: (layer, b, 0, 0)),
            pl.BlockSpec((N_HEADS, LANES, 2 * LANES), lambda b: (0, 0, 0)),
            pl.BlockSpec((N_HEADS * QK_NOPE, KV_LORA), lambda b: (0, 0)),
            pl.BlockSpec((KV_LORA, N_HEADS * V_HEAD), lambda b: (0, 0)),
            pl.BlockSpec((N_HEADS * seq, LANES), lambda b: (0, 0)),
            pl.BlockSpec((1, N_HEADS * V_HEAD), lambda b: (0, 0)),
        ],
        out_specs=pl.BlockSpec((seq, N_HEADS * V_HEAD), lambda b: (b, 0)),
        out_shape=jax.ShapeDtypeStruct((n, N_HEADS * V_HEAD), BF16),
        scratch_shapes=[pltpu.VMEM((N_HEADS * seq, past), F32), pltpu.VMEM((past + LANES, LANES), F32)],
        compiler_params=pltpu.CompilerParams(dimension_semantics=("arbitrary",),
                                             vmem_limit_bytes=VMEM_LIMIT_BYTES),
        name="attn_sample",
    )(q, cnew, krnew, cache_ckv, cache_krope, w2, wkt, w_kvv, bias, gout)


def _ffn_kernel(x_ref, oa_ref, og_ref, wo_ref, gffn_ref, wg_ref, wu_ref, wd_ref, y_ref, *, fchunk):
    aw = oa_ref.shape[1]
    x1 = x_ref[...] + _dot(oa_ref[...], wo_ref[0:aw, :]) + _dot(og_ref[...], wo_ref[aw:, :])
    h2 = (x1 * _rms_scale(x1, x1.shape[1]) * gffn_ref[...]).astype(BF16)
    acc = None
    for c in range(0, wg_ref.shape[1], fchunk):
        g = _dot(h2, wg_ref[:, c:c + fchunk])
        u = _dot(h2, wu_ref[:, c:c + fchunk])
        a = (g * jax.nn.sigmoid(g) * u).astype(BF16)
        d = _dot(a, wd_ref[c:c + fchunk, :])
        acc = d if acc is None else acc + d
    y_ref[...] = x1 + acc


def _ffn(x, oa, og, lw, *, tm, fchunk):
    n, d = x.shape
    dff = lw["w_gate"].shape[1]
    row = lambda i: (i, 0)
    return pl.pallas_call(
        functools.partial(_ffn_kernel, fchunk=fchunk),
        grid=(n // tm,),
        in_specs=[
            pl.BlockSpec((tm, d), row),
            pl.BlockSpec((tm, oa.shape[1]), row),
            pl.BlockSpec((tm, og.shape[1]), row),
            _const_spec((oa.shape[1] + og.shape[1], d)),
            _const_spec((1, d)),
            _const_spec((d, dff)),
            _const_spec((d, dff)),
            _const_spec((dff, d)),
        ],
        out_specs=pl.BlockSpec((tm, d), row),
        out_shape=jax.ShapeDtypeStruct((n, d), F32),
        compiler_params=pltpu.CompilerParams(dimension_semantics=("arbitrary",),
                                             vmem_limit_bytes=VMEM_LIMIT_BYTES),
        name="outproj_ffn",
    )(x, oa, og, lw["w_o"], lw["g_ffn"], lw["w_gate"], lw["w_up"], lw["w_down"])


def _rope_tables(pos):
    inv = ROPE_THETA ** (-np.arange(HALF, dtype=np.float64) / HALF)
    ang = np.asarray(pos, np.float64)[:, None] * inv[None, :]
    cos, sin = np.cos(ang), np.sin(ang)
    n = ang.shape[0]
    tq = np.concatenate([np.ones((n, QK_NOPE)), cos, cos, -sin, sin], axis=1)
    tk = np.concatenate([cos, cos, -sin, sin, np.zeros((n, LANES - 2 * QK_ROPE))], axis=1)
    return tq.astype(np.float32), tk.astype(np.float32)


def _rope_fold_matrix():
    f = np.zeros((LANES, LANES), np.float32)
    for i in range(QK_ROPE):
        f[QK_NOPE + i, i] = 1.0
        f[QK_NOPE + QK_ROPE + i, i] = 1.0
    return f


_ROPE_FOLD = _rope_fold_matrix()


def _swap_halves(g):
    return jnp.concatenate([g[..., HALF:], g[..., :HALF]], axis=-1)


def _layer_weights(l, g_mix, w_in, g_q_lora, w_q_up, g_qn, g_qr, g_kv_lora, g_kr, w_kv_up, g_kn,
                   gm_ln_g, gm_ln_b, g_out_att, g_out_gm, w_o, g_ffn, w_gate, w_up, w_down):
    d = w_in.shape[1]
    w = w_in[l]
    kr0 = Q_LORA + KV_LORA
    w_in_p = jnp.concatenate(
        [w[:, :kr0 + QK_ROPE], w[:, kr0 + HALF:kr0 + QK_ROPE], w[:, kr0:kr0 + HALF],
         jnp.zeros((d, LANES - 2 * QK_ROPE), F32), w[:, kr0 + QK_ROPE:]], axis=1).astype(BF16)
    wq = w_q_up[l].reshape(Q_LORA, N_HEADS, QK_HEAD)
    r1, r2 = wq[..., QK_NOPE:QK_NOPE + HALF], wq[..., QK_NOPE + HALF:]
    w_q = jnp.concatenate([wq[..., :QK_NOPE], r1, r2, r2, r1], axis=-1).reshape(Q_LORA, N_HEADS * LANES)
    gq = jnp.concatenate([g_qn[l], g_qr[l], _swap_halves(g_qr[l])]) * (QK_HEAD ** -0.5 * LOG2_E)
    gkr = jnp.concatenate([g_kr[l], _swap_halves(g_kr[l]), jnp.zeros((LANES - 2 * QK_ROPE,), F32)])
    wkv = w_kv_up[l].reshape(KV_LORA, N_HEADS, QK_NOPE + V_HEAD)
    w_kvk = jnp.concatenate([wkv[..., :QK_NOPE], jnp.zeros((KV_LORA, N_HEADS, LANES - QK_NOPE), F32)],
                            axis=-1).reshape(KV_LORA, N_HEADS * LANES)
    w_kvv = wkv[..., QK_NOPE:].reshape(KV_LORA, N_HEADS * V_HEAD)
    w_kvv_t = jnp.concatenate([wkv[..., QK_NOPE:], jnp.zeros((KV_LORA, N_HEADS, LANES - V_HEAD), F32)],
                              axis=-1).reshape(KV_LORA, N_HEADS * LANES).T
    gkn = jnp.concatenate([g_kn[l], jnp.zeros((LANES - QK_NOPE,), F32)])
    wk_t = jnp.transpose(wkv[..., :QK_NOPE], (1, 2, 0))
    w_abs = jnp.concatenate([wk_t * g_kn[l][None, :, None], jnp.zeros((N_HEADS, LANES - QK_NOPE, KV_LORA), F32)],
                            axis=1)
    w2 = jnp.concatenate([w_abs, jnp.broadcast_to(jnp.asarray(_ROPE_FOLD), (N_HEADS, LANES, LANES))], axis=2)
    return {
        "g_mix": g_mix[l][None, :], "w_in": w_in_p, "g_q_lora": g_q_lora[l][None, :],
        "w_q": w_q.astype(BF16), "gq": gq[None, :], "g_kv_lora": g_kv_lora[l][None, :], "gkr": gkr[None, :],
        "w_kvk": w_kvk.astype(BF16), "gkn": gkn[None, :], "w_kvv": w_kvv.astype(BF16),
        "w_kvv_t": w_kvv_t.astype(BF16),
        "w2": w2.astype(BF16), "wk_t": wk_t.reshape(N_HEADS * QK_NOPE, KV_LORA).astype(BF16),
        "ln_g": gm_ln_g[l].reshape(1, GMLP_WIDTH), "ln_b": gm_ln_b[l].reshape(1, GMLP_WIDTH),
        "g_out_att": g_out_att[l].reshape(1, N_HEADS * V_HEAD), "g_out_gm": g_out_gm[l].reshape(1, GMLP_WIDTH),
        "w_o": w_o[l].astype(BF16), "g_ffn": g_ffn[l][None, :],
        "w_gate": w_gate[l].astype(BF16), "w_up": w_up[l].astype(BF16), "w_down": w_down[l].astype(BF16),
    }


def _chunk_bias(qpos, kpos):
    vis = (kpos[None, :] // CHUNK) <= (qpos[:, None] // CHUNK)
    return np.where(vis, 0.0, NEG_INF).astype(np.float32)


def kernel(x_prompt, x_sample, cache_ckv, cache_krope, g_mix, w_in, g_q_lora, w_q_up, g_qn, g_qr, g_kv_lora, g_kr,
           w_kv_up, g_kn, gm_ln_g, gm_ln_b, gm_w_s, gm_b_s, g_out_att, g_out_gm, w_o, g_ffn, w_gate, w_up, w_down):
    batch, seq, d = x_prompt.shape
    dbatch, dseq, _ = x_sample.shape
    depth, _, past, _ = cache_ckv.shape
    assert seq % GMLP_CHUNK == 0 and GMLP_CHUNK % dseq == 0 and (dbatch * dseq) % GMLP_CHUNK == 0

    tq_attn = 256
    tm_p = 512
    tm_s = dbatch * dseq

    tq_p, tk_p = (jnp.asarray(t) for t in _rope_tables(np.arange(seq)))
    tq_s, tk_s = (jnp.asarray(np.tile(t, (dbatch, 1))) for t in _rope_tables(past + np.arange(dseq)))
    bias_p = jnp.asarray(_chunk_bias(np.arange(tq_attn), np.arange(tq_attn)))
    bias_s = np.full((dseq, LANES), NEG_INF, np.float32)
    bias_s[:, :dseq] = _chunk_bias(past + np.arange(dseq), past + np.arange(dseq))
    bias_s = jnp.asarray(np.tile(bias_s, (N_HEADS, 1)))
    reps = GMLP_CHUNK // dseq

    yp = x_prompt.reshape(batch * seq, d)
    ys = x_sample.reshape(dbatch * dseq, d)
    ckv_p, kr_p, ckv_s, kr_s, gv_s = [], [], [], [], []
    for l in range(depth):
        lw = _layer_weights(l, g_mix, w_in, g_q_lora, w_q_up, g_qn, g_qr, g_kv_lora, g_kr, w_kv_up, g_kn,
                            gm_ln_g, gm_ln_b, g_out_att, g_out_gm, w_o, g_ffn, w_gate, w_up, w_down)
        ws_p = gm_w_s[l]
        bs_p = jnp.repeat(gm_b_s[l].T, GMLP_GROUP_DIM, axis=1)
        ws_s = jnp.tile(gm_w_s[l][:, :dseq, :dseq], (1, reps, reps))
        bs_s = jnp.tile(jnp.repeat(gm_b_s[l][:, :dseq].T, GMLP_GROUP_DIM, axis=1), (reps, 1))

        q, k, vt, c, r, ogm = _inproj(yp, lw, tq_p, tk_p, ws_p, bs_p, tm=tm_p, table_blocks=seq // tm_p,
                                      ws_mask="causal", want_vn=False, v_transposed=True)
        g_t = jnp.broadcast_to(lw["g_out_att"].reshape(N_HEADS * V_HEAD, 1), (N_HEADS * V_HEAD, tq_attn))
        oatt = _attn_prompt(q, k, vt, bias_p.T, g_t, batch=batch, seq=seq, tq=tq_attn)
        yp = _ffn(yp, oatt, ogm, lw, tm=tm_p, fchunk=256)
        ckv_p.append(c.reshape(batch, seq, KV_LORA))
        kr_p.append(r.reshape(batch, seq, QK_ROPE))

        q, k, v, c, r, ogm, vn = _inproj(ys, lw, tq_s, tk_s, ws_s, bs_s, tm=tm_s, table_blocks=1,
                                         ws_mask="blockdiag", want_vn=True, v_transposed=False)
        oatt = _attn_sample(q, c, r, cache_ckv, cache_krope, lw["w2"], lw["wk_t"], lw["w_kvv"], bias_s,
                            lw["g_out_att"], layer=l, batch=dbatch, seq=dseq)
        ys = _ffn(ys, oatt, ogm, lw, tm=tm_s, fchunk=256)
        ckv_s.append(c.reshape(dbatch, dseq, KV_LORA))
        kr_s.append(r.reshape(dbatch, dseq, QK_ROPE))
        gv_s.append(vn.reshape(dbatch, dseq, GMLP_WIDTH))

    return (yp.reshape(batch, seq, d), ys.reshape(dbatch, dseq, d), jnp.stack(ckv_p), jnp.stack(kr_p),
            jnp.stack(ckv_s), jnp.stack(kr_s), jnp.stack(gv_s))
```

```python
import functools

import numpy as np
import jax
import jax.numpy as jnp
from jax import lax
from jax.experimental import pallas as pl
from jax.experimental.pallas import tpu as pltpu

F32 = jnp.float32
BF16 = jnp.bfloat16

CHUNK = 64
N_HEADS = 8
QK_NOPE = 64
QK_ROPE = 32
QK_HEAD = QK_NOPE + QK_ROPE
V_HEAD = 64
Q_LORA = 256
KV_LORA = 128
ROPE_THETA = 10000.0
GMLP_GROUPS = 8
GMLP_GROUP_DIM = 64
GMLP_WIDTH = GMLP_GROUPS * GMLP_GROUP_DIM
GMLP_CHUNK = 128
EPS = 1e-6
NEG_INF = -1e30
LOG2_E = 1.4426950408889634

LANES = 128
V7X_VMEM_BYTES = 64 * 1024 * 1024
VMEM_LIMIT_BYTES = (V7X_VMEM_BYTES * 3) // 4

HALF = QK_ROPE // 2
IN_COLS_PADDED = Q_LORA + KV_LORA + LANES + 2 * GMLP_WIDTH
COL_CKV = Q_LORA
COL_KR = Q_LORA + KV_LORA
COL_U = COL_KR + LANES
COL_V = COL_U + GMLP_WIDTH

_NT = (((1,), (1,)), ((), ()))


def _dot(a, b):
    return jnp.dot(a, b, preferred_element_type=F32)


def _dot_nt(a, b):
    return lax.dot_general(a, b, _NT, preferred_element_type=F32)


def _rsqrt_sumsq(x, n):
    return lax.rsqrt(jnp.sum(x * x, axis=-1, keepdims=True) + n * EPS)


def _half_sums(x, lo):
    s_lo = jnp.sum(jnp.where(lo, x, 0.0), axis=-1, keepdims=True)
    s_hi = jnp.sum(jnp.where(lo, 0.0, x), axis=-1, keepdims=True)
    return jnp.where(lo, s_lo, s_hi)


def _half_rms_scale(x, lo):
    return lax.rsqrt(_half_sums(x * x, lo) * (1.0 / 64) + EPS)


def _half_rsqrt_sumsq(x, lo):
    return lax.rsqrt(_half_sums(x * x, lo) + 64 * EPS)


def _gelu_tanh(t):
    c = 0.7978845608028654
    h = 0.5 * t
    return h + h * jnp.tanh(t * (c + (c * 0.044715) * (t * t)))


INPROJ_SUB_ROWS = 256


def _inproj_kernel(x_ref, win_ref, wq_ref, gq_ref, tq_ref, gkv_ref, gkr_ref, tk_ref,
                   wkk_ref, wkv_ref, lng_ref, lnb_ref, ws_ref, bs_ref, ggm_ref, seg_ref, epsq_ref,
                   q_out, k_out, v_out, ckv_out, kr_out, ogm_out, *vn_out, ws_mask, v_transposed):
    tm = x_ref.shape[0]
    sub = min(tm, INPROJ_SUB_ROWS)
    lane = lax.broadcasted_iota(jnp.int32, (sub, LANES), 1)
    lo = lane < 64
    lo_c = lax.broadcasted_iota(jnp.int32, (GMLP_CHUNK, LANES), 1) < 64
    row = lax.broadcasted_iota(jnp.int32, (GMLP_CHUNK, GMLP_CHUNK), 0) // CHUNK
    col = lax.broadcasted_iota(jnp.int32, (GMLP_CHUNK, GMLP_CHUNK), 1) // CHUNK
    keep = (row >= col) if ws_mask == "causal" else (row == col)

    def project(r0):
        x = x_ref[r0:r0 + sub, :]
        h = (x * _rsqrt_sumsq(x, x.shape[1])).astype(BF16)
        return _dot(h, win_ref[...])

    def finish(z, r0):
        rows = slice(r0, r0 + sub)
        ql = z[:, 0:Q_LORA]
        qn = (ql * _rsqrt_sumsq(ql, Q_LORA)).astype(BF16)
        qz = _dot(qn, wq_ref[...])
        tq = tq_ref[rows, :] * gq_ref[...]
        for hd in range(N_HEADS):
            cols = slice(hd * LANES, (hd + 1) * LANES)
            xh = qz[:, cols]
            ss = _dot((xh * xh).astype(BF16), seg_ref[...])
            q_out[rows, cols] = (xh * lax.rsqrt(ss + epsq_ref[...]) * tq).astype(BF16)

        y0 = z[:, COL_KR:COL_KR + LANES]
        s_k = jnp.sum(jnp.where(lane < QK_ROPE, y0 * y0, 0.0), axis=-1, keepdims=True)
        y = y0 * lax.rsqrt(s_k + QK_ROPE * EPS) * (gkr_ref[...] * tk_ref[rows, :])
        r64 = pltpu.roll(y, 64, 1)
        r32 = pltpu.roll(y, 32, 1)
        r96 = pltpu.roll(y, 96, 1)
        kr_out[rows, :] = (y + r96)[:, 0:QK_ROPE]
        kr_hi = jnp.where(lo, 0.0, r64 + r32 + r96)

        c0 = z[:, COL_CKV:COL_CKV + KV_LORA]
        cn = c0 * _rsqrt_sumsq(c0, KV_LORA) * gkv_ref[...]
        ckv_out[rows, :] = cn
        cb = cn.astype(BF16)
        kz = _dot(cb, wkk_ref[...])
        for hd in range(N_HEADS):
            cols = slice(hd * LANES, (hd + 1) * LANES)
            xh = kz[:, cols]
            kn = xh * _rsqrt_sumsq(xh, QK_NOPE)
            k_out[rows, cols] = jnp.where(lo, kn, kr_hi).astype(BF16)
        if v_transposed:
            vt = _dot_nt(wkv_ref[...], cb)
            vrow = lax.broadcasted_iota(jnp.int32, vt.shape, 0)
            v_out[:, rows] = jnp.where((vrow & (LANES - 1)) == V_HEAD, 1.0, vt).astype(BF16)
        else:
            v_out[rows, :] = _dot(cb, wkv_ref[...]).astype(BF16)

        for p in range(GMLP_GROUPS // 2):
            sl = slice(p * LANES, (p + 1) * LANES)
            w_pair = jnp.concatenate(
                [jnp.where(keep, ws_ref[2 * p], 0.0), jnp.where(keep, ws_ref[2 * p + 1], 0.0)], axis=0).astype(BF16)
            ug = _gelu_tanh(z[:, COL_U + p * LANES:COL_U + (p + 1) * LANES])
            vg = _gelu_tanh(z[:, COL_V + p * LANES:COL_V + (p + 1) * LANES])
            xc = vg - _half_sums(vg, lo) * (1.0 / GMLP_GROUP_DIM)
            vn = xc * _half_rsqrt_sumsq(xc, lo) * lng_ref[:, sl] + lnb_ref[:, sl]
            if vn_out:
                vn_out[0][rows, sl] = vn
            vnb = vn.astype(BF16)
            for c in range(sub // GMLP_CHUNK):
                crows = slice(c * GMLP_CHUNK, (c + 1) * GMLP_CHUNK)
                res = _dot(w_pair, vnb[crows])
                mixed = jnp.where(lo_c, res[0:GMLP_CHUNK], res[GMLP_CHUNK:]) + bs_ref[:, sl]
                og = ug[crows] * mixed
                ogm_out[r0 + c * GMLP_CHUNK:r0 + (c + 1) * GMLP_CHUNK, sl] = (
                    og * _half_rsqrt_sumsq(og, lo_c) * ggm_ref[:, sl]).astype(BF16)

    starts = list(range(0, tm, sub))
    z_next = project(starts[0])
    for i, r0 in enumerate(starts):
        z_cur = z_next
        if i + 1 < len(starts):
            z_next = project(starts[i + 1])
        finish(z_cur, r0)


def _const_spec(shape):
    return pl.BlockSpec(shape, lambda *_: (0,) * len(shape), pipeline_mode=pl.Buffered(1))


def _inproj(x, lw, tq, tk, ws, bs, *, tm, table_blocks, ws_mask, want_vn, v_transposed):
    n, d = x.shape
    w_v = lw["w_kvv_t"] if v_transposed else lw["w_kvv"]
    grid = (n // tm,)
    tbl = lambda i: (i % table_blocks, 0)
    row = lambda i: (i, 0)
    in_specs = [
        pl.BlockSpec((tm, d), row),
        _const_spec((d, IN_COLS_PADDED)),
        _const_spec((Q_LORA, N_HEADS * LANES)),
        _const_spec((1, LANES)),
        pl.BlockSpec((tm, LANES), tbl),
        _const_spec((1, KV_LORA)),
        _const_spec((1, LANES)),
        pl.BlockSpec((tm, LANES), tbl),
        _const_spec((KV_LORA, N_HEADS * LANES)),
        _const_spec(w_v.shape),
        _const_spec((1, GMLP_WIDTH)),
        _const_spec((1, GMLP_WIDTH)),
        _const_spec((GMLP_GROUPS, GMLP_CHUNK, GMLP_CHUNK)),
        _const_spec((GMLP_CHUNK, GMLP_WIDTH)),
        _const_spec((1, GMLP_WIDTH)),
        _const_spec((LANES, LANES)),
        _const_spec((1, LANES)),
    ]
    out_shape = [
        jax.ShapeDtypeStruct((n, N_HEADS * LANES), BF16),
        jax.ShapeDtypeStruct((n, N_HEADS * LANES), BF16),
        (jax.ShapeDtypeStruct((N_HEADS * LANES, n), BF16) if v_transposed
         else jax.ShapeDtypeStruct((n, N_HEADS * V_HEAD), BF16)),
        jax.ShapeDtypeStruct((n, KV_LORA), F32),
        jax.ShapeDtypeStruct((n, QK_ROPE), F32),
        jax.ShapeDtypeStruct((n, GMLP_WIDTH), BF16),
    ]
    out_specs = [
        pl.BlockSpec((tm, N_HEADS * LANES), row),
        pl.BlockSpec((tm, N_HEADS * LANES), row),
        (pl.BlockSpec((N_HEADS * LANES, tm), lambda i: (0, i)) if v_transposed
         else pl.BlockSpec((tm, N_HEADS * V_HEAD), row)),
        pl.BlockSpec((tm, KV_LORA), row),
        pl.BlockSpec((tm, QK_ROPE), row),
        pl.BlockSpec((tm, GMLP_WIDTH), row),
    ]
    if want_vn:
        out_shape.append(jax.ShapeDtypeStruct((n, GMLP_WIDTH), F32))
        out_specs.append(pl.BlockSpec((tm, GMLP_WIDTH), row))
    return pl.pallas_call(
        functools.partial(_inproj_kernel, ws_mask=ws_mask, v_transposed=v_transposed),
        grid=grid, in_specs=in_specs, out_specs=out_specs, out_shape=out_shape,
        compiler_params=pltpu.CompilerParams(dimension_semantics=("arbitrary",),
                                             vmem_limit_bytes=VMEM_LIMIT_BYTES),
        name="inproj_vn" if want_vn else "inproj",
    )(x, lw["w_in"], lw["w_q"], lw["gq"], tq, lw["g_kv_lora"], lw["gkr"], tk,
      lw["w_kvk"], w_v, lw["ln_g"], lw["ln_b"], ws, bs, lw["g_out_gm"],
      jnp.asarray(_Q_SEGMENTS, BF16), jnp.asarray(_Q_SEGMENT_EPS))


ATTN_STREAM_LAG = 18


def _attn_prompt_kernel(q_ref, k_ref, vt_ref, bias_ref, g_ref, o_ref, s_scr, *, tq):
    s_len = q_ref.shape[0]
    bodies = [(qi, hh) for qi in reversed(range(s_len // tq)) for hh in range(2)]
    bases = np.cumsum([0] + [qi + 1 for qi, _ in bodies])
    state = [{} for _ in bodies]

    def a_steps(bi):
        qi, hh = bodies[bi]
        cols = slice(hh * LANES, (hh + 1) * LANES)
        rows = slice(qi * tq, (qi + 1) * tq)
        st = state[bi]

        def chunk(j):
            s = _dot_nt(k_ref[j * tq:(j + 1) * tq, cols], q_ref[rows, cols])
            if j == qi:
                s = s + bias_ref[...]
            s_scr[bases[bi] + j] = s
            cm = jnp.max(s.reshape(tq // 8, 8, tq), axis=0)
            st["m_run"] = cm if j == 0 else jnp.maximum(st["m_run"], cm)

        def finish():
            st["m"] = jnp.max(st["m_run"], axis=0, keepdims=True)

        return [functools.partial(chunk, j) for j in range(qi + 1)] + [finish]

    def b_steps(bi):
        qi, hh = bodies[bi]
        st = state[bi]

        def chunk(j):
            p = jnp.exp2(s_scr[bases[bi] + j] - st["m"]).astype(BF16)
            d = _dot(vt_ref[hh * LANES:(hh + 1) * LANES, j * tq:(j + 1) * tq], p)
            st["acc"] = d if j == 0 else st["acc"] + d

        def finish():
            acc = st["acc"]
            o = acc[0:V_HEAD] * (1.0 / acc[V_HEAD:V_HEAD + 1])
            r = lax.rsqrt(jnp.sum(o * o, axis=0, keepdims=True) * (1.0 / V_HEAD) + EPS)
            st["out"] = o * r * g_ref[hh * V_HEAD:(hh + 1) * V_HEAD, :]
            if hh == 1:
                pair_t = jnp.concatenate([state[bi - 1]["out"], st["out"]], axis=0)
                o_ref[qi * tq:(qi + 1) * tq, :] = pair_t.T.astype(BF16)

        return [functools.partial(chunk, j) for j in range(qi + 1)] + [finish]

    a_all = [f for bi in range(len(bodies)) for f in a_steps(bi)]
    b_all = [f for bi in range(len(bodies)) for f in b_steps(bi)]
    for i in range(len(a_all) + ATTN_STREAM_LAG):
        if i < len(a_all):
            a_all[i]()
        if i >= ATTN_STREAM_LAG:
            b_all[i - ATTN_STREAM_LAG]()


def _attn_prompt(q, k, vt, bias_t, g_t, *, batch, seq, tq):
    n = q.shape[0]
    pairs = N_HEADS // 2
    nq = seq // tq
    assert ATTN_STREAM_LAG > nq + 1
    return pl.pallas_call(
        functools.partial(_attn_prompt_kernel, tq=tq),
        grid=(batch, pairs),
        in_specs=[
            pl.BlockSpec((seq, 2 * LANES), lambda b, p: (b, p)),
            pl.BlockSpec((seq, 2 * LANES), lambda b, p: (b, p)),
            pl.BlockSpec((2 * LANES, seq), lambda b, p: (p, b)),
            pl.BlockSpec((tq, tq), lambda b, p: (0, 0)),
            pl.BlockSpec((2 * V_HEAD, tq), lambda b, p: (p, 0)),
        ],
        out_specs=pl.BlockSpec((seq, 2 * V_HEAD), lambda b, p: (b, p)),
        out_shape=jax.ShapeDtypeStruct((n, N_HEADS * V_HEAD), BF16),
        scratch_shapes=[pltpu.VMEM((nq * (nq + 1), tq, tq), F32)],
        compiler_params=pltpu.CompilerParams(dimension_semantics=("arbitrary", "arbitrary"),
                                             vmem_limit_bytes=VMEM_LIMIT_BYTES),
        name="attn_prompt",
    )(q, k, vt, bias_t, g_t)


SAMPLE_KEY_BLOCK = 1024


def _attn_sample_kernel(q_ref, cnew_ref, krnew_ref, ckv_ref, kr_ref, w2_ref, wkt_ref, wv_ref, bias_ref, gout_ref,
                        o_ref, s_scr, kr_scr, *, kblk):
    past = ckv_ref.shape[0]
    sq = q_ref.shape[0]
    q2 = jnp.concatenate([_dot(q_ref[:, h * LANES:(h + 1) * LANES], w2_ref[h]) for h in range(N_HEADS)],
                         axis=0).astype(BF16)
    qt = q2[:, 0:KV_LORA]
    qr = q2[:, KV_LORA:]
    kr_scr[...] = jnp.zeros(kr_scr.shape, F32)
    kr_scr[0:past, 0:QK_ROPE] = kr_ref[...]
    kr_scr[past:past + sq, 0:QK_ROPE] = krnew_ref[...]
    ones_blk = jnp.ones((kblk, LANES), BF16)

    def scores(cb, krb):
        kzt = _dot_nt(wkt_ref[...], cb)
        s1 = _dot_nt(qt, cb)
        s2 = _dot_nt(qr, krb)
        out = []
        for h in range(N_HEADS):
            kh = kzt[h * QK_NOPE:(h + 1) * QK_NOPE]
            r = lax.rsqrt(jnp.sum(kh * kh, axis=0, keepdims=True) + QK_NOPE * EPS)
            out.append(s1[h * sq:(h + 1) * sq] * r + s2[h * sq:(h + 1) * sq])
        return jnp.concatenate(out, axis=0)

    m_run = None
    for j in range(past // kblk):
        ks = slice(j * kblk, (j + 1) * kblk)
        s = scores(ckv_ref[ks, :].astype(BF16), kr_scr[ks, :].astype(BF16))
        s_scr[:, ks] = s
        for c in range(0, kblk, LANES):
            blk = s[:, c:c + LANES]
            m_run = blk if m_run is None else jnp.maximum(m_run, blk)
    cn = jnp.concatenate([cnew_ref[...], jnp.zeros((LANES - sq, KV_LORA), F32)], axis=0).astype(BF16)
    s_new = scores(cn, kr_scr[past:past + LANES, :].astype(BF16)) + bias_ref[...]
    m = jnp.max(jnp.maximum(m_run, s_new), axis=-1, keepdims=True)
    acc = _dot(jnp.exp2(s_new - m).astype(BF16), jnp.concatenate([cn, ones_blk[0:LANES]], axis=1))
    for j in range(past // kblk):
        ks = slice(j * kblk, (j + 1) * kblk)
        p = jnp.exp2(s_scr[:, ks] - m).astype(BF16)
        acc = acc + _dot(p, jnp.concatenate([ckv_ref[ks, :].astype(BF16), ones_blk], axis=1))
    o_lat = (acc[:, 0:KV_LORA] * (1.0 / acc[:, KV_LORA:])).astype(BF16)
    o_full = _dot(o_lat, wv_ref[...])
    head = lax.broadcasted_iota(jnp.int32, (sq, N_HEADS * V_HEAD), 1) >> 6
    o = jnp.zeros((sq, N_HEADS * V_HEAD), F32)
    for h in range(N_HEADS):
        o = jnp.where(head == h, o_full[h * sq:(h + 1) * sq], o)
    lo = lax.broadcasted_iota(jnp.int32, (sq, LANES), 1) < 64
    for c in range(0, N_HEADS * V_HEAD, LANES):
        oc = o[:, c:c + LANES]
        o_ref[:, c:c + LANES] = (oc * _half_rms_scale(oc, lo) * gout_ref[:, c:c + LANES]).astype(BF16)


def _attn_sample(q, cnew, krnew, cache_ckv, cache_krope, w2, wkt, w_kvv, bias, gout, *, layer, batch, seq):
    n = q.shape[0]
    past = cache_ckv.shape[2]
    assert V_HEAD == 64 and past % SAMPLE_KEY_BLOCK == 0 and seq <= LANES
    return pl.pallas_call(
        functools.partial(_attn_sample_kernel, kblk=SAMPLE_KEY_BLOCK),
        grid=(batch,),
        in_specs=[
            pl.BlockSpec((seq, N_HEADS * LANES), lambda b: (b, 0)),
            pl.BlockSpec((seq, KV_LORA), lambda b: (b, 0)),
            pl.BlockSpec((seq, QK_ROPE), lambda b: (b, 0)),
            pl.BlockSpec((None, None, past, KV_LORA), lambda b: (layer, b, 0, 0)),
            pl.BlockSpec((None, None, past, QK_ROPE), lambda b: (layer, b, 0, 0)),
            pl.BlockSpec((N_HEADS, LANES, 2 * LANES), lambda b: (0, 0, 0)),
            pl.BlockSpec((N_HEADS * QK_NOPE, KV_LORA), lambda b: (0, 0)),
            pl.BlockSpec((KV_LORA, N_HEADS * V_HEAD), lambda b: (0, 0)),
            pl.BlockSpec((N_HEADS * seq, LANES), lambda b: (0, 0)),
            pl.BlockSpec((1, N_HEADS * V_HEAD), lambda b: (0, 0)),
        ],
        out_specs=pl.BlockSpec((seq, N_HEADS * V_HEAD), lambda b: (b, 0)),
        out_shape=jax.ShapeDtypeStruct((n, N_HEADS * V_HEAD), BF16),
        scratch_shapes=[pltpu.VMEM((N_HEADS * seq, past), F32), pltpu.VMEM((past + LANES, LANES), F32)],
        compiler_params=pltpu.CompilerParams(dimension_semantics=("arbitrary",),
                                             vmem_limit_bytes=VMEM_LIMIT_BYTES),
        name="attn_sample",
    )(q, cnew, krnew, cache_ckv, cache_krope, w2, wkt, w_kvv, bias, gout)


def _ffn_kernel(x_ref, oa_ref, og_ref, wo_ref, wg_ref, wu_ref, wd_ref, y_ref, *, fchunk):
    aw = oa_ref.shape[1]
    x1 = x_ref[...] + _dot(oa_ref[...], wo_ref[0:aw, :]) + _dot(og_ref[...], wo_ref[aw:, :])
    h2 = (x1 * _rsqrt_sumsq(x1, x1.shape[1])).astype(BF16)
    acc = None
    for c in range(0, wg_ref.shape[1], fchunk):
        g = _dot(h2, wg_ref[:, c:c + fchunk])
        u = _dot(h2, wu_ref[:, c:c + fchunk])
        a = (g * jax.nn.sigmoid(g) * u).astype(BF16)
        d = _dot(a, wd_ref[c:c + fchunk, :])
        acc = d if acc is None else acc + d
    y_ref[...] = x1 + acc


def _ffn(x, oa, og, lw, *, tm, fchunk):
    n, d = x.shape
    dff = lw["w_gate"].shape[1]
    row = lambda i: (i, 0)
    return pl.pallas_call(
        functools.partial(_ffn_kernel, fchunk=fchunk),
        grid=(n // tm,),
        in_specs=[
            pl.BlockSpec((tm, d), row),
            pl.BlockSpec((tm, oa.shape[1]), row),
            pl.BlockSpec((tm, og.shape[1]), row),
            _const_spec((oa.shape[1] + og.shape[1], d)),
            _const_spec((d, dff)),
            _const_spec((d, dff)),
            _const_spec((dff, d)),
        ],
        out_specs=pl.BlockSpec((tm, d), row),
        out_shape=jax.ShapeDtypeStruct((n, d), F32),
        compiler_params=pltpu.CompilerParams(dimension_semantics=("arbitrary",),
                                             vmem_limit_bytes=VMEM_LIMIT_BYTES),
        name="outproj_ffn",
    )(x, oa, og, lw["w_o"], lw["w_gate"], lw["w_up"], lw["w_down"])


def _rope_tables(pos):
    inv = ROPE_THETA ** (-np.arange(HALF, dtype=np.float64) / HALF)
    ang = np.asarray(pos, np.float64)[:, None] * inv[None, :]
    cos, sin = np.cos(ang), np.sin(ang)
    n = ang.shape[0]
    tq = np.concatenate([np.ones((n, QK_NOPE)), cos, cos, -sin, sin], axis=1)
    tk = np.concatenate([cos, cos, -sin, sin, np.zeros((n, LANES - 2 * QK_ROPE))], axis=1)
    return tq.astype(np.float32), tk.astype(np.float32)


def _rope_fold_matrix():
    f = np.zeros((LANES, LANES), np.float32)
    for i in range(QK_ROPE):
        f[QK_NOPE + i, i] = 1.0
        f[QK_NOPE + QK_ROPE + i, i] = 1.0
    return f


_ROPE_FOLD = _rope_fold_matrix()


def _q_segment_matrix():
    m = np.zeros((LANES, LANES), np.float32)
    m[:QK_NOPE, :QK_NOPE] = 1.0
    m[QK_NOPE:QK_NOPE + QK_ROPE, QK_NOPE:] = 1.0
    return m


_Q_SEGMENTS = _q_segment_matrix()
_Q_SEGMENT_COUNT = np.concatenate([np.full(QK_NOPE, QK_NOPE), np.full(LANES - QK_NOPE, QK_ROPE)]).astype(np.float32)
_Q_SEGMENT_EPS = (_Q_SEGMENT_COUNT * EPS)[None, :]


def _swap_halves(g):
    return jnp.concatenate([g[..., HALF:], g[..., :HALF]], axis=-1)


def _layer_weights(l, g_mix, w_in, g_q_lora, w_q_up, g_qn, g_qr, g_kv_lora, g_kr, w_kv_up, g_kn,
                   gm_ln_g, gm_ln_b, g_out_att, g_out_gm, w_o, g_ffn, w_gate, w_up, w_down):
    d_model = w_in.shape[1]
    w = w_in[l]
    kr0 = Q_LORA + KV_LORA
    w_in_p = jnp.concatenate(
        [w[:, :kr0 + QK_ROPE], w[:, kr0 + HALF:kr0 + QK_ROPE], w[:, kr0:kr0 + HALF],
         jnp.zeros((d_model, LANES - 2 * QK_ROPE), F32), w[:, kr0 + QK_ROPE:]], axis=1)
    w_in_p = (w_in_p * (g_mix[l] * d_model ** 0.5)[:, None]).astype(BF16)
    wq = w_q_up[l].reshape(Q_LORA, N_HEADS, QK_HEAD)
    r1, r2 = wq[..., QK_NOPE:QK_NOPE + HALF], wq[..., QK_NOPE + HALF:]
    w_q = jnp.concatenate([wq[..., :QK_NOPE], r1, r2, r2, r1], axis=-1).reshape(Q_LORA, N_HEADS * LANES)
    w_q = w_q * (g_q_lora[l] * Q_LORA ** 0.5)[:, None]
    gq = (jnp.concatenate([g_qn[l] * g_kn[l] * QK_NOPE ** 0.5, g_qr[l], _swap_halves(g_qr[l])])
          * np.sqrt(_Q_SEGMENT_COUNT) * (QK_HEAD ** -0.5 * LOG2_E))
    gkr = jnp.concatenate([g_kr[l], _swap_halves(g_kr[l]), jnp.zeros((LANES - 2 * QK_ROPE,), F32)]) * QK_ROPE ** 0.5
    wkv = w_kv_up[l].reshape(KV_LORA, N_HEADS, QK_NOPE + V_HEAD)
    w_kvk = jnp.concatenate([wkv[..., :QK_NOPE], jnp.zeros((KV_LORA, N_HEADS, LANES - QK_NOPE), F32)],
                            axis=-1).reshape(KV_LORA, N_HEADS * LANES)
    w_kvv = wkv[..., QK_NOPE:].reshape(KV_LORA, N_HEADS * V_HEAD)
    w_kvv_t = jnp.concatenate([wkv[..., QK_NOPE:], jnp.zeros((KV_LORA, N_HEADS, LANES - V_HEAD), F32)],
                              axis=-1).reshape(KV_LORA, N_HEADS * LANES).T
    wk_t = jnp.transpose(wkv[..., :QK_NOPE], (1, 2, 0))
    w_abs = jnp.concatenate([wk_t, jnp.zeros((N_HEADS, LANES - QK_NOPE, KV_LORA), F32)], axis=1)
    w2 = jnp.concatenate([w_abs, jnp.broadcast_to(jnp.asarray(_ROPE_FOLD), (N_HEADS, LANES, LANES))], axis=2)
    g_ffn_col = (g_ffn[l] * d_model ** 0.5)[:, None]
    return {
        "w_in": w_in_p, "w_q": w_q.astype(BF16), "gq": gq[None, :],
        "g_kv_lora": g_kv_lora[l][None, :] * KV_LORA ** 0.5, "gkr": gkr[None, :],
        "w_kvk": w_kvk.astype(BF16), "w_kvv": w_kvv.astype(BF16),
        "w_kvv_t": w_kvv_t.astype(BF16),
        "w2": w2.astype(BF16), "wk_t": wk_t.reshape(N_HEADS * QK_NOPE, KV_LORA).astype(BF16),
        "ln_g": gm_ln_g[l].reshape(1, GMLP_WIDTH) * GMLP_GROUP_DIM ** 0.5, "ln_b": gm_ln_b[l].reshape(1, GMLP_WIDTH),
        "g_out_att": g_out_att[l].reshape(1, N_HEADS * V_HEAD),
        "g_out_gm": g_out_gm[l].reshape(1, GMLP_WIDTH) * GMLP_GROUP_DIM ** 0.5,
        "w_o": w_o[l].astype(BF16),
        "w_gate": (w_gate[l] * g_ffn_col).astype(BF16), "w_up": (w_up[l] * g_ffn_col).astype(BF16),
        "w_down": w_down[l].astype(BF16),
    }


def _chunk_bias(qpos, kpos):
    vis = (kpos[None, :] // CHUNK) <= (qpos[:, None] // CHUNK)
    return np.where(vis, 0.0, NEG_INF).astype(np.float32)


def kernel(x_prompt, x_sample, cache_ckv, cache_krope, g_mix, w_in, g_q_lora, w_q_up, g_qn, g_qr, g_kv_lora, g_kr,
           w_kv_up, g_kn, gm_ln_g, gm_ln_b, gm_w_s, gm_b_s, g_out_att, g_out_gm, w_o, g_ffn, w_gate, w_up, w_down):
    batch, seq, d = x_prompt.shape
    dbatch, dseq, _ = x_sample.shape
    depth, _, past, _ = cache_ckv.shape
    assert seq % GMLP_CHUNK == 0 and GMLP_CHUNK % dseq == 0 and (dbatch * dseq) % GMLP_CHUNK == 0

    tq_attn = 256
    tm_in = 1024
    tm_ffn = 512
    tm_s = dbatch * dseq

    tq_p, tk_p = (jnp.asarray(t) for t in _rope_tables(np.arange(seq)))
    tq_s, tk_s = (jnp.asarray(np.tile(t, (dbatch, 1))) for t in _rope_tables(past + np.arange(dseq)))
    bias_p = jnp.asarray(_chunk_bias(np.arange(tq_attn), np.arange(tq_attn)))
    bias_s = np.full((dseq, LANES), NEG_INF, np.float32)
    bias_s[:, :dseq] = _chunk_bias(past + np.arange(dseq), past + np.arange(dseq))
    bias_s = jnp.asarray(np.tile(bias_s, (N_HEADS, 1)))
    reps = GMLP_CHUNK // dseq

    yp = x_prompt.reshape(batch * seq, d)
    ys = x_sample.reshape(dbatch * dseq, d)
    ckv_p, kr_p, ckv_s, kr_s, gv_s = [], [], [], [], []
    for l in range(depth):
        lw = _layer_weights(l, g_mix, w_in, g_q_lora, w_q_up, g_qn, g_qr, g_kv_lora, g_kr, w_kv_up, g_kn,
                            gm_ln_g, gm_ln_b, g_out_att, g_out_gm, w_o, g_ffn, w_gate, w_up, w_down)
        ws_p = gm_w_s[l]
        bs_p = jnp.repeat(gm_b_s[l].T, GMLP_GROUP_DIM, axis=1)
        ws_s = jnp.tile(gm_w_s[l][:, :dseq, :dseq], (1, reps, reps))
        bs_s = jnp.tile(jnp.repeat(gm_b_s[l][:, :dseq].T, GMLP_GROUP_DIM, axis=1), (reps, 1))

        q, k, vt, c, r, ogm = _inproj(yp, lw, tq_p, tk_p, ws_p, bs_p, tm=tm_in, table_blocks=seq // tm_in,
                                      ws_mask="causal", want_vn=False, v_transposed=True)
        g_t = jnp.broadcast_to(lw["g_out_att"].reshape(N_HEADS * V_HEAD, 1), (N_HEADS * V_HEAD, tq_attn))
        oatt = _attn_prompt(q, k, vt, bias_p.T, g_t, batch=batch, seq=seq, tq=tq_attn)
        yp = _ffn(yp, oatt, ogm, lw, tm=tm_ffn, fchunk=256)
        ckv_p.append(c.reshape(batch, seq, KV_LORA))
        kr_p.append(r.reshape(batch, seq, QK_ROPE))

        q, k, v, c, r, ogm, vn = _inproj(ys, lw, tq_s, tk_s, ws_s, bs_s, tm=tm_s, table_blocks=1,
                                         ws_mask="blockdiag", want_vn=True, v_transposed=False)
        oatt = _attn_sample(q, c, r, cache_ckv, cache_krope, lw["w2"], lw["wk_t"], lw["w_kvv"], bias_s,
                            lw["g_out_att"], layer=l, batch=dbatch, seq=dseq)
        ys = _ffn(ys, oatt, ogm, lw, tm=tm_s, fchunk=256)
        ckv_s.append(c.reshape(dbatch, dseq, KV_LORA))
        kr_s.append(r.reshape(dbatch, dseq, QK_ROPE))
        gv_s.append(vn.reshape(dbatch, dseq, GMLP_WIDTH))

    return (yp.reshape(batch, seq, d), ys.reshape(dbatch, dseq, d), jnp.stack(ckv_p), jnp.stack(kr_p),
            jnp.stack(ckv_s), jnp.stack(kr_s), jnp.stack(gv_s))
```

```python
import functools

import numpy as np
import jax
import jax.numpy as jnp
from jax import lax
from jax.experimental import pallas as pl
from jax.experimental.pallas import tpu as pltpu

F32 = jnp.float32
BF16 = jnp.bfloat16

CHUNK = 64
N_HEADS = 8
QK_NOPE = 64
QK_ROPE = 32
QK_HEAD = QK_NOPE + QK_ROPE
V_HEAD = 64
Q_LORA = 256
KV_LORA = 128
ROPE_THETA = 10000.0
GMLP_GROUPS = 8
GMLP_GROUP_DIM = 64
GMLP_WIDTH = GMLP_GROUPS * GMLP_GROUP_DIM
GMLP_CHUNK = 128
EPS = 1e-6
NEG_INF = -1e30
LOG2_E = 1.4426950408889634

LANES = 128
V7X_VMEM_BYTES = 64 * 1024 * 1024
VMEM_LIMIT_BYTES = (V7X_VMEM_BYTES * 3) // 4

HALF = QK_ROPE // 2
IN_COLS_PADDED = Q_LORA + KV_LORA + LANES + 2 * GMLP_WIDTH
COL_CKV = Q_LORA
COL_KR = Q_LORA + KV_LORA
COL_U = COL_KR + LANES
COL_V = COL_U + GMLP_WIDTH

_NT = (((1,), (1,)), ((), ()))


def _dot(a, b):
    return jnp.dot(a, b, preferred_element_type=F32)


def _dot_nt(a, b):
    return lax.dot_general(a, b, _NT, preferred_element_type=F32)


def _rsqrt_sumsq(x, n):
    return lax.rsqrt(jnp.sum(x * x, axis=-1, keepdims=True) + n * EPS)


def _half_sums(x, lo):
    s_lo = jnp.sum(jnp.where(lo, x, 0.0), axis=-1, keepdims=True)
    s_hi = jnp.sum(jnp.where(lo, 0.0, x), axis=-1, keepdims=True)
    return jnp.where(lo, s_lo, s_hi)


def _half_rms_scale(x, lo):
    return lax.rsqrt(_half_sums(x * x, lo) * (1.0 / 64) + EPS)


def _half_rsqrt_sumsq(x, lo):
    return lax.rsqrt(_half_sums(x * x, lo) + 64 * EPS)


def _gelu_tanh(t):
    c = 0.7978845608028654
    h = 0.5 * t
    return h + h * jnp.tanh(t * (c + (c * 0.044715) * (t * t)))


INPROJ_SUB_ROWS = 256


def _inproj_kernel(x_ref, win_ref, wq_ref, gq_ref, tq_ref, gkv_ref, gkr_ref, tk_ref,
                   wkk_ref, wkv_ref, lng_ref, lnb_ref, ws_ref, bs_ref, ggm_ref, seg_ref, epsq_ref,
                   q_out, k_out, v_out, ckv_out, kr_out, ogm_out, *vn_out, ws_mask, v_transposed):
    tm = x_ref.shape[0]
    sub = min(tm, INPROJ_SUB_ROWS)
    lane = lax.broadcasted_iota(jnp.int32, (sub, LANES), 1)
    lo = lane < 64
    lo_c = lax.broadcasted_iota(jnp.int32, (GMLP_CHUNK, LANES), 1) < 64
    row = lax.broadcasted_iota(jnp.int32, (GMLP_CHUNK, GMLP_CHUNK), 0) // CHUNK
    col = lax.broadcasted_iota(jnp.int32, (GMLP_CHUNK, GMLP_CHUNK), 1) // CHUNK
    keep = (row >= col) if ws_mask == "causal" else (row == col)

    def project(r0):
        x = x_ref[r0:r0 + sub, :]
        h = (x * _rsqrt_sumsq(x, x.shape[1])).astype(BF16)
        return _dot(h, win_ref[...])

    def finish(z, r0):
        rows = slice(r0, r0 + sub)
        ql = z[:, 0:Q_LORA]
        qn = (ql * _rsqrt_sumsq(ql, Q_LORA)).astype(BF16)
        qz = _dot(qn, wq_ref[...])
        tq = tq_ref[rows, :] * gq_ref[...]
        for hd in range(N_HEADS):
            cols = slice(hd * LANES, (hd + 1) * LANES)
            xh = qz[:, cols]
            ss = _dot((xh * xh).astype(BF16), seg_ref[...])
            q_out[rows, cols] = (xh * lax.rsqrt(ss + epsq_ref[...]) * tq).astype(BF16)

        y0 = z[:, COL_KR:COL_KR + LANES]
        s_k = jnp.sum(jnp.where(lane < QK_ROPE, y0 * y0, 0.0), axis=-1, keepdims=True)
        y = y0 * lax.rsqrt(s_k + QK_ROPE * EPS) * (gkr_ref[...] * tk_ref[rows, :])
        r64 = pltpu.roll(y, 64, 1)
        r32 = pltpu.roll(y, 32, 1)
        r96 = pltpu.roll(y, 96, 1)
        if v_transposed:
            kr_out[:, rows] = (y + r96).T[0:QK_ROPE, :]
        else:
            kr_out[rows, :] = (y + r96)[:, 0:QK_ROPE]
        kr_hi = jnp.where(lo, 0.0, r64 + r32 + r96)

        c0 = z[:, COL_CKV:COL_CKV + KV_LORA]
        cn = c0 * _rsqrt_sumsq(c0, KV_LORA) * gkv_ref[...]
        ckv_out[rows, :] = cn
        cb = cn.astype(BF16)
        kz = _dot(cb, wkk_ref[...])
        for hd in range(N_HEADS):
            cols = slice(hd * LANES, (hd + 1) * LANES)
            xh = kz[:, cols]
            kn = xh * _rsqrt_sumsq(xh, QK_NOPE)
            k_out[rows, cols] = jnp.where(lo, kn, kr_hi).astype(BF16)
        if v_transposed:
            vt = _dot_nt(wkv_ref[...], cb)
            vrow = lax.broadcasted_iota(jnp.int32, vt.shape, 0)
            v_out[:, rows] = jnp.where((vrow & (LANES - 1)) == V_HEAD, 1.0, vt).astype(BF16)
        else:
            v_out[rows, :] = _dot(cb, wkv_ref[...]).astype(BF16)

        for p in range(GMLP_GROUPS // 2):
            sl = slice(p * LANES, (p + 1) * LANES)
            w_pair = jnp.concatenate(
                [jnp.where(keep, ws_ref[2 * p], 0.0), jnp.where(keep, ws_ref[2 * p + 1], 0.0)], axis=0).astype(BF16)
            ug = _gelu_tanh(z[:, COL_U + p * LANES:COL_U + (p + 1) * LANES])
            vg = _gelu_tanh(z[:, COL_V + p * LANES:COL_V + (p + 1) * LANES])
            xc = vg - _half_sums(vg, lo) * (1.0 / GMLP_GROUP_DIM)
            vn = xc * _half_rsqrt_sumsq(xc, lo) * lng_ref[:, sl] + lnb_ref[:, sl]
            if vn_out:
                vn_out[0][rows, sl] = vn
            vnb = vn.astype(BF16)
            for c in range(sub // GMLP_CHUNK):
                crows = slice(c * GMLP_CHUNK, (c + 1) * GMLP_CHUNK)
                res = _dot(w_pair, vnb[crows])
                mixed = jnp.where(lo_c, res[0:GMLP_CHUNK], res[GMLP_CHUNK:]) + bs_ref[:, sl]
                og = ug[crows] * mixed
                ogm_out[r0 + c * GMLP_CHUNK:r0 + (c + 1) * GMLP_CHUNK, sl] = (
                    og * _half_rsqrt_sumsq(og, lo_c) * ggm_ref[:, sl]).astype(BF16)

    starts = list(range(0, tm, sub))
    z_next = project(starts[0])
    for i, r0 in enumerate(starts):
        z_cur = z_next
        if i + 1 < len(starts):
            z_next = project(starts[i + 1])
        finish(z_cur, r0)


def _const_spec(shape):
    return pl.BlockSpec(shape, lambda *_: (0,) * len(shape), pipeline_mode=pl.Buffered(1))


def _layer_spec(stacked, layer):
    return pl.BlockSpec((None,) + stacked.shape[1:], lambda *_: (layer, 0, 0), pipeline_mode=pl.Buffered(1))


def _inproj(x, big, lw, tq, tk, ws, bs, *, layer, tm, table_blocks, ws_mask, want_vn, v_transposed):
    n, d = x.shape
    w_v = lw["w_kvv_t"] if v_transposed else lw["w_kvv"]
    grid = (n // tm,)
    tbl = lambda i: (i % table_blocks, 0)
    row = lambda i: (i, 0)
    in_specs = [
        pl.BlockSpec((tm, d), row),
        _layer_spec(big["w_in"], layer),
        _const_spec((Q_LORA, N_HEADS * LANES)),
        _const_spec((1, LANES)),
        pl.BlockSpec((tm, LANES), tbl),
        _const_spec((1, KV_LORA)),
        _const_spec((1, LANES)),
        pl.BlockSpec((tm, LANES), tbl),
        _const_spec((KV_LORA, N_HEADS * LANES)),
        _const_spec(w_v.shape),
        _const_spec((1, GMLP_WIDTH)),
        _const_spec((1, GMLP_WIDTH)),
        _const_spec((GMLP_GROUPS, GMLP_CHUNK, GMLP_CHUNK)),
        _const_spec((GMLP_CHUNK, GMLP_WIDTH)),
        _const_spec((1, GMLP_WIDTH)),
        _const_spec((LANES, LANES)),
        _const_spec((1, LANES)),
    ]
    out_shape = [
        jax.ShapeDtypeStruct((n, N_HEADS * LANES), BF16),
        jax.ShapeDtypeStruct((n, N_HEADS * LANES), BF16),
        (jax.ShapeDtypeStruct((N_HEADS * LANES, n), BF16) if v_transposed
         else jax.ShapeDtypeStruct((n, N_HEADS * V_HEAD), BF16)),
        jax.ShapeDtypeStruct((n, KV_LORA), F32),
        (jax.ShapeDtypeStruct((QK_ROPE, n), F32) if v_transposed
         else jax.ShapeDtypeStruct((n, QK_ROPE), F32)),
        jax.ShapeDtypeStruct((n, GMLP_WIDTH), BF16),
    ]
    out_specs = [
        pl.BlockSpec((tm, N_HEADS * LANES), row),
        pl.BlockSpec((tm, N_HEADS * LANES), row),
        (pl.BlockSpec((N_HEADS * LANES, tm), lambda i: (0, i)) if v_transposed
         else pl.BlockSpec((tm, N_HEADS * V_HEAD), row)),
        pl.BlockSpec((tm, KV_LORA), row),
        (pl.BlockSpec((QK_ROPE, tm), lambda i: (0, i)) if v_transposed else pl.BlockSpec((tm, QK_ROPE), row)),
        pl.BlockSpec((tm, GMLP_WIDTH), row),
    ]
    if want_vn:
        out_shape.append(jax.ShapeDtypeStruct((n, GMLP_WIDTH), F32))
        out_specs.append(pl.BlockSpec((tm, GMLP_WIDTH), row))
    return pl.pallas_call(
        functools.partial(_inproj_kernel, ws_mask=ws_mask, v_transposed=v_transposed),
        grid=grid, in_specs=in_specs, out_specs=out_specs, out_shape=out_shape,
        compiler_params=pltpu.CompilerParams(dimension_semantics=("arbitrary",),
                                             vmem_limit_bytes=VMEM_LIMIT_BYTES),
        name="inproj_vn" if want_vn else "inproj",
    )(x, big["w_in"], lw["w_q"], lw["gq"], tq, lw["g_kv_lora"], lw["gkr"], tk,
      lw["w_kvk"], w_v, lw["ln_g"], lw["ln_b"], ws, bs, lw["g_out_gm"],
      jnp.asarray(_Q_SEGMENTS, BF16), jnp.asarray(_Q_SEGMENT_EPS))


ATTN_STREAM_LAG = 18


def _attn_prompt_kernel(q_ref, k_ref, vt_ref, bias_ref, g_ref, o_ref, s_scr, *, tq):
    s_len = q_ref.shape[0]
    bodies = [(qi, hh) for qi in reversed(range(s_len // tq)) for hh in range(2)]
    bases = np.cumsum([0] + [qi + 1 for qi, _ in bodies])
    state = [{} for _ in bodies]

    def a_steps(bi):
        qi, hh = bodies[bi]
        cols = slice(hh * LANES, (hh + 1) * LANES)
        rows = slice(qi * tq, (qi + 1) * tq)
        st = state[bi]

        def chunk(j):
            s = _dot_nt(k_ref[j * tq:(j + 1) * tq, cols], q_ref[rows, cols])
            if j == qi:
                s = s + bias_ref[...]
            s_scr[bases[bi] + j] = s
            cm = jnp.max(s.reshape(tq // 8, 8, tq), axis=0)
            st["m_run"] = cm if j == 0 else jnp.maximum(st["m_run"], cm)

        def finish():
            st["m"] = jnp.max(st["m_run"], axis=0, keepdims=True)

        return [functools.partial(chunk, j) for j in range(qi + 1)] + [finish]

    def b_steps(bi):
        qi, hh = bodies[bi]
        st = state[bi]

        def chunk(j):
            p = jnp.exp2(s_scr[bases[bi] + j] - st["m"]).astype(BF16)
            d = _dot(vt_ref[hh * LANES:(hh + 1) * LANES, j * tq:(j + 1) * tq], p)
            st["acc"] = d if j == 0 else st["acc"] + d

        def finish():
            acc = st["acc"]
            o = acc[0:V_HEAD] * (1.0 / acc[V_HEAD:V_HEAD + 1])
            r = lax.rsqrt(jnp.sum(o * o, axis=0, keepdims=True) * (1.0 / V_HEAD) + EPS)
            st["out"] = o * r * g_ref[hh * V_HEAD:(hh + 1) * V_HEAD, :]
            if hh == 1:
                pair_t = jnp.concatenate([state[bi - 1]["out"], st["out"]], axis=0)
                o_ref[qi * tq:(qi + 1) * tq, :] = pair_t.T.astype(BF16)

        return [functools.partial(chunk, j) for j in range(qi + 1)] + [finish]

    a_all = [f for bi in range(len(bodies)) for f in a_steps(bi)]
    b_all = [f for bi in range(len(bodies)) for f in b_steps(bi)]
    for i in range(len(a_all) + ATTN_STREAM_LAG):
        if i < len(a_all):
            a_all[i]()
        if i >= ATTN_STREAM_LAG:
            b_all[i - ATTN_STREAM_LAG]()


def _attn_prompt(q, k, vt, bias_t, g_t, *, batch, seq, tq):
    n = q.shape[0]
    pairs = N_HEADS // 2
    nq = seq // tq
    assert ATTN_STREAM_LAG > nq + 1
    return pl.pallas_call(
        functools.partial(_attn_prompt_kernel, tq=tq),
        grid=(batch, pairs),
        in_specs=[
            pl.BlockSpec((seq, 2 * LANES), lambda b, p: (b, p)),
            pl.BlockSpec((seq, 2 * LANES), lambda b, p: (b, p)),
            pl.BlockSpec((2 * LANES, seq), lambda b, p: (p, b)),
            pl.BlockSpec((tq, tq), lambda b, p: (0, 0)),
            pl.BlockSpec((2 * V_HEAD, tq), lambda b, p: (p, 0)),
        ],
        out_specs=pl.BlockSpec((seq, 2 * V_HEAD), lambda b, p: (b, p)),
        out_shape=jax.ShapeDtypeStruct((n, N_HEADS * V_HEAD), BF16),
        scratch_shapes=[pltpu.VMEM((nq * (nq + 1), tq, tq), F32)],
        compiler_params=pltpu.CompilerParams(dimension_semantics=("arbitrary", "arbitrary"),
                                             vmem_limit_bytes=VMEM_LIMIT_BYTES),
        name="attn_prompt",
    )(q, k, vt, bias_t, g_t)


SAMPLE_KEY_BLOCK = 1024


def _attn_sample_kernel(q_ref, cnew_ref, krnew_ref, ckv_ref, krt_ref, w2_ref, wkt_ref, wv_ref, bias_ref, gout_ref,
                        o_ref, s_scr, kr_scr, *, kblk):
    past = ckv_ref.shape[0]
    sq = q_ref.shape[0]
    q2 = jnp.concatenate([_dot(q_ref[:, h * LANES:(h + 1) * LANES], w2_ref[h]) for h in range(N_HEADS)],
                         axis=0).astype(BF16)
    qt = q2[:, 0:KV_LORA]
    qr = q2[:, KV_LORA:]
    kr_t = jnp.concatenate([krt_ref[...].astype(BF16), jnp.zeros((LANES - QK_ROPE, past), BF16)], axis=0)
    kr_scr[...] = jnp.zeros(kr_scr.shape, F32)
    kr_scr[0:sq, 0:QK_ROPE] = krnew_ref[...]
    ones_blk = jnp.ones((kblk, LANES), BF16)

    def scores(cb, s2):
        kzt = _dot_nt(wkt_ref[...], cb)
        s1 = _dot_nt(qt, cb)
        out = []
        for h in range(N_HEADS):
            kh = kzt[h * QK_NOPE:(h + 1) * QK_NOPE]
            r = lax.rsqrt(jnp.sum(kh * kh, axis=0, keepdims=True) + QK_NOPE * EPS)
            out.append(s1[h * sq:(h + 1) * sq] * r + s2[h * sq:(h + 1) * sq])
        return jnp.concatenate(out, axis=0)

    m_run = None
    for j in range(past // kblk):
        ks = slice(j * kblk, (j + 1) * kblk)
        s = scores(ckv_ref[ks, :].astype(BF16), _dot(qr, kr_t[:, ks]))
        s_scr[:, ks] = s
        for c in range(0, kblk, LANES):
            blk = s[:, c:c + LANES]
            m_run = blk if m_run is None else jnp.maximum(m_run, blk)
    cn = jnp.concatenate([cnew_ref[...], jnp.zeros((LANES - sq, KV_LORA), F32)], axis=0).astype(BF16)
    s_new = scores(cn, _dot_nt(qr, kr_scr[...].astype(BF16))) + bias_ref[...]
    m = jnp.max(jnp.maximum(m_run, s_new), axis=-1, keepdims=True)
    acc = _dot(jnp.exp2(s_new - m).astype(BF16), jnp.concatenate([cn, ones_blk[0:LANES]], axis=1))
    for j in range(past // kblk):
        ks = slice(j * kblk, (j + 1) * kblk)
        p = jnp.exp2(s_scr[:, ks] - m).astype(BF16)
        acc = acc + _dot(p, jnp.concatenate([ckv_ref[ks, :].astype(BF16), ones_blk], axis=1))
    o_lat = (acc[:, 0:KV_LORA] * (1.0 / acc[:, KV_LORA:])).astype(BF16)
    o_full = _dot(o_lat, wv_ref[...])
    head = lax.broadcasted_iota(jnp.int32, (sq, N_HEADS * V_HEAD), 1) >> 6
    o = jnp.zeros((sq, N_HEADS * V_HEAD), F32)
    for h in range(N_HEADS):
        o = jnp.where(head == h, o_full[h * sq:(h + 1) * sq], o)
    lo = lax.broadcasted_iota(jnp.int32, (sq, LANES), 1) < 64
    for c in range(0, N_HEADS * V_HEAD, LANES):
        oc = o[:, c:c + LANES]
        o_ref[:, c:c + LANES] = (oc * _half_rms_scale(oc, lo) * gout_ref[:, c:c + LANES]).astype(BF16)


def _attn_sample(q, cnew, krnew, cache_ckv, kr_cache_t, w2, wkt, w_kvv, bias, gout, *, layer, batch, seq):
    n = q.shape[0]
    past = cache_ckv.shape[2]
    assert V_HEAD == 64 and past % SAMPLE_KEY_BLOCK == 0 and seq <= LANES
    return pl.pallas_call(
        functools.partial(_attn_sample_kernel, kblk=SAMPLE_KEY_BLOCK),
        grid=(batch,),
        in_specs=[
            pl.BlockSpec((seq, N_HEADS * LANES), lambda b: (b, 0)),
            pl.BlockSpec((seq, KV_LORA), lambda b: (b, 0)),
            pl.BlockSpec((seq, QK_ROPE), lambda b: (b, 0)),
            pl.BlockSpec((None, None, past, KV_LORA), lambda b: (layer, b, 0, 0)),
            pl.BlockSpec((None, None, QK_ROPE, past), lambda b: (layer, b, 0, 0)),
            pl.BlockSpec((N_HEADS, LANES, 2 * LANES), lambda b: (0, 0, 0)),
            pl.BlockSpec((N_HEADS * QK_NOPE, KV_LORA), lambda b: (0, 0)),
            pl.BlockSpec((KV_LORA, N_HEADS * V_HEAD), lambda b: (0, 0)),
            pl.BlockSpec((N_HEADS * seq, LANES), lambda b: (0, 0)),
            pl.BlockSpec((1, N_HEADS * V_HEAD), lambda b: (0, 0)),
        ],
        out_specs=pl.BlockSpec((seq, N_HEADS * V_HEAD), lambda b: (b, 0)),
        out_shape=jax.ShapeDtypeStruct((n, N_HEADS * V_HEAD), BF16),
        scratch_shapes=[pltpu.VMEM((N_HEADS * seq, past), F32), pltpu.VMEM((LANES, LANES), F32)],
        compiler_params=pltpu.CompilerParams(dimension_semantics=("arbitrary",),
                                             vmem_limit_bytes=VMEM_LIMIT_BYTES),
        name="attn_sample",
    )(q, cnew, krnew, cache_ckv, kr_cache_t, w2, wkt, w_kvv, bias, gout)


def _ffn_kernel(x_ref, oa_ref, og_ref, wo_ref, wg_ref, wu_ref, wd_ref, y_ref, *, fchunk):
    aw = oa_ref.shape[1]
    x1 = x_ref[...] + _dot(oa_ref[...], wo_ref[0:aw, :]) + _dot(og_ref[...], wo_ref[aw:, :])
    h2 = (x1 * _rsqrt_sumsq(x1, x1.shape[1])).astype(BF16)
    acc = None
    for c in range(0, wg_ref.shape[1], fchunk):
        g = _dot(h2, wg_ref[:, c:c + fchunk])
        u = _dot(h2, wu_ref[:, c:c + fchunk])
        a = (g * jax.nn.sigmoid(g) * u).astype(BF16)
        d = _dot(a, wd_ref[c:c + fchunk, :])
        acc = d if acc is None else acc + d
    y_ref[...] = x1 + acc


def _ffn(x, oa, og, big, *, layer, tm, fchunk):
    n, d = x.shape
    row = lambda i: (i, 0)
    return pl.pallas_call(
        functools.partial(_ffn_kernel, fchunk=fchunk),
        grid=(n // tm,),
        in_specs=[
            pl.BlockSpec((tm, d), row),
            pl.BlockSpec((tm, oa.shape[1]), row),
            pl.BlockSpec((tm, og.shape[1]), row),
            _layer_spec(big["w_o"], layer),
            _layer_spec(big["w_gate"], layer),
            _layer_spec(big["w_up"], layer),
            _layer_spec(big["w_down"], layer),
        ],
        out_specs=pl.BlockSpec((tm, d), row),
        out_shape=jax.ShapeDtypeStruct((n, d), F32),
        compiler_params=pltpu.CompilerParams(dimension_semantics=("arbitrary",),
                                             vmem_limit_bytes=VMEM_LIMIT_BYTES),
        name="outproj_ffn",
    )(x, oa, og, big["w_o"], big["w_gate"], big["w_up"], big["w_down"])


def _rope_tables(pos):
    inv = ROPE_THETA ** (-np.arange(HALF, dtype=np.float64) / HALF)
    ang = np.asarray(pos, np.float64)[:, None] * inv[None, :]
    cos, sin = np.cos(ang), np.sin(ang)
    n = ang.shape[0]
    tq = np.concatenate([np.ones((n, QK_NOPE)), cos, cos, -sin, sin], axis=1)
    tk = np.concatenate([cos, cos, -sin, sin, np.zeros((n, LANES - 2 * QK_ROPE))], axis=1)
    return tq.astype(np.float32), tk.astype(np.float32)


def _rope_fold_matrix():
    f = np.zeros((LANES, LANES), np.float32)
    for i in range(QK_ROPE):
        f[QK_NOPE + i, i] = 1.0
        f[QK_NOPE + QK_ROPE + i, i] = 1.0
    return f


_ROPE_FOLD = _rope_fold_matrix()


def _q_segment_matrix():
    m = np.zeros((LANES, LANES), np.float32)
    m[:QK_NOPE, :QK_NOPE] = 1.0
    m[QK_NOPE:QK_NOPE + QK_ROPE, QK_NOPE:] = 1.0
    return m


_Q_SEGMENTS = _q_segment_matrix()
_Q_SEGMENT_COUNT = np.concatenate([np.full(QK_NOPE, QK_NOPE), np.full(LANES - QK_NOPE, QK_ROPE)]).astype(np.float32)
_Q_SEGMENT_EPS = (_Q_SEGMENT_COUNT * EPS)[None, :]


def _swap_halves(g):
    return jnp.concatenate([g[..., HALF:], g[..., :HALF]], axis=-1)


def _stacked_weights(g_mix, w_in, w_o, g_ffn, w_gate, w_up, w_down):
    depth, d_model, _ = w_in.shape
    w = (w_in * (g_mix * d_model ** 0.5)[:, :, None]).astype(BF16)
    kr0 = Q_LORA + KV_LORA
    w_in_p = jnp.concatenate(
        [w[..., :kr0 + QK_ROPE], w[..., kr0 + HALF:kr0 + QK_ROPE], w[..., kr0:kr0 + HALF],
         jnp.zeros((depth, d_model, LANES - 2 * QK_ROPE), BF16), w[..., kr0 + QK_ROPE:]], axis=-1)
    g_ffn_col = (g_ffn * d_model ** 0.5)[:, :, None]
    return {"w_in": w_in_p, "w_o": w_o.astype(BF16), "w_gate": (w_gate * g_ffn_col).astype(BF16),
            "w_up": (w_up * g_ffn_col).astype(BF16), "w_down": w_down.astype(BF16)}


def _layer_weights(l, g_q_lora, w_q_up, g_qn, g_qr, g_kv_lora, g_kr, w_kv_up, g_kn,
                   gm_ln_g, gm_ln_b, g_out_att, g_out_gm):
    wq = w_q_up[l].reshape(Q_LORA, N_HEADS, QK_HEAD)
    r1, r2 = wq[..., QK_NOPE:QK_NOPE + HALF], wq[..., QK_NOPE + HALF:]
    w_q = jnp.concatenate([wq[..., :QK_NOPE], r1, r2, r2, r1], axis=-1).reshape(Q_LORA, N_HEADS * LANES)
    w_q = w_q * (g_q_lora[l] * Q_LORA ** 0.5)[:, None]
    gq = (jnp.concatenate([g_qn[l] * g_kn[l] * QK_NOPE ** 0.5, g_qr[l], _swap_halves(g_qr[l])])
          * np.sqrt(_Q_SEGMENT_COUNT) * (QK_HEAD ** -0.5 * LOG2_E))
    gkr = jnp.concatenate([g_kr[l], _swap_halves(g_kr[l]), jnp.zeros((LANES - 2 * QK_ROPE,), F32)]) * QK_ROPE ** 0.5
    wkv = w_kv_up[l].reshape(KV_LORA, N_HEADS, QK_NOPE + V_HEAD)
    w_kvk = jnp.concatenate([wkv[..., :QK_NOPE], jnp.zeros((KV_LORA, N_HEADS, LANES - QK_NOPE), F32)],
                            axis=-1).reshape(KV_LORA, N_HEADS * LANES)
    w_kvv = wkv[..., QK_NOPE:].reshape(KV_LORA, N_HEADS * V_HEAD)
    w_kvv_t = jnp.concatenate([wkv[..., QK_NOPE:], jnp.zeros((KV_LORA, N_HEADS, LANES - V_HEAD), F32)],
                              axis=-1).reshape(KV_LORA, N_HEADS * LANES).T
    wk_t = jnp.transpose(wkv[..., :QK_NOPE], (1, 2, 0))
    w_abs = jnp.concatenate([wk_t, jnp.zeros((N_HEADS, LANES - QK_NOPE, KV_LORA), F32)], axis=1)
    w2 = jnp.concatenate([w_abs, jnp.broadcast_to(jnp.asarray(_ROPE_FOLD), (N_HEADS, LANES, LANES))], axis=2)
    return {
        "w_q": w_q.astype(BF16), "gq": gq[None, :],
        "g_kv_lora": g_kv_lora[l][None, :] * KV_LORA ** 0.5, "gkr": gkr[None, :],
        "w_kvk": w_kvk.astype(BF16), "w_kvv": w_kvv.astype(BF16),
        "w_kvv_t": w_kvv_t.astype(BF16),
        "w2": w2.astype(BF16), "wk_t": wk_t.reshape(N_HEADS * QK_NOPE, KV_LORA).astype(BF16),
        "ln_g": gm_ln_g[l].reshape(1, GMLP_WIDTH) * GMLP_GROUP_DIM ** 0.5, "ln_b": gm_ln_b[l].reshape(1, GMLP_WIDTH),
        "g_out_att": g_out_att[l].reshape(1, N_HEADS * V_HEAD),
        "g_out_gm": g_out_gm[l].reshape(1, GMLP_WIDTH) * GMLP_GROUP_DIM ** 0.5,
    }


def _chunk_bias(qpos, kpos):
    vis = (kpos[None, :] // CHUNK) <= (qpos[:, None] // CHUNK)
    return np.where(vis, 0.0, NEG_INF).astype(np.float32)


def kernel(x_prompt, x_sample, cache_ckv, cache_krope, g_mix, w_in, g_q_lora, w_q_up, g_qn, g_qr, g_kv_lora, g_kr,
           w_kv_up, g_kn, gm_ln_g, gm_ln_b, gm_w_s, gm_b_s, g_out_att, g_out_gm, w_o, g_ffn, w_gate, w_up, w_down):
    batch, seq, d = x_prompt.shape
    dbatch, dseq, _ = x_sample.shape
    depth, _, past, _ = cache_ckv.shape
    assert seq % GMLP_CHUNK == 0 and GMLP_CHUNK % dseq == 0 and (dbatch * dseq) % GMLP_CHUNK == 0

    tq_attn = 256
    tm_in = 1024
    tm_ffn = 512
    tm_s = dbatch * dseq

    tq_p, tk_p = (jnp.asarray(t) for t in _rope_tables(np.arange(seq)))
    tq_s, tk_s = (jnp.asarray(np.tile(t, (dbatch, 1))) for t in _rope_tables(past + np.arange(dseq)))
    bias_p = jnp.asarray(_chunk_bias(np.arange(tq_attn), np.arange(tq_attn)))
    bias_s = np.full((dseq, LANES), NEG_INF, np.float32)
    bias_s[:, :dseq] = _chunk_bias(past + np.arange(dseq), past + np.arange(dseq))
    bias_s = jnp.asarray(np.tile(bias_s, (N_HEADS, 1)))
    reps = GMLP_CHUNK // dseq

    yp = x_prompt.reshape(batch * seq, d)
    ys = x_sample.reshape(dbatch * dseq, d)
    ckv_p, kr_p, ckv_s, kr_s, gv_s = [], [], [], [], []
    big = _stacked_weights(g_mix, w_in, w_o, g_ffn, w_gate, w_up, w_down)
    kr_cache_t = jnp.swapaxes(cache_krope, 2, 3)
    for l in range(depth):
        lw = _layer_weights(l, g_q_lora, w_q_up, g_qn, g_qr, g_kv_lora, g_kr, w_kv_up, g_kn,
                            gm_ln_g, gm_ln_b, g_out_att, g_out_gm)
        ws_p = gm_w_s[l]
        bs_p = jnp.repeat(gm_b_s[l].T, GMLP_GROUP_DIM, axis=1)
        ws_s = jnp.tile(gm_w_s[l][:, :dseq, :dseq], (1, reps, reps))
        bs_s = jnp.tile(jnp.repeat(gm_b_s[l][:, :dseq].T, GMLP_GROUP_DIM, axis=1), (reps, 1))

        q, k, vt, c, r, ogm = _inproj(yp, big, lw, tq_p, tk_p, ws_p, bs_p, layer=l, tm=tm_in, table_blocks=seq // tm_in,
                                      ws_mask="causal", want_vn=False, v_transposed=True)
        g_t = jnp.broadcast_to(lw["g_out_att"].reshape(N_HEADS * V_HEAD, 1), (N_HEADS * V_HEAD, tq_attn))
        oatt = _attn_prompt(q, k, vt, bias_p.T, g_t, batch=batch, seq=seq, tq=tq_attn)
        yp = _ffn(yp, oatt, ogm, big, layer=l, tm=tm_ffn, fchunk=256)
        ckv_p.append(c.reshape(batch, seq, KV_LORA))
        kr_p.append(jnp.transpose(r.reshape(QK_ROPE, batch, seq), (1, 2, 0)))

        q, k, v, c, r, ogm, vn = _inproj(ys, big, lw, tq_s, tk_s, ws_s, bs_s, layer=l, tm=tm_s, table_blocks=1,
                                         ws_mask="blockdiag", want_vn=True, v_transposed=False)
        oatt = _attn_sample(q, c, r, cache_ckv, kr_cache_t, lw["w2"], lw["wk_t"], lw["w_kvv"], bias_s,
                            lw["g_out_att"], layer=l, batch=dbatch, seq=dseq)
        ys = _ffn(ys, oatt, ogm, big, layer=l, tm=tm_s, fchunk=256)
        ckv_s.append(c.reshape(dbatch, dseq, KV_LORA))
        kr_s.append(r.reshape(dbatch, dseq, QK_ROPE))
        gv_s.append(vn.reshape(dbatch, dseq, GMLP_WIDTH))

    return (yp.reshape(batch, seq, d), ys.reshape(dbatch, dseq, d), jnp.stack(ckv_p), jnp.stack(kr_p),
            jnp.stack(ckv_s), jnp.stack(kr_s), jnp.stack(gv_s))
```

```python
import functools

import numpy as np
import jax
import jax.numpy as jnp
from jax import lax
from jax.experimental import pallas as pl
from jax.experimental.pallas import tpu as pltpu

F32 = jnp.float32
BF16 = jnp.bfloat16

CHUNK = 64
N_HEADS = 8
QK_NOPE = 64
QK_ROPE = 32
QK_HEAD = QK_NOPE + QK_ROPE
V_HEAD = 64
Q_LORA = 256
KV_LORA = 128
ROPE_THETA = 10000.0
GMLP_GROUPS = 8
GMLP_GROUP_DIM = 64
GMLP_WIDTH = GMLP_GROUPS * GMLP_GROUP_DIM
GMLP_CHUNK = 128
EPS = 1e-6
NEG_INF = -1e30
LOG2_E = 1.4426950408889634

LANES = 128
BF16_SUBLANES = 16
V7X_VMEM_BYTES = 64 * 1024 * 1024
VMEM_LIMIT_BYTES = (V7X_VMEM_BYTES * 3) // 4

HALF = QK_ROPE // 2
V_ROWS = V_HEAD + BF16_SUBLANES
IN_COLS_PADDED = Q_LORA + KV_LORA + LANES + 2 * GMLP_WIDTH
COL_CKV = Q_LORA
COL_KR = Q_LORA + KV_LORA
COL_U = COL_KR + LANES
COL_V = COL_U + GMLP_WIDTH

_NT = (((1,), (1,)), ((), ()))


def _dot(a, b):
    return jnp.dot(a, b, preferred_element_type=F32)


def _dot_nt(a, b):
    return lax.dot_general(a, b, _NT, preferred_element_type=F32)


def _rsqrt_sumsq(x, n):
    return lax.rsqrt(jnp.sum(x * x, axis=-1, keepdims=True) + n * EPS)


def _half_sums(x, lo):
    s_lo = jnp.sum(jnp.where(lo, x, 0.0), axis=-1, keepdims=True)
    s_hi = jnp.sum(jnp.where(lo, 0.0, x), axis=-1, keepdims=True)
    return jnp.where(lo, s_lo, s_hi)


def _half_rms_scale(x, lo):
    return lax.rsqrt(_half_sums(x * x, lo) * (1.0 / 64) + EPS)


def _half_rsqrt_sumsq(x, lo):
    return lax.rsqrt(_half_sums(x * x, lo) + 64 * EPS)


def _gelu_tanh(t):
    c = 0.7978845608028654
    h = 0.5 * t
    return h + h * jnp.tanh(t * (c + (c * 0.044715) * (t * t)))


INPROJ_SUB_ROWS = 256


def _inproj_kernel(x_ref, win_ref, wq_ref, gq_ref, tq_ref, gkv_ref, gkr_ref, tk_ref,
                   wkk_ref, wkv_ref, lng_ref, lnb_ref, ws_ref, bs_ref, ggm_ref, seg_ref, epsq_ref,
                   q_out, k_out, v_out, ckv_out, kr_out, ogm_out, *vn_out, ws_mask, v_transposed):
    tm = x_ref.shape[0]
    sub = min(tm, INPROJ_SUB_ROWS)
    lane = lax.broadcasted_iota(jnp.int32, (sub, LANES), 1)
    lo = lane < 64
    lo_c = lax.broadcasted_iota(jnp.int32, (GMLP_CHUNK, LANES), 1) < 64
    row = lax.broadcasted_iota(jnp.int32, (GMLP_CHUNK, GMLP_CHUNK), 0) // CHUNK
    col = lax.broadcasted_iota(jnp.int32, (GMLP_CHUNK, GMLP_CHUNK), 1) // CHUNK
    keep = (row >= col) if ws_mask == "causal" else (row == col)

    def project(r0):
        x = x_ref[r0:r0 + sub, :]
        h = (x * _rsqrt_sumsq(x, x.shape[1])).astype(BF16)
        return _dot(h, win_ref[...])

    def finish(z, r0):
        rows = slice(r0, r0 + sub)
        ql = z[:, 0:Q_LORA]
        qn = (ql * _rsqrt_sumsq(ql, Q_LORA)).astype(BF16)
        qz = _dot(qn, wq_ref[...])
        tq = tq_ref[rows, :] * gq_ref[...]
        for hd in range(N_HEADS):
            cols = slice(hd * LANES, (hd + 1) * LANES)
            xh = qz[:, cols]
            ss = _dot((xh * xh).astype(BF16), seg_ref[...])
            q_out[rows, cols] = (xh * lax.rsqrt(ss + epsq_ref[...]) * tq).astype(BF16)

        y0 = z[:, COL_KR:COL_KR + LANES]
        s_k = jnp.sum(jnp.where(lane < QK_ROPE, y0 * y0, 0.0), axis=-1, keepdims=True)
        y = y0 * lax.rsqrt(s_k + QK_ROPE * EPS) * (gkr_ref[...] * tk_ref[rows, :])
        r64 = pltpu.roll(y, 64, 1)
        r32 = pltpu.roll(y, 32, 1)
        r96 = pltpu.roll(y, 96, 1)
        if v_transposed:
            kr_out[:, rows] = (y + r96).T[0:QK_ROPE, :]
        else:
            kr_out[rows, :] = (y + r96)[:, 0:QK_ROPE]
        kr_hi = jnp.where(lo, 0.0, r64 + r32 + r96)

        c0 = z[:, COL_CKV:COL_CKV + KV_LORA]
        cn = c0 * _rsqrt_sumsq(c0, KV_LORA) * gkv_ref[...]
        ckv_out[rows, :] = cn
        cb = cn.astype(BF16)
        kz = _dot(cb, wkk_ref[...])
        for hd in range(N_HEADS):
            cols = slice(hd * LANES, (hd + 1) * LANES)
            xh = kz[:, cols]
            kn = xh * _rsqrt_sumsq(xh, QK_NOPE)
            k_out[rows, cols] = jnp.where(lo, kn, kr_hi).astype(BF16)
        if v_transposed:
            vt = _dot_nt(wkv_ref[...], cb).astype(BF16)
            tail = jnp.where(lax.broadcasted_iota(jnp.int32, (BF16_SUBLANES, sub), 0) == 0, 1.0, 0.0).astype(BF16)
            for hd in range(N_HEADS):
                v_out[hd * V_ROWS:hd * V_ROWS + V_HEAD, rows] = vt[hd * V_HEAD:(hd + 1) * V_HEAD]
                v_out[hd * V_ROWS + V_HEAD:(hd + 1) * V_ROWS, rows] = tail
        else:
            v_out[rows, :] = _dot(cb, wkv_ref[...]).astype(BF16)

        for p in range(GMLP_GROUPS // 2):
            sl = slice(p * LANES, (p + 1) * LANES)
            w_pair = jnp.concatenate(
                [jnp.where(keep, ws_ref[2 * p], 0.0), jnp.where(keep, ws_ref[2 * p + 1], 0.0)], axis=0).astype(BF16)
            ug = _gelu_tanh(z[:, COL_U + p * LANES:COL_U + (p + 1) * LANES])
            vg = _gelu_tanh(z[:, COL_V + p * LANES:COL_V + (p + 1) * LANES])
            xc = vg - _half_sums(vg, lo) * (1.0 / GMLP_GROUP_DIM)
            vn = xc * _half_rsqrt_sumsq(xc, lo) * lng_ref[:, sl] + lnb_ref[:, sl]
            if vn_out:
                vn_out[0][rows, sl] = vn
            vnb = vn.astype(BF16)
            for c in range(sub // GMLP_CHUNK):
                crows = slice(c * GMLP_CHUNK, (c + 1) * GMLP_CHUNK)
                res = _dot(w_pair, vnb[crows])
                mixed = jnp.where(lo_c, res[0:GMLP_CHUNK], res[GMLP_CHUNK:]) + bs_ref[:, sl]
                og = ug[crows] * mixed
                ogm_out[r0 + c * GMLP_CHUNK:r0 + (c + 1) * GMLP_CHUNK, sl] = (
                    og * _half_rsqrt_sumsq(og, lo_c) * ggm_ref[:, sl]).astype(BF16)

    starts = list(range(0, tm, sub))
    z_next = project(starts[0])
    for i, r0 in enumerate(starts):
        z_cur = z_next
        if i + 1 < len(starts):
            z_next = project(starts[i + 1])
        finish(z_cur, r0)


def _const_spec(shape):
    return pl.BlockSpec(shape, lambda *_: (0,) * len(shape), pipeline_mode=pl.Buffered(1))


def _layer_spec(stacked, layer):
    return pl.BlockSpec((None,) + stacked.shape[1:], lambda *_: (layer, 0, 0), pipeline_mode=pl.Buffered(1))


def _inproj(x, big, lw, tq, tk, ws, bs, *, layer, tm, table_blocks, ws_mask, want_vn, v_transposed):
    n, d = x.shape
    w_v = lw["w_kvv_t"] if v_transposed else lw["w_kvv"]
    grid = (n // tm,)
    tbl = lambda i: (i % table_blocks, 0)
    row = lambda i: (i, 0)
    in_specs = [
        pl.BlockSpec((tm, d), row),
        _layer_spec(big["w_in"], layer),
        _const_spec((Q_LORA, N_HEADS * LANES)),
        _const_spec((1, LANES)),
        pl.BlockSpec((tm, LANES), tbl),
        _const_spec((1, KV_LORA)),
        _const_spec((1, LANES)),
        pl.BlockSpec((tm, LANES), tbl),
        _const_spec((KV_LORA, N_HEADS * LANES)),
        _const_spec(w_v.shape),
        _const_spec((1, GMLP_WIDTH)),
        _const_spec((1, GMLP_WIDTH)),
        _const_spec((GMLP_GROUPS, GMLP_CHUNK, GMLP_CHUNK)),
        _const_spec((GMLP_CHUNK, GMLP_WIDTH)),
        _const_spec((1, GMLP_WIDTH)),
        _const_spec((LANES, LANES)),
        _const_spec((1, LANES)),
    ]
    out_shape = [
        jax.ShapeDtypeStruct((n, N_HEADS * LANES), BF16),
        jax.ShapeDtypeStruct((n, N_HEADS * LANES), BF16),
        (jax.ShapeDtypeStruct((N_HEADS * V_ROWS, n), BF16) if v_transposed
         else jax.ShapeDtypeStruct((n, N_HEADS * V_HEAD), BF16)),
        jax.ShapeDtypeStruct((n, KV_LORA), F32),
        (jax.ShapeDtypeStruct((QK_ROPE, n), F32) if v_transposed
         else jax.ShapeDtypeStruct((n, QK_ROPE), F32)),
        jax.ShapeDtypeStruct((n, GMLP_WIDTH), BF16),
    ]
    out_specs = [
        pl.BlockSpec((tm, N_HEADS * LANES), row),
        pl.BlockSpec((tm, N_HEADS * LANES), row),
        (pl.BlockSpec((N_HEADS * V_ROWS, tm), lambda i: (0, i)) if v_transposed
         else pl.BlockSpec((tm, N_HEADS * V_HEAD), row)),
        pl.BlockSpec((tm, KV_LORA), row),
        (pl.BlockSpec((QK_ROPE, tm), lambda i: (0, i)) if v_transposed else pl.BlockSpec((tm, QK_ROPE), row)),
        pl.BlockSpec((tm, GMLP_WIDTH), row),
    ]
    if want_vn:
        out_shape.append(jax.ShapeDtypeStruct((n, GMLP_WIDTH), F32))
        out_specs.append(pl.BlockSpec((tm, GMLP_WIDTH), row))
    return pl.pallas_call(
        functools.partial(_inproj_kernel, ws_mask=ws_mask, v_transposed=v_transposed),
        grid=grid, in_specs=in_specs, out_specs=out_specs, out_shape=out_shape,
        compiler_params=pltpu.CompilerParams(dimension_semantics=("arbitrary",),
                                             vmem_limit_bytes=VMEM_LIMIT_BYTES),
        name="inproj_vn" if want_vn else "inproj",
    )(x, big["w_in"], lw["w_q"], lw["gq"], tq, lw["g_kv_lora"], lw["gkr"], tk,
      lw["w_kvk"], w_v, lw["ln_g"], lw["ln_b"], ws, bs, lw["g_out_gm"],
      jnp.asarray(_Q_SEGMENTS, BF16), jnp.asarray(_Q_SEGMENT_EPS))


ATTN_STREAM_LAG = 18


def _attn_prompt_kernel(q_ref, k_ref, vt_ref, bias_ref, g_ref, o_ref, s_scr, *, tq):
    s_len = q_ref.shape[0]
    bodies = [(qi, hh) for qi in reversed(range(s_len // tq)) for hh in range(2)]
    bases = np.cumsum([0] + [qi + 1 for qi, _ in bodies])
    state = [{} for _ in bodies]

    def a_steps(bi):
        qi, hh = bodies[bi]
        cols = slice(hh * LANES, (hh + 1) * LANES)
        rows = slice(qi * tq, (qi + 1) * tq)
        st = state[bi]

        def chunk(j):
            s = _dot_nt(k_ref[j * tq:(j + 1) * tq, cols], q_ref[rows, cols])
            if j == qi:
                s = s + bias_ref[...]
            s_scr[bases[bi] + j] = s
            cm = jnp.max(s.reshape(tq // 8, 8, tq), axis=0)
            st["m_run"] = cm if j == 0 else jnp.maximum(st["m_run"], cm)

        def finish():
            st["m"] = jnp.max(st["m_run"], axis=0, keepdims=True)

        return [functools.partial(chunk, j) for j in range(qi + 1)] + [finish]

    def b_steps(bi):
        qi, hh = bodies[bi]
        st = state[bi]

        def chunk(j):
            p = jnp.exp2(s_scr[bases[bi] + j] - st["m"]).astype(BF16)
            d = _dot(vt_ref[hh * V_ROWS:(hh + 1) * V_ROWS, j * tq:(j + 1) * tq], p)
            st["acc"] = d if j == 0 else st["acc"] + d

        def finish():
            acc = st["acc"]
            o = acc[0:V_HEAD] * (1.0 / acc[V_HEAD:V_HEAD + 1])
            r = lax.rsqrt(jnp.sum(o * o, axis=0, keepdims=True) * (1.0 / V_HEAD) + EPS)
            st["out"] = o * r * g_ref[hh * V_HEAD:(hh + 1) * V_HEAD, :]
            if hh == 1:
                pair_t = jnp.concatenate([state[bi - 1]["out"], st["out"]], axis=0)
                o_ref[qi * tq:(qi + 1) * tq, :] = pair_t.T.astype(BF16)

        return [functools.partial(chunk, j) for j in range(qi + 1)] + [finish]

    a_all = [f for bi in range(len(bodies)) for f in a_steps(bi)]
    b_all = [f for bi in range(len(bodies)) for f in b_steps(bi)]
    for i in range(len(a_all) + ATTN_STREAM_LAG):
        if i < len(a_all):
            a_all[i]()
        if i >= ATTN_STREAM_LAG:
            b_all[i - ATTN_STREAM_LAG]()


def _attn_prompt(q, k, vt, bias_t, g_t, *, batch, seq, tq):
    n = q.shape[0]
    pairs = N_HEADS // 2
    nq = seq // tq
    assert ATTN_STREAM_LAG > nq + 1
    return pl.pallas_call(
        functools.partial(_attn_prompt_kernel, tq=tq),
        grid=(batch, pairs),
        in_specs=[
            pl.BlockSpec((seq, 2 * LANES), lambda b, p: (b, p)),
            pl.BlockSpec((seq, 2 * LANES), lambda b, p: (b, p)),
            pl.BlockSpec((2 * V_ROWS, seq), lambda b, p: (p, b)),
            pl.BlockSpec((tq, tq), lambda b, p: (0, 0)),
            pl.BlockSpec((2 * V_HEAD, tq), lambda b, p: (p, 0)),
        ],
        out_specs=pl.BlockSpec((seq, 2 * V_HEAD), lambda b, p: (b, p)),
        out_shape=jax.ShapeDtypeStruct((n, N_HEADS * V_HEAD), BF16),
        scratch_shapes=[pltpu.VMEM((nq * (nq + 1), tq, tq), F32)],
        compiler_params=pltpu.CompilerParams(dimension_semantics=("arbitrary", "arbitrary"),
                                             vmem_limit_bytes=VMEM_LIMIT_BYTES),
        name="attn_prompt",
    )(q, k, vt, bias_t, g_t)


SAMPLE_KEY_BLOCK = 1024


def _attn_sample_kernel(q_ref, cnew_ref, krnew_ref, ckv_ref, krt_ref, w2_ref, wkt_ref, wv_ref, bias_ref, gout_ref,
                        o_ref, kr_scr, *, kblk):
    past = ckv_ref.shape[0]
    sq = q_ref.shape[0]
    q2 = jnp.concatenate([_dot(q_ref[:, h * LANES:(h + 1) * LANES], w2_ref[h]) for h in range(N_HEADS)],
                         axis=0).astype(BF16)
    qt = q2[:, 0:KV_LORA]
    qr = q2[:, KV_LORA:]
    kr_t = jnp.concatenate([krt_ref[...].astype(BF16), jnp.zeros((LANES - QK_ROPE, past), BF16)], axis=0)
    kr_scr[...] = jnp.zeros(kr_scr.shape, F32)
    kr_scr[0:sq, 0:QK_ROPE] = krnew_ref[...]
    ones_blk = jnp.ones((kblk, LANES), BF16)

    def scores(cb, s2):
        kzt = _dot_nt(wkt_ref[...], cb)
        s1 = _dot_nt(qt, cb)
        out = []
        for h in range(N_HEADS):
            kh = kzt[h * QK_NOPE:(h + 1) * QK_NOPE]
            r = lax.rsqrt(jnp.sum(kh * kh, axis=0, keepdims=True) + QK_NOPE * EPS)
            out.append(s1[h * sq:(h + 1) * sq] * r + s2[h * sq:(h + 1) * sq])
        return jnp.concatenate(out, axis=0)

    cn = jnp.concatenate([cnew_ref[...], jnp.zeros((LANES - sq, KV_LORA), F32)], axis=0).astype(BF16)
    blocks = [(ckv_ref[j * kblk:(j + 1) * kblk, :], kr_t[:, j * kblk:(j + 1) * kblk]) for j in range(past // kblk)]

    def block_scores(j):
        if j < len(blocks):
            cb = blocks[j][0].astype(BF16)
            return scores(cb, _dot(qr, blocks[j][1])), jnp.concatenate([cb, ones_blk], axis=1)
        s_new = scores(cn, _dot_nt(qr, kr_scr[...].astype(BF16))) + bias_ref[...]
        return s_new, jnp.concatenate([cn, ones_blk[0:LANES]], axis=1)

    m = acc = None
    nxt = block_scores(0)
    for j in range(len(blocks) + 1):
        s, vals = nxt
        if j < len(blocks):
            nxt = block_scores(j + 1)
        m_blk = jnp.max(s, axis=-1, keepdims=True)
        m_new = m_blk if m is None else jnp.maximum(m, m_blk)
        d = _dot(jnp.exp2(s - m_new).astype(BF16), vals)
        acc = d if acc is None else acc * jnp.exp2(m - m_new) + d
        m = m_new
    o_lat = (acc[:, 0:KV_LORA] * (1.0 / acc[:, KV_LORA:])).astype(BF16)
    o_full = _dot(o_lat, wv_ref[...])
    head = lax.broadcasted_iota(jnp.int32, (sq, N_HEADS * V_HEAD), 1) >> 6
    o = jnp.zeros((sq, N_HEADS * V_HEAD), F32)
    for h in range(N_HEADS):
        o = jnp.where(head == h, o_full[h * sq:(h + 1) * sq], o)
    lo = lax.broadcasted_iota(jnp.int32, (sq, LANES), 1) < 64
    for c in range(0, N_HEADS * V_HEAD, LANES):
        oc = o[:, c:c + LANES]
        o_ref[:, c:c + LANES] = (oc * _half_rms_scale(oc, lo) * gout_ref[:, c:c + LANES]).astype(BF16)


def _attn_sample(q, cnew, krnew, cache_ckv, kr_cache_t, w2, wkt, w_kvv, bias, gout, *, layer, batch, seq):
    n = q.shape[0]
    past = cache_ckv.shape[2]
    assert V_HEAD == 64 and past % SAMPLE_KEY_BLOCK == 0 and seq <= LANES
    return pl.pallas_call(
        functools.partial(_attn_sample_kernel, kblk=SAMPLE_KEY_BLOCK),
        grid=(batch,),
        in_specs=[
            pl.BlockSpec((seq, N_HEADS * LANES), lambda b: (b, 0)),
            pl.BlockSpec((seq, KV_LORA), lambda b: (b, 0)),
            pl.BlockSpec((seq, QK_ROPE), lambda b: (b, 0)),
            pl.BlockSpec((None, None, past, KV_LORA), lambda b: (layer, b, 0, 0)),
            pl.BlockSpec((None, None, QK_ROPE, past), lambda b: (layer, b, 0, 0)),
            pl.BlockSpec((N_HEADS, LANES, 2 * LANES), lambda b: (0, 0, 0)),
            pl.BlockSpec((N_HEADS * QK_NOPE, KV_LORA), lambda b: (0, 0)),
            pl.BlockSpec((KV_LORA, N_HEADS * V_HEAD), lambda b: (0, 0)),
            pl.BlockSpec((N_HEADS * seq, LANES), lambda b: (0, 0)),
            pl.BlockSpec((1, N_HEADS * V_HEAD), lambda b: (0, 0)),
        ],
        out_specs=pl.BlockSpec((seq, N_HEADS * V_HEAD), lambda b: (b, 0)),
        out_shape=jax.ShapeDtypeStruct((n, N_HEADS * V_HEAD), BF16),
        scratch_shapes=[pltpu.VMEM((LANES, LANES), F32)],
        compiler_params=pltpu.CompilerParams(dimension_semantics=("arbitrary",),
                                             vmem_limit_bytes=VMEM_LIMIT_BYTES),
        name="attn_sample",
    )(q, cnew, krnew, cache_ckv, kr_cache_t, w2, wkt, w_kvv, bias, gout)


def _ffn_kernel(x_ref, oa_ref, og_ref, wo_ref, wg_ref, wu_ref, wd_ref, y_ref, *, fchunk):
    aw = oa_ref.shape[1]
    x1 = x_ref[...] + _dot(oa_ref[...], wo_ref[0:aw, :]) + _dot(og_ref[...], wo_ref[aw:, :])
    h2 = (x1 * _rsqrt_sumsq(x1, x1.shape[1])).astype(BF16)
    acc = None
    for c in range(0, wg_ref.shape[1], fchunk):
        g = _dot(h2, wg_ref[:, c:c + fchunk])
        u = _dot(h2, wu_ref[:, c:c + fchunk])
        a = (g * jax.nn.sigmoid(g) * u).astype(BF16)
        d = _dot(a, wd_ref[c:c + fchunk, :])
        acc = d if acc is None else acc + d
    y_ref[...] = x1 + acc


def _ffn(x, oa, og, big, *, layer, tm, fchunk):
    n, d = x.shape
    row = lambda i: (i, 0)
    return pl.pallas_call(
        functools.partial(_ffn_kernel, fchunk=fchunk),
        grid=(n // tm,),
        in_specs=[
            pl.BlockSpec((tm, d), row),
            pl.BlockSpec((tm, oa.shape[1]), row),
            pl.BlockSpec((tm, og.shape[1]), row),
            _layer_spec(big["w_o"], layer),
            _layer_spec(big["w_gate"], layer),
            _layer_spec(big["w_up"], layer),
            _layer_spec(big["w_down"], layer),
        ],
        out_specs=pl.BlockSpec((tm, d), row),
        out_shape=jax.ShapeDtypeStruct((n, d), F32),
        compiler_params=pltpu.CompilerParams(dimension_semantics=("arbitrary",),
                                             vmem_limit_bytes=VMEM_LIMIT_BYTES),
        name="outproj_ffn",
    )(x, oa, og, big["w_o"], big["w_gate"], big["w_up"], big["w_down"])


def _rope_tables(pos):
    inv = ROPE_THETA ** (-np.arange(HALF, dtype=np.float64) / HALF)
    ang = np.asarray(pos, np.float64)[:, None] * inv[None, :]
    cos, sin = np.cos(ang), np.sin(ang)
    n = ang.shape[0]
    tq = np.concatenate([np.ones((n, QK_NOPE)), cos, cos, -sin, sin], axis=1)
    tk = np.concatenate([cos, cos, -sin, sin, np.zeros((n, LANES - 2 * QK_ROPE))], axis=1)
    return tq.astype(np.float32), tk.astype(np.float32)


def _rope_fold_matrix():
    f = np.zeros((LANES, LANES), np.float32)
    for i in range(QK_ROPE):
        f[QK_NOPE + i, i] = 1.0
        f[QK_NOPE + QK_ROPE + i, i] = 1.0
    return f


_ROPE_FOLD = _rope_fold_matrix()


def _q_segment_matrix():
    m = np.zeros((LANES, LANES), np.float32)
    m[:QK_NOPE, :QK_NOPE] = 1.0
    m[QK_NOPE:QK_NOPE + QK_ROPE, QK_NOPE:] = 1.0
    return m


_Q_SEGMENTS = _q_segment_matrix()
_Q_SEGMENT_COUNT = np.concatenate([np.full(QK_NOPE, QK_NOPE), np.full(LANES - QK_NOPE, QK_ROPE)]).astype(np.float32)
_Q_SEGMENT_EPS = (_Q_SEGMENT_COUNT * EPS)[None, :]


def _swap_halves(g):
    return jnp.concatenate([g[..., HALF:], g[..., :HALF]], axis=-1)


def _stacked_weights(g_mix, w_in, w_o, g_ffn, w_gate, w_up, w_down):
    depth, d_model, _ = w_in.shape
    w = (w_in * (g_mix * d_model ** 0.5)[:, :, None]).astype(BF16)
    kr0 = Q_LORA + KV_LORA
    w_in_p = jnp.concatenate(
        [w[..., :kr0 + QK_ROPE], w[..., kr0 + HALF:kr0 + QK_ROPE], w[..., kr0:kr0 + HALF],
         jnp.zeros((depth, d_model, LANES - 2 * QK_ROPE), BF16), w[..., kr0 + QK_ROPE:]], axis=-1)
    g_ffn_col = (g_ffn * d_model ** 0.5)[:, :, None]
    return {"w_in": w_in_p, "w_o": w_o.astype(BF16), "w_gate": (w_gate * g_ffn_col).astype(BF16),
            "w_up": (w_up * g_ffn_col).astype(BF16), "w_down": w_down.astype(BF16)}


def _layer_weights(l, g_q_lora, w_q_up, g_qn, g_qr, g_kv_lora, g_kr, w_kv_up, g_kn,
                   gm_ln_g, gm_ln_b, g_out_att, g_out_gm):
    wq = w_q_up[l].reshape(Q_LORA, N_HEADS, QK_HEAD)
    r1, r2 = wq[..., QK_NOPE:QK_NOPE + HALF], wq[..., QK_NOPE + HALF:]
    w_q = jnp.concatenate([wq[..., :QK_NOPE], r1, r2, r2, r1], axis=-1).reshape(Q_LORA, N_HEADS * LANES)
    w_q = w_q * (g_q_lora[l] * Q_LORA ** 0.5)[:, None]
    gq = (jnp.concatenate([g_qn[l] * g_kn[l] * QK_NOPE ** 0.5, g_qr[l], _swap_halves(g_qr[l])])
          * np.sqrt(_Q_SEGMENT_COUNT) * (QK_HEAD ** -0.5 * LOG2_E))
    gkr = jnp.concatenate([g_kr[l], _swap_halves(g_kr[l]), jnp.zeros((LANES - 2 * QK_ROPE,), F32)]) * QK_ROPE ** 0.5
    wkv = w_kv_up[l].reshape(KV_LORA, N_HEADS, QK_NOPE + V_HEAD)
    w_kvk = jnp.concatenate([wkv[..., :QK_NOPE], jnp.zeros((KV_LORA, N_HEADS, LANES - QK_NOPE), F32)],
                            axis=-1).reshape(KV_LORA, N_HEADS * LANES)
    w_kvv = wkv[..., QK_NOPE:].reshape(KV_LORA, N_HEADS * V_HEAD)
    w_kvv_t = w_kvv.T
    wk_t = jnp.transpose(wkv[..., :QK_NOPE], (1, 2, 0))
    w_abs = jnp.concatenate([wk_t, jnp.zeros((N_HEADS, LANES - QK_NOPE, KV_LORA), F32)], axis=1)
    w2 = jnp.concatenate([w_abs, jnp.broadcast_to(jnp.asarray(_ROPE_FOLD), (N_HEADS, LANES, LANES))], axis=2)
    return {
        "w_q": w_q.astype(BF16), "gq": gq[None, :],
        "g_kv_lora": g_kv_lora[l][None, :] * KV_LORA ** 0.5, "gkr": gkr[None, :],
        "w_kvk": w_kvk.astype(BF16), "w_kvv": w_kvv.astype(BF16),
        "w_kvv_t": w_kvv_t.astype(BF16),
        "w2": w2.astype(BF16), "wk_t": wk_t.reshape(N_HEADS * QK_NOPE, KV_LORA).astype(BF16),
        "ln_g": gm_ln_g[l].reshape(1, GMLP_WIDTH) * GMLP_GROUP_DIM ** 0.5, "ln_b": gm_ln_b[l].reshape(1, GMLP_WIDTH),
        "g_out_att": g_out_att[l].reshape(1, N_HEADS * V_HEAD),
        "g_out_gm": g_out_gm[l].reshape(1, GMLP_WIDTH) * GMLP_GROUP_DIM ** 0.5,
    }


def _chunk_bias(qpos, kpos):
    vis = (kpos[None, :] // CHUNK) <= (qpos[:, None] // CHUNK)
    return np.where(vis, 0.0, NEG_INF).astype(np.float32)


def kernel(x_prompt, x_sample, cache_ckv, cache_krope, g_mix, w_in, g_q_lora, w_q_up, g_qn, g_qr, g_kv_lora, g_kr,
           w_kv_up, g_kn, gm_ln_g, gm_ln_b, gm_w_s, gm_b_s, g_out_att, g_out_gm, w_o, g_ffn, w_gate, w_up, w_down):
    batch, seq, d = x_prompt.shape
    dbatch, dseq, _ = x_sample.shape
    depth, _, past, _ = cache_ckv.shape
    assert seq % GMLP_CHUNK == 0 and GMLP_CHUNK % dseq == 0 and (dbatch * dseq) % GMLP_CHUNK == 0

    tq_attn = 256
    tm_in = 1024
    tm_ffn = 512
    tm_s = dbatch * dseq

    tq_p, tk_p = (jnp.asarray(t) for t in _rope_tables(np.arange(seq)))
    tq_s, tk_s = (jnp.asarray(np.tile(t, (dbatch, 1))) for t in _rope_tables(past + np.arange(dseq)))
    bias_p = jnp.asarray(_chunk_bias(np.arange(tq_attn), np.arange(tq_attn)))
    bias_s = np.full((dseq, LANES), NEG_INF, np.float32)
    bias_s[:, :dseq] = _chunk_bias(past + np.arange(dseq), past + np.arange(dseq))
    bias_s = jnp.asarray(np.tile(bias_s, (N_HEADS, 1)))
    reps = GMLP_CHUNK // dseq

    yp = x_prompt.reshape(batch * seq, d)
    ys = x_sample.reshape(dbatch * dseq, d)
    ckv_p, kr_p, ckv_s, kr_s, gv_s = [], [], [], [], []
    big = _stacked_weights(g_mix, w_in, w_o, g_ffn, w_gate, w_up, w_down)
    kr_cache_t = jnp.swapaxes(cache_krope, 2, 3)
    for l in range(depth):
        lw = _layer_weights(l, g_q_lora, w_q_up, g_qn, g_qr, g_kv_lora, g_kr, w_kv_up, g_kn,
                            gm_ln_g, gm_ln_b, g_out_att, g_out_gm)
        ws_p = gm_w_s[l]
        bs_p = jnp.repeat(gm_b_s[l].T, GMLP_GROUP_DIM, axis=1)
        ws_s = jnp.tile(gm_w_s[l][:, :dseq, :dseq], (1, reps, reps))
        bs_s = jnp.tile(jnp.repeat(gm_b_s[l][:, :dseq].T, GMLP_GROUP_DIM, axis=1), (reps, 1))

        q, k, vt, c, r, ogm = _inproj(yp, big, lw, tq_p, tk_p, ws_p, bs_p, layer=l, tm=tm_in, table_blocks=seq // tm_in,
                                      ws_mask="causal", want_vn=False, v_transposed=True)
        g_t = jnp.broadcast_to(lw["g_out_att"].reshape(N_HEADS * V_HEAD, 1), (N_HEADS * V_HEAD, tq_attn))
        oatt = _attn_prompt(q, k, vt, bias_p.T, g_t, batch=batch, seq=seq, tq=tq_attn)
        yp = _ffn(yp, oatt, ogm, big, layer=l, tm=tm_ffn, fchunk=256)
        ckv_p.append(c.reshape(batch, seq, KV_LORA))
        kr_p.append(jnp.transpose(r.reshape(QK_ROPE, batch, seq), (1, 2, 0)))

        q, k, v, c, r, ogm, vn = _inproj(ys, big, lw, tq_s, tk_s, ws_s, bs_s, layer=l, tm=tm_s, table_blocks=1,
                                         ws_mask="blockdiag", want_vn=True, v_transposed=False)
        oatt = _attn_sample(q, c, r, cache_ckv, kr_cache_t, lw["w2"], lw["wk_t"], lw["w_kvv"], bias_s,
                            lw["g_out_att"], layer=l, batch=dbatch, seq=dseq)
        ys = _ffn(ys, oatt, ogm, big, layer=l, tm=tm_s, fchunk=256)
        ckv_s.append(c.reshape(dbatch, dseq, KV_LORA))
        kr_s.append(r.reshape(dbatch, dseq, QK_ROPE))
        gv_s.append(vn.reshape(dbatch, dseq, GMLP_WIDTH))

    return (yp.reshape(batch, seq, d), ys.reshape(dbatch, dseq, d), jnp.stack(ckv_p), jnp.stack(kr_p),
            jnp.stack(ckv_s), jnp.stack(kr_s), jnp.stack(gv_s))
```

```python
import functools

import numpy as np
import jax
import jax.numpy as jnp
from jax import lax
from jax.experimental import pallas as pl
from jax.experimental.pallas import tpu as pltpu

F32 = jnp.float32
BF16 = jnp.bfloat16

CHUNK = 64
N_HEADS = 8
QK_NOPE = 64
QK_ROPE = 32
QK_HEAD = QK_NOPE + QK_ROPE
V_HEAD = 64
Q_LORA = 256
KV_LORA = 128
ROPE_THETA = 10000.0
GMLP_GROUPS = 8
GMLP_GROUP_DIM = 64
GMLP_WIDTH = GMLP_GROUPS * GMLP_GROUP_DIM
GMLP_CHUNK = 128
EPS = 1e-6
NEG_INF = -1e30
LOG2_E = 1.4426950408889634

LANES = 128
BF16_SUBLANES = 16
V7X_VMEM_BYTES = 64 * 1024 * 1024
VMEM_LIMIT_BYTES = (V7X_VMEM_BYTES * 3) // 4

HALF = QK_ROPE // 2
V_ROWS = V_HEAD + BF16_SUBLANES
IN_COLS_PADDED = Q_LORA + KV_LORA + LANES + 2 * GMLP_WIDTH
COL_CKV = Q_LORA
COL_KR = Q_LORA + KV_LORA
COL_U = COL_KR + LANES
COL_V = COL_U + GMLP_WIDTH

_NT = (((1,), (1,)), ((), ()))


def _dot(a, b):
    return jnp.dot(a, b, preferred_element_type=F32)


def _dot_nt(a, b):
    return lax.dot_general(a, b, _NT, preferred_element_type=F32)


def _rsqrt_sumsq(x, n):
    return lax.rsqrt(jnp.sum(x * x, axis=-1, keepdims=True) + n * EPS)


def _half_sums(x, lo):
    s_lo = jnp.sum(jnp.where(lo, x, 0.0), axis=-1, keepdims=True)
    s_hi = jnp.sum(jnp.where(lo, 0.0, x), axis=-1, keepdims=True)
    return jnp.where(lo, s_lo, s_hi)


def _half_rms_scale(x, lo):
    return lax.rsqrt(_half_sums(x * x, lo) * (1.0 / 64) + EPS)


def _half_rsqrt_sumsq(x, lo):
    return lax.rsqrt(_half_sums(x * x, lo) + 64 * EPS)


def _gelu_tanh(t):
    c = 0.7978845608028654
    h = 0.5 * t
    return h + h * jnp.tanh(t * (c + (c * 0.044715) * (t * t)))


INPROJ_SUB_ROWS = 256


def _inproj_kernel(x_ref, win_ref, wq_ref, gq_ref, tq_ref, gkv_ref, gkr_ref, tk_ref,
                   wkk_ref, wkv_ref, lng_ref, lnb_ref, ws_ref, bs_ref, ggm_ref, seg_ref, epsq_ref,
                   q_out, k_out, v_out, ckv_out, kr_out, ogm_out, *vn_out, ws_mask, v_transposed):
    tm = x_ref.shape[0]
    sub = min(tm, INPROJ_SUB_ROWS)
    lane = lax.broadcasted_iota(jnp.int32, (sub, LANES), 1)
    lo = lane < 64
    lo_c = lax.broadcasted_iota(jnp.int32, (GMLP_CHUNK, LANES), 1) < 64
    row = lax.broadcasted_iota(jnp.int32, (GMLP_CHUNK, GMLP_CHUNK), 0) // CHUNK
    col = lax.broadcasted_iota(jnp.int32, (GMLP_CHUNK, GMLP_CHUNK), 1) // CHUNK
    keep = (row >= col) if ws_mask == "causal" else (row == col)

    def project(r0):
        x = x_ref[r0:r0 + sub, :]
        h = (x * _rsqrt_sumsq(x, x.shape[1])).astype(BF16)
        return _dot_nt(h, win_ref[...])

    def finish(z, r0):
        rows = slice(r0, r0 + sub)
        ql = z[:, 0:Q_LORA]
        qn = (ql * _rsqrt_sumsq(ql, Q_LORA)).astype(BF16)
        qz = _dot(qn, wq_ref[...])
        tq = tq_ref[rows, :] * gq_ref[...]
        for hd in range(N_HEADS):
            cols = slice(hd * LANES, (hd + 1) * LANES)
            xh = qz[:, cols]
            ss = _dot((xh * xh).astype(BF16), seg_ref[...])
            q_out[rows, cols] = (xh * lax.rsqrt(ss + epsq_ref[...]) * tq).astype(BF16)

        y0 = z[:, COL_KR:COL_KR + LANES]
        s_k = jnp.sum(jnp.where(lane < QK_ROPE, y0 * y0, 0.0), axis=-1, keepdims=True)
        y = y0 * lax.rsqrt(s_k + QK_ROPE * EPS) * (gkr_ref[...] * tk_ref[rows, :])
        r64 = pltpu.roll(y, 64, 1)
        r32 = pltpu.roll(y, 32, 1)
        r96 = pltpu.roll(y, 96, 1)
        if v_transposed:
            kr_out[:, rows] = (y + r96).T[0:QK_ROPE, :]
        else:
            kr_out[rows, :] = (y + r96)[:, 0:QK_ROPE]
        kr_hi = jnp.where(lo, 0.0, r64 + r32 + r96)

        c0 = z[:, COL_CKV:COL_CKV + KV_LORA]
        cn = c0 * _rsqrt_sumsq(c0, KV_LORA) * gkv_ref[...]
        ckv_out[rows, :] = cn
        cb = cn.astype(BF16)
        kz = _dot(cb, wkk_ref[...])
        for hd in range(N_HEADS):
            cols = slice(hd * LANES, (hd + 1) * LANES)
            xh = kz[:, cols]
            kn = xh * _rsqrt_sumsq(xh, QK_NOPE)
            k_out[rows, cols] = jnp.where(lo, kn, kr_hi).astype(BF16)
        if v_transposed:
            vt = _dot_nt(wkv_ref[...], cb).astype(BF16)
            tail = jnp.where(lax.broadcasted_iota(jnp.int32, (BF16_SUBLANES, sub), 0) == 0, 1.0, 0.0).astype(BF16)
            for hd in range(N_HEADS):
                v_out[hd * V_ROWS:hd * V_ROWS + V_HEAD, rows] = vt[hd * V_HEAD:(hd + 1) * V_HEAD]
                v_out[hd * V_ROWS + V_HEAD:(hd + 1) * V_ROWS, rows] = tail
        else:
            v_out[rows, :] = _dot(cb, wkv_ref[...]).astype(BF16)

        for p in range(GMLP_GROUPS // 2):
            sl = slice(p * LANES, (p + 1) * LANES)
            w_pair = jnp.concatenate(
                [jnp.where(keep, ws_ref[2 * p], 0.0), jnp.where(keep, ws_ref[2 * p + 1], 0.0)], axis=0).astype(BF16)
            ug = _gelu_tanh(z[:, COL_U + p * LANES:COL_U + (p + 1) * LANES])
            vg = _gelu_tanh(z[:, COL_V + p * LANES:COL_V + (p + 1) * LANES])
            xc = vg - _half_sums(vg, lo) * (1.0 / GMLP_GROUP_DIM)
            vn = xc * _half_rsqrt_sumsq(xc, lo) * lng_ref[:, sl] + lnb_ref[:, sl]
            if vn_out:
                vn_out[0][rows, sl] = vn
            vnb = vn.astype(BF16)
            for c in range(sub // GMLP_CHUNK):
                crows = slice(c * GMLP_CHUNK, (c + 1) * GMLP_CHUNK)
                res = _dot(w_pair, vnb[crows])
                mixed = jnp.where(lo_c, res[0:GMLP_CHUNK], res[GMLP_CHUNK:]) + bs_ref[:, sl]
                og = ug[crows] * mixed
                ogm_out[r0 + c * GMLP_CHUNK:r0 + (c + 1) * GMLP_CHUNK, sl] = (
                    og * _half_rsqrt_sumsq(og, lo_c) * ggm_ref[:, sl]).astype(BF16)

    starts = list(range(0, tm, sub))
    z_next = project(starts[0])
    for i, r0 in enumerate(starts):
        z_cur = z_next
        if i + 1 < len(starts):
            z_next = project(starts[i + 1])
        finish(z_cur, r0)


def _const_spec(shape):
    return pl.BlockSpec(shape, lambda *_: (0,) * len(shape), pipeline_mode=pl.Buffered(1))


def _layer_spec(stacked, layer):
    return pl.BlockSpec((None,) + stacked.shape[1:], lambda *_: (layer, 0, 0), pipeline_mode=pl.Buffered(1))


def _inproj(x, big, lw, tq, tk, ws, bs, *, layer, tm, table_blocks, ws_mask, want_vn, v_transposed):
    n, d = x.shape
    w_v = lw["w_kvv_t"] if v_transposed else lw["w_kvv"]
    grid = (n // tm,)
    tbl = lambda i: (i % table_blocks, 0)
    row = lambda i: (i, 0)
    in_specs = [
        pl.BlockSpec((tm, d), row),
        _layer_spec(big["w_in"], layer),
        _const_spec((Q_LORA, N_HEADS * LANES)),
        _const_spec((1, LANES)),
        pl.BlockSpec((tm, LANES), tbl),
        _const_spec((1, KV_LORA)),
        _const_spec((1, LANES)),
        pl.BlockSpec((tm, LANES), tbl),
        _const_spec((KV_LORA, N_HEADS * LANES)),
        _const_spec(w_v.shape),
        _const_spec((1, GMLP_WIDTH)),
        _const_spec((1, GMLP_WIDTH)),
        _const_spec((GMLP_GROUPS, GMLP_CHUNK, GMLP_CHUNK)),
        _const_spec((GMLP_CHUNK, GMLP_WIDTH)),
        _const_spec((1, GMLP_WIDTH)),
        _const_spec((LANES, LANES)),
        _const_spec((1, LANES)),
    ]
    out_shape = [
        jax.ShapeDtypeStruct((n, N_HEADS * LANES), BF16),
        jax.ShapeDtypeStruct((n, N_HEADS * LANES), BF16),
        (jax.ShapeDtypeStruct((N_HEADS * V_ROWS, n), BF16) if v_transposed
         else jax.ShapeDtypeStruct((n, N_HEADS * V_HEAD), BF16)),
        jax.ShapeDtypeStruct((n, KV_LORA), F32),
        (jax.ShapeDtypeStruct((QK_ROPE, n), F32) if v_transposed
         else jax.ShapeDtypeStruct((n, QK_ROPE), F32)),
        jax.ShapeDtypeStruct((n, GMLP_WIDTH), BF16),
    ]
    out_specs = [
        pl.BlockSpec((tm, N_HEADS * LANES), row),
        pl.BlockSpec((tm, N_HEADS * LANES), row),
        (pl.BlockSpec((N_HEADS * V_ROWS, tm), lambda i: (0, i)) if v_transposed
         else pl.BlockSpec((tm, N_HEADS * V_HEAD), row)),
        pl.BlockSpec((tm, KV_LORA), row),
        (pl.BlockSpec((QK_ROPE, tm), lambda i: (0, i)) if v_transposed else pl.BlockSpec((tm, QK_ROPE), row)),
        pl.BlockSpec((tm, GMLP_WIDTH), row),
    ]
    if want_vn:
        out_shape.append(jax.ShapeDtypeStruct((n, GMLP_WIDTH), F32))
        out_specs.append(pl.BlockSpec((tm, GMLP_WIDTH), row))
    return pl.pallas_call(
        functools.partial(_inproj_kernel, ws_mask=ws_mask, v_transposed=v_transposed),
        grid=grid, in_specs=in_specs, out_specs=out_specs, out_shape=out_shape,
        compiler_params=pltpu.CompilerParams(dimension_semantics=("arbitrary",),
                                             vmem_limit_bytes=VMEM_LIMIT_BYTES),
        name="inproj_vn" if want_vn else "inproj",
    )(x, big["w_in"], lw["w_q"], lw["gq"], tq, lw["g_kv_lora"], lw["gkr"], tk,
      lw["w_kvk"], w_v, lw["ln_g"], lw["ln_b"], ws, bs, lw["g_out_gm"],
      jnp.asarray(_Q_SEGMENTS, BF16), jnp.asarray(_Q_SEGMENT_EPS))


ATTN_STREAM_LAG = 18


def _attn_prompt_kernel(q_ref, k_ref, vt_ref, bias_ref, g_ref, o_ref, s_scr, *, tq):
    s_len = q_ref.shape[0]
    bodies = [(qi, hh) for qi in reversed(range(s_len // tq)) for hh in range(2)]
    bases = np.cumsum([0] + [qi + 1 for qi, _ in bodies])
    state = [{} for _ in bodies]

    def a_steps(bi):
        qi, hh = bodies[bi]
        cols = slice(hh * LANES, (hh + 1) * LANES)
        rows = slice(qi * tq, (qi + 1) * tq)
        st = state[bi]

        def chunk(j):
            s = _dot_nt(k_ref[j * tq:(j + 1) * tq, cols], q_ref[rows, cols])
            if j == qi:
                s = s + bias_ref[...]
            s_scr[bases[bi] + j] = s
            cm = jnp.max(s.reshape(tq // 8, 8, tq), axis=0)
            st["m_run"] = cm if j == 0 else jnp.maximum(st["m_run"], cm)

        def finish():
            st["m"] = jnp.max(st["m_run"], axis=0, keepdims=True)

        return [functools.partial(chunk, j) for j in range(qi + 1)] + [finish]

    def b_steps(bi):
        qi, hh = bodies[bi]
        st = state[bi]

        def chunk(j):
            p = jnp.exp2(s_scr[bases[bi] + j] - st["m"]).astype(BF16)
            d = _dot(vt_ref[hh * V_ROWS:(hh + 1) * V_ROWS, j * tq:(j + 1) * tq], p)
            st["acc"] = d if j == 0 else st["acc"] + d

        def finish():
            acc = st["acc"]
            o = acc[0:V_HEAD] * (1.0 / acc[V_HEAD:V_HEAD + 1])
            r = lax.rsqrt(jnp.sum(o * o, axis=0, keepdims=True) * (1.0 / V_HEAD) + EPS)
            st["out"] = o * r * g_ref[hh * V_HEAD:(hh + 1) * V_HEAD, :]
            if hh == 1:
                pair_t = jnp.concatenate([state[bi - 1]["out"], st["out"]], axis=0)
                o_ref[qi * tq:(qi + 1) * tq, :] = pair_t.T.astype(BF16)

        return [functools.partial(chunk, j) for j in range(qi + 1)] + [finish]

    a_all = [f for bi in range(len(bodies)) for f in a_steps(bi)]
    b_all = [f for bi in range(len(bodies)) for f in b_steps(bi)]
    for i in range(len(a_all) + ATTN_STREAM_LAG):
        if i < len(a_all):
            a_all[i]()
        if i >= ATTN_STREAM_LAG:
            b_all[i - ATTN_STREAM_LAG]()


def _attn_prompt(q, k, vt, bias_t, g_t, *, batch, seq, tq):
    n = q.shape[0]
    pairs = N_HEADS // 2
    nq = seq // tq
    assert ATTN_STREAM_LAG > nq + 1
    return pl.pallas_call(
        functools.partial(_attn_prompt_kernel, tq=tq),
        grid=(batch, pairs),
        in_specs=[
            pl.BlockSpec((seq, 2 * LANES), lambda b, p: (b, p)),
            pl.BlockSpec((seq, 2 * LANES), lambda b, p: (b, p)),
            pl.BlockSpec((2 * V_ROWS, seq), lambda b, p: (p, b)),
            pl.BlockSpec((tq, tq), lambda b, p: (0, 0)),
            pl.BlockSpec((2 * V_HEAD, tq), lambda b, p: (p, 0)),
        ],
        out_specs=pl.BlockSpec((seq, 2 * V_HEAD), lambda b, p: (b, p)),
        out_shape=jax.ShapeDtypeStruct((n, N_HEADS * V_HEAD), BF16),
        scratch_shapes=[pltpu.VMEM((nq * (nq + 1), tq, tq), F32)],
        compiler_params=pltpu.CompilerParams(dimension_semantics=("arbitrary", "arbitrary"),
                                             vmem_limit_bytes=VMEM_LIMIT_BYTES),
        name="attn_prompt",
    )(q, k, vt, bias_t, g_t)


SAMPLE_KEY_BLOCK = 1024


def _attn_sample_kernel(q_ref, cnew_ref, krnew_ref, ckv_ref, krt_ref, w2_ref, wkt_ref, wv_ref, bias_ref, gout_ref,
                        o_ref, kr_scr, *, kblk):
    past = ckv_ref.shape[0]
    sq = q_ref.shape[0]
    q2 = jnp.concatenate([_dot(q_ref[:, h * LANES:(h + 1) * LANES], w2_ref[h]) for h in range(N_HEADS)],
                         axis=0).astype(BF16)
    qt = q2[:, 0:KV_LORA]
    qr = q2[:, KV_LORA:]
    kr_t = jnp.concatenate([krt_ref[...].astype(BF16), jnp.zeros((LANES - QK_ROPE, past), BF16)], axis=0)
    kr_scr[...] = jnp.zeros(kr_scr.shape, F32)
    kr_scr[0:sq, 0:QK_ROPE] = krnew_ref[...]
    ones_blk = jnp.ones((kblk, LANES), BF16)

    def scores(cb, s2):
        kzt = _dot_nt(wkt_ref[...], cb)
        s1 = _dot_nt(qt, cb)
        out = []
        for h in range(N_HEADS):
            kh = kzt[h * QK_NOPE:(h + 1) * QK_NOPE]
            r = lax.rsqrt(jnp.sum(kh * kh, axis=0, keepdims=True) + QK_NOPE * EPS)
            out.append(s1[h * sq:(h + 1) * sq] * r + s2[h * sq:(h + 1) * sq])
        return jnp.concatenate(out, axis=0)

    cn = jnp.concatenate([cnew_ref[...], jnp.zeros((LANES - sq, KV_LORA), F32)], axis=0).astype(BF16)
    blocks = [(ckv_ref[j * kblk:(j + 1) * kblk, :], kr_t[:, j * kblk:(j + 1) * kblk]) for j in range(past // kblk)]

    def block_scores(j):
        if j < len(blocks):
            cb = blocks[j][0].astype(BF16)
            return scores(cb, _dot(qr, blocks[j][1])), jnp.concatenate([cb, ones_blk], axis=1)
        s_new = scores(cn, _dot_nt(qr, kr_scr[...].astype(BF16))) + bias_ref[...]
        return s_new, jnp.concatenate([cn, ones_blk[0:LANES]], axis=1)

    m = acc = None
    nxt = block_scores(0)
    for j in range(len(blocks) + 1):
        s, vals = nxt
        if j < len(blocks):
            nxt = block_scores(j + 1)
        m_blk = jnp.max(s, axis=-1, keepdims=True)
        m_new = m_blk if m is None else jnp.maximum(m, m_blk)
        d = _dot(jnp.exp2(s - m_new).astype(BF16), vals)
        acc = d if acc is None else acc * jnp.exp2(m - m_new) + d
        m = m_new
    o_lat = (acc[:, 0:KV_LORA] * (1.0 / acc[:, KV_LORA:])).astype(BF16)
    o_full = _dot(o_lat, wv_ref[...])
    head = lax.broadcasted_iota(jnp.int32, (sq, N_HEADS * V_HEAD), 1) >> 6
    o = jnp.zeros((sq, N_HEADS * V_HEAD), F32)
    for h in range(N_HEADS):
        o = jnp.where(head == h, o_full[h * sq:(h + 1) * sq], o)
    lo = lax.broadcasted_iota(jnp.int32, (sq, LANES), 1) < 64
    for c in range(0, N_HEADS * V_HEAD, LANES):
        oc = o[:, c:c + LANES]
        o_ref[:, c:c + LANES] = (oc * _half_rms_scale(oc, lo) * gout_ref[:, c:c + LANES]).astype(BF16)


def _attn_sample(q, cnew, krnew, cache_ckv, kr_cache_t, w2, wkt, w_kvv, bias, gout, *, layer, batch, seq):
    n = q.shape[0]
    past = cache_ckv.shape[2]
    assert V_HEAD == 64 and past % SAMPLE_KEY_BLOCK == 0 and seq <= LANES
    return pl.pallas_call(
        functools.partial(_attn_sample_kernel, kblk=SAMPLE_KEY_BLOCK),
        grid=(batch,),
        in_specs=[
            pl.BlockSpec((seq, N_HEADS * LANES), lambda b: (b, 0)),
            pl.BlockSpec((seq, KV_LORA), lambda b: (b, 0)),
            pl.BlockSpec((seq, QK_ROPE), lambda b: (b, 0)),
            pl.BlockSpec((None, None, past, KV_LORA), lambda b: (layer, b, 0, 0)),
            pl.BlockSpec((None, None, QK_ROPE, past), lambda b: (layer, b, 0, 0)),
            pl.BlockSpec((N_HEADS, LANES, 2 * LANES), lambda b: (0, 0, 0)),
            pl.BlockSpec((N_HEADS * QK_NOPE, KV_LORA), lambda b: (0, 0)),
            pl.BlockSpec((KV_LORA, N_HEADS * V_HEAD), lambda b: (0, 0)),
            pl.BlockSpec((N_HEADS * seq, LANES), lambda b: (0, 0)),
            pl.BlockSpec((1, N_HEADS * V_HEAD), lambda b: (0, 0)),
        ],
        out_specs=pl.BlockSpec((seq, N_HEADS * V_HEAD), lambda b: (b, 0)),
        out_shape=jax.ShapeDtypeStruct((n, N_HEADS * V_HEAD), BF16),
        scratch_shapes=[pltpu.VMEM((LANES, LANES), F32)],
        compiler_params=pltpu.CompilerParams(dimension_semantics=("arbitrary",),
                                             vmem_limit_bytes=VMEM_LIMIT_BYTES),
        name="attn_sample",
    )(q, cnew, krnew, cache_ckv, kr_cache_t, w2, wkt, w_kvv, bias, gout)


def _ffn_kernel(x_ref, oa_ref, og_ref, wo_ref, wg_ref, wu_ref, wd_ref, y_ref, *, fchunk):
    aw = oa_ref.shape[1]
    x1 = x_ref[...] + _dot(oa_ref[...], wo_ref[0:aw, :]) + _dot(og_ref[...], wo_ref[aw:, :])
    h2 = (x1 * _rsqrt_sumsq(x1, x1.shape[1])).astype(BF16)
    acc = None
    for c in range(0, wg_ref.shape[1], fchunk):
        g = _dot(h2, wg_ref[:, c:c + fchunk])
        u = _dot(h2, wu_ref[:, c:c + fchunk])
        a = (g * jax.nn.sigmoid(g) * u).astype(BF16)
        d = _dot(a, wd_ref[c:c + fchunk, :])
        acc = d if acc is None else acc + d
    y_ref[...] = x1 + acc


def _ffn(x, oa, og, big, *, layer, tm, fchunk):
    n, d = x.shape
    row = lambda i: (i, 0)
    return pl.pallas_call(
        functools.partial(_ffn_kernel, fchunk=fchunk),
        grid=(n // tm,),
        in_specs=[
            pl.BlockSpec((tm, d), row),
            pl.BlockSpec((tm, oa.shape[1]), row),
            pl.BlockSpec((tm, og.shape[1]), row),
            _layer_spec(big["w_o"], layer),
            _layer_spec(big["w_gate"], layer),
            _layer_spec(big["w_up"], layer),
            _layer_spec(big["w_down"], layer),
        ],
        out_specs=pl.BlockSpec((tm, d), row),
        out_shape=jax.ShapeDtypeStruct((n, d), F32),
        compiler_params=pltpu.CompilerParams(dimension_semantics=("arbitrary",),
                                             vmem_limit_bytes=VMEM_LIMIT_BYTES),
        name="outproj_ffn",
    )(x, oa, og, big["w_o"], big["w_gate"], big["w_up"], big["w_down"])


def _rope_tables(pos):
    inv = ROPE_THETA ** (-np.arange(HALF, dtype=np.float64) / HALF)
    ang = np.asarray(pos, np.float64)[:, None] * inv[None, :]
    cos, sin = np.cos(ang), np.sin(ang)
    n = ang.shape[0]
    tq = np.concatenate([np.ones((n, QK_NOPE)), cos, cos, -sin, sin], axis=1)
    tk = np.concatenate([cos, cos, -sin, sin, np.zeros((n, LANES - 2 * QK_ROPE))], axis=1)
    return tq.astype(np.float32), tk.astype(np.float32)


def _rope_fold_matrix():
    f = np.zeros((LANES, LANES), np.float32)
    for i in range(QK_ROPE):
        f[QK_NOPE + i, i] = 1.0
        f[QK_NOPE + QK_ROPE + i, i] = 1.0
    return f


_ROPE_FOLD = _rope_fold_matrix()


def _q_segment_matrix():
    m = np.zeros((LANES, LANES), np.float32)
    m[:QK_NOPE, :QK_NOPE] = 1.0
    m[QK_NOPE:QK_NOPE + QK_ROPE, QK_NOPE:] = 1.0
    return m


_Q_SEGMENTS = _q_segment_matrix()
_Q_SEGMENT_COUNT = np.concatenate([np.full(QK_NOPE, QK_NOPE), np.full(LANES - QK_NOPE, QK_ROPE)]).astype(np.float32)
_Q_SEGMENT_EPS = (_Q_SEGMENT_COUNT * EPS)[None, :]


def _swap_halves(g):
    return jnp.concatenate([g[..., HALF:], g[..., :HALF]], axis=-1)


def _stacked_weights(g_mix, w_in, w_o, g_ffn, w_gate, w_up, w_down):
    depth, d_model, _ = w_in.shape
    w = (jnp.swapaxes(w_in, 1, 2) * (g_mix * d_model ** 0.5)[:, None, :]).astype(BF16)
    kr0 = Q_LORA + KV_LORA
    w_in_p = jnp.concatenate(
        [w[:, :kr0 + QK_ROPE], w[:, kr0 + HALF:kr0 + QK_ROPE], w[:, kr0:kr0 + HALF],
         jnp.zeros((depth, LANES - 2 * QK_ROPE, d_model), BF16), w[:, kr0 + QK_ROPE:]], axis=1)
    g_ffn_col = (g_ffn * d_model ** 0.5)[:, :, None]
    return {"w_in": w_in_p, "w_o": w_o.astype(BF16), "w_gate": (w_gate * g_ffn_col).astype(BF16),
            "w_up": (w_up * g_ffn_col).astype(BF16), "w_down": w_down.astype(BF16)}


def _layer_weights(l, g_q_lora, w_q_up, g_qn, g_qr, g_kv_lora, g_kr, w_kv_up, g_kn,
                   gm_ln_g, gm_ln_b, g_out_att, g_out_gm):
    wq = w_q_up[l].reshape(Q_LORA, N_HEADS, QK_HEAD)
    r1, r2 = wq[..., QK_NOPE:QK_NOPE + HALF], wq[..., QK_NOPE + HALF:]
    w_q = jnp.concatenate([wq[..., :QK_NOPE], r1, r2, r2, r1], axis=-1).reshape(Q_LORA, N_HEADS * LANES)
    w_q = w_q * (g_q_lora[l] * Q_LORA ** 0.5)[:, None]
    gq = (jnp.concatenate([g_qn[l] * g_kn[l] * QK_NOPE ** 0.5, g_qr[l], _swap_halves(g_qr[l])])
          * np.sqrt(_Q_SEGMENT_COUNT) * (QK_HEAD ** -0.5 * LOG2_E))
    gkr = jnp.concatenate([g_kr[l], _swap_halves(g_kr[l]), jnp.zeros((LANES - 2 * QK_ROPE,), F32)]) * QK_ROPE ** 0.5
    wkv = w_kv_up[l].reshape(KV_LORA, N_HEADS, QK_NOPE + V_HEAD)
    w_kvk = jnp.concatenate([wkv[..., :QK_NOPE], jnp.zeros((KV_LORA, N_HEADS, LANES - QK_NOPE), F32)],
                            axis=-1).reshape(KV_LORA, N_HEADS * LANES)
    w_kvv = wkv[..., QK_NOPE:].reshape(KV_LORA, N_HEADS * V_HEAD)
    w_kvv_t = w_kvv.T
    wk_t = jnp.transpose(wkv[..., :QK_NOPE], (1, 2, 0))
    w_abs = jnp.concatenate([wk_t, jnp.zeros((N_HEADS, LANES - QK_NOPE, KV_LORA), F32)], axis=1)
    w2 = jnp.concatenate([w_abs, jnp.broadcast_to(jnp.asarray(_ROPE_FOLD), (N_HEADS, LANES, LANES))], axis=2)
    return {
        "w_q": w_q.astype(BF16), "gq": gq[None, :],
        "g_kv_lora": g_kv_lora[l][None, :] * KV_LORA ** 0.5, "gkr": gkr[None, :],
        "w_kvk": w_kvk.astype(BF16), "w_kvv": w_kvv.astype(BF16),
        "w_kvv_t": w_kvv_t.astype(BF16),
        "w2": w2.astype(BF16), "wk_t": wk_t.reshape(N_HEADS * QK_NOPE, KV_LORA).astype(BF16),
        "ln_g": gm_ln_g[l].reshape(1, GMLP_WIDTH) * GMLP_GROUP_DIM ** 0.5, "ln_b": gm_ln_b[l].reshape(1, GMLP_WIDTH),
        "g_out_att": g_out_att[l].reshape(1, N_HEADS * V_HEAD),
        "g_out_gm": g_out_gm[l].reshape(1, GMLP_WIDTH) * GMLP_GROUP_DIM ** 0.5,
    }


def _chunk_bias(qpos, kpos):
    vis = (kpos[None, :] // CHUNK) <= (qpos[:, None] // CHUNK)
    return np.where(vis, 0.0, NEG_INF).astype(np.float32)


def kernel(x_prompt, x_sample, cache_ckv, cache_krope, g_mix, w_in, g_q_lora, w_q_up, g_qn, g_qr, g_kv_lora, g_kr,
           w_kv_up, g_kn, gm_ln_g, gm_ln_b, gm_w_s, gm_b_s, g_out_att, g_out_gm, w_o, g_ffn, w_gate, w_up, w_down):
    batch, seq, d = x_prompt.shape
    dbatch, dseq, _ = x_sample.shape
    depth, _, past, _ = cache_ckv.shape
    assert seq % GMLP_CHUNK == 0 and GMLP_CHUNK % dseq == 0 and (dbatch * dseq) % GMLP_CHUNK == 0

    tq_attn = 256
    tm_in = 1024
    tm_ffn = 1024
    tm_s = dbatch * dseq

    tq_p, tk_p = (jnp.asarray(t) for t in _rope_tables(np.arange(seq)))
    tq_s, tk_s = (jnp.asarray(np.tile(t, (dbatch, 1))) for t in _rope_tables(past + np.arange(dseq)))
    bias_p = jnp.asarray(_chunk_bias(np.arange(tq_attn), np.arange(tq_attn)))
    bias_s = np.full((dseq, LANES), NEG_INF, np.float32)
    bias_s[:, :dseq] = _chunk_bias(past + np.arange(dseq), past + np.arange(dseq))
    bias_s = jnp.asarray(np.tile(bias_s, (N_HEADS, 1)))
    reps = GMLP_CHUNK // dseq

    yp = x_prompt.reshape(batch * seq, d)
    ys = x_sample.reshape(dbatch * dseq, d)
    ckv_p, kr_p, ckv_s, kr_s, gv_s = [], [], [], [], []
    big = _stacked_weights(g_mix, w_in, w_o, g_ffn, w_gate, w_up, w_down)
    kr_cache_t = jnp.swapaxes(cache_krope, 2, 3)
    for l in range(depth):
        lw = _layer_weights(l, g_q_lora, w_q_up, g_qn, g_qr, g_kv_lora, g_kr, w_kv_up, g_kn,
                            gm_ln_g, gm_ln_b, g_out_att, g_out_gm)
        ws_p = gm_w_s[l]
        bs_p = jnp.repeat(gm_b_s[l].T, GMLP_GROUP_DIM, axis=1)
        ws_s = jnp.tile(gm_w_s[l][:, :dseq, :dseq], (1, reps, reps))
        bs_s = jnp.tile(jnp.repeat(gm_b_s[l][:, :dseq].T, GMLP_GROUP_DIM, axis=1), (reps, 1))

        q, k, vt, c, r, ogm = _inproj(yp, big, lw, tq_p, tk_p, ws_p, bs_p, layer=l, tm=tm_in, table_blocks=seq // tm_in,
                                      ws_mask="causal", want_vn=False, v_transposed=True)
        g_t = jnp.broadcast_to(lw["g_out_att"].reshape(N_HEADS * V_HEAD, 1), (N_HEADS * V_HEAD, tq_attn))
        oatt = _attn_prompt(q, k, vt, bias_p.T, g_t, batch=batch, seq=seq, tq=tq_attn)
        yp = _ffn(yp, oatt, ogm, big, layer=l, tm=tm_ffn, fchunk=256)
        ckv_p.append(c.reshape(batch, seq, KV_LORA))
        kr_p.append(jnp.transpose(r.reshape(QK_ROPE, batch, seq), (1, 2, 0)))

        q, k, v, c, r, ogm, vn = _inproj(ys, big, lw, tq_s, tk_s, ws_s, bs_s, layer=l, tm=tm_s, table_blocks=1,
                                         ws_mask="blockdiag", want_vn=True, v_transposed=False)
        oatt = _attn_sample(q, c, r, cache_ckv, kr_cache_t, lw["w2"], lw["wk_t"], lw["w_kvv"], bias_s,
                            lw["g_out_att"], layer=l, batch=dbatch, seq=dseq)
        ys = _ffn(ys, oatt, ogm, big, layer=l, tm=tm_s, fchunk=256)
        ckv_s.append(c.reshape(dbatch, dseq, KV_LORA))
        kr_s.append(r.reshape(dbatch, dseq, QK_ROPE))
        gv_s.append(vn.reshape(dbatch, dseq, GMLP_WIDTH))

    return (yp.reshape(batch, seq, d), ys.reshape(dbatch, dseq, d), jnp.stack(ckv_p), jnp.stack(kr_p),
            jnp.stack(ckv_s), jnp.stack(kr_s), jnp.stack(gv_s))
```

```python
import functools

import numpy as np
import jax
import jax.numpy as jnp
from jax import lax
from jax.experimental import pallas as pl
from jax.experimental.pallas import tpu as pltpu

F32 = jnp.float32
BF16 = jnp.bfloat16

CHUNK = 64
N_HEADS = 8
QK_NOPE = 64
QK_ROPE = 32
QK_HEAD = QK_NOPE + QK_ROPE
V_HEAD = 64
Q_LORA = 256
KV_LORA = 128
ROPE_THETA = 10000.0
GMLP_GROUPS = 8
GMLP_GROUP_DIM = 64
GMLP_WIDTH = GMLP_GROUPS * GMLP_GROUP_DIM
GMLP_CHUNK = 128
EPS = 1e-6
NEG_INF = -1e30
LOG2_E = 1.4426950408889634

LANES = 128
BF16_SUBLANES = 16
V7X_VMEM_BYTES = 64 * 1024 * 1024
VMEM_LIMIT_BYTES = (V7X_VMEM_BYTES * 3) // 4

HALF = QK_ROPE // 2
V_ROWS = V_HEAD + BF16_SUBLANES
IN_COLS_PADDED = Q_LORA + KV_LORA + LANES + 2 * GMLP_WIDTH
COL_CKV = Q_LORA
COL_KR = Q_LORA + KV_LORA
COL_U = COL_KR + LANES
COL_V = COL_U + GMLP_WIDTH

_NT = (((1,), (1,)), ((), ()))


def _dot(a, b):
    return jnp.dot(a, b, preferred_element_type=F32)


def _dot_nt(a, b):
    return lax.dot_general(a, b, _NT, preferred_element_type=F32)


def _rsqrt_sumsq(x, n):
    return lax.rsqrt(jnp.sum(x * x, axis=-1, keepdims=True) + n * EPS)


def _half_sums(x, lo):
    s_lo = jnp.sum(jnp.where(lo, x, 0.0), axis=-1, keepdims=True)
    s_hi = jnp.sum(jnp.where(lo, 0.0, x), axis=-1, keepdims=True)
    return jnp.where(lo, s_lo, s_hi)


def _half_rms_scale(x, lo):
    return lax.rsqrt(_half_sums(x * x, lo) * (1.0 / 64) + EPS)


def _half_rsqrt_sumsq(x, lo):
    return lax.rsqrt(_half_sums(x * x, lo) + 64 * EPS)


def _gelu_tanh(t):
    c = 0.7978845608028654
    h = 0.5 * t
    return h + h * jnp.tanh(t * (c + (c * 0.044715) * (t * t)))


INPROJ_SUB_ROWS = 256


def _inproj_kernel(x_ref, win_ref, wq_ref, gq_ref, tq_ref, gkv_ref, gkr_ref, tk_ref,
                   wkk_ref, wkv_ref, lng_ref, lnb_ref, ws_ref, bs_ref, ggm_ref, seg_ref, epsq_ref,
                   q_out, k_out, v_out, ckv_out, kr_out, ogm_out, *rest, ws_mask, v_transposed):
    *vn_out, win_scr = rest
    tm = x_ref.shape[0]
    sub = min(tm, INPROJ_SUB_ROWS)

    @pl.when(pl.program_id(0) == 0)
    def _():
        win_scr[...] = win_ref[...].T

    lane = lax.broadcasted_iota(jnp.int32, (sub, LANES), 1)
    lo = lane < 64
    lo_c = lax.broadcasted_iota(jnp.int32, (GMLP_CHUNK, LANES), 1) < 64
    row = lax.broadcasted_iota(jnp.int32, (GMLP_CHUNK, GMLP_CHUNK), 0) // CHUNK
    col = lax.broadcasted_iota(jnp.int32, (GMLP_CHUNK, GMLP_CHUNK), 1) // CHUNK
    keep = (row >= col) if ws_mask == "causal" else (row == col)

    def project(r0):
        x = x_ref[r0:r0 + sub, :]
        h = (x * _rsqrt_sumsq(x, x.shape[1])).astype(BF16)
        return _dot(h, win_scr[...])

    def finish(z, r0):
        rows = slice(r0, r0 + sub)
        ql = z[:, 0:Q_LORA]
        qn = (ql * _rsqrt_sumsq(ql, Q_LORA)).astype(BF16)
        qz = _dot(qn, wq_ref[...])
        tq = tq_ref[rows, :] * gq_ref[...]
        for hd in range(N_HEADS):
            cols = slice(hd * LANES, (hd + 1) * LANES)
            xh = qz[:, cols]
            ss = _dot((xh * xh).astype(BF16), seg_ref[...])
            q_out[rows, cols] = (xh * lax.rsqrt(ss + epsq_ref[...]) * tq).astype(BF16)

        y0 = z[:, COL_KR:COL_KR + LANES]
        s_k = jnp.sum(jnp.where(lane < QK_ROPE, y0 * y0, 0.0), axis=-1, keepdims=True)
        y = y0 * lax.rsqrt(s_k + QK_ROPE * EPS) * (gkr_ref[...] * tk_ref[rows, :])
        r64 = pltpu.roll(y, 64, 1)
        r32 = pltpu.roll(y, 32, 1)
        r96 = pltpu.roll(y, 96, 1)
        if v_transposed:
            kr_out[:, rows] = (y + r96).T[0:QK_ROPE, :]
        else:
            kr_out[rows, :] = (y + r96)[:, 0:QK_ROPE]
        kr_hi = jnp.where(lo, 0.0, r64 + r32 + r96)

        c0 = z[:, COL_CKV:COL_CKV + KV_LORA]
        cn = c0 * _rsqrt_sumsq(c0, KV_LORA) * gkv_ref[...]
        ckv_out[rows, :] = cn
        cb = cn.astype(BF16)
        kz = _dot(cb, wkk_ref[...])
        for hd in range(N_HEADS):
            cols = slice(hd * LANES, (hd + 1) * LANES)
            xh = kz[:, cols]
            kn = xh * _rsqrt_sumsq(xh, QK_NOPE)
            k_out[rows, cols] = jnp.where(lo, kn, kr_hi).astype(BF16)
        if v_transposed:
            vt = _dot_nt(wkv_ref[...], cb).astype(BF16)
            tail = jnp.where(lax.broadcasted_iota(jnp.int32, (BF16_SUBLANES, sub), 0) == 0, 1.0, 0.0).astype(BF16)
            for hd in range(N_HEADS):
                v_out[hd * V_ROWS:hd * V_ROWS + V_HEAD, rows] = vt[hd * V_HEAD:(hd + 1) * V_HEAD]
                v_out[hd * V_ROWS + V_HEAD:(hd + 1) * V_ROWS, rows] = tail
        else:
            v_out[rows, :] = _dot(cb, wkv_ref[...]).astype(BF16)

        for p in range(GMLP_GROUPS // 2):
            sl = slice(p * LANES, (p + 1) * LANES)
            w_pair = jnp.concatenate(
                [jnp.where(keep, ws_ref[2 * p], 0.0), jnp.where(keep, ws_ref[2 * p + 1], 0.0)], axis=0).astype(BF16)
            ug = _gelu_tanh(z[:, COL_U + p * LANES:COL_U + (p + 1) * LANES])
            vg = _gelu_tanh(z[:, COL_V + p * LANES:COL_V + (p + 1) * LANES])
            xc = vg - _half_sums(vg, lo) * (1.0 / GMLP_GROUP_DIM)
            vn = xc * _half_rsqrt_sumsq(xc, lo) * lng_ref[:, sl] + lnb_ref[:, sl]
            if vn_out:
                vn_out[0][rows, sl] = vn
            vnb = vn.astype(BF16)
            for c in range(sub // GMLP_CHUNK):
                crows = slice(c * GMLP_CHUNK, (c + 1) * GMLP_CHUNK)
                res = _dot(w_pair, vnb[crows])
                mixed = jnp.where(lo_c, res[0:GMLP_CHUNK], res[GMLP_CHUNK:]) + bs_ref[:, sl]
                og = ug[crows] * mixed
                ogm_out[r0 + c * GMLP_CHUNK:r0 + (c + 1) * GMLP_CHUNK, sl] = (
                    og * _half_rsqrt_sumsq(og, lo_c) * ggm_ref[:, sl]).astype(BF16)

    starts = list(range(0, tm, sub))
    z_next = project(starts[0])
    for i, r0 in enumerate(starts):
        z_cur = z_next
        if i + 1 < len(starts):
            z_next = project(starts[i + 1])
        finish(z_cur, r0)


def _const_spec(shape):
    return pl.BlockSpec(shape, lambda *_: (0,) * len(shape), pipeline_mode=pl.Buffered(1))


def _layer_spec(stacked, layer):
    return pl.BlockSpec((None,) + stacked.shape[1:], lambda *_: (layer, 0, 0), pipeline_mode=pl.Buffered(1))


def _inproj(x, big, lw, tq, tk, ws, bs, *, layer, tm, table_blocks, ws_mask, want_vn, v_transposed):
    n, d = x.shape
    w_v = lw["w_kvv_t"] if v_transposed else lw["w_kvv"]
    grid = (n // tm,)
    tbl = lambda i: (i % table_blocks, 0)
    row = lambda i: (i, 0)
    in_specs = [
        pl.BlockSpec((tm, d), row),
        _layer_spec(big["w_in"], layer),
        _const_spec((Q_LORA, N_HEADS * LANES)),
        _const_spec((1, LANES)),
        pl.BlockSpec((tm, LANES), tbl),
        _const_spec((1, KV_LORA)),
        _const_spec((1, LANES)),
        pl.BlockSpec((tm, LANES), tbl),
        _const_spec((KV_LORA, N_HEADS * LANES)),
        _const_spec(w_v.shape),
        _const_spec((1, GMLP_WIDTH)),
        _const_spec((1, GMLP_WIDTH)),
        _const_spec((GMLP_GROUPS, GMLP_CHUNK, GMLP_CHUNK)),
        _const_spec((GMLP_CHUNK, GMLP_WIDTH)),
        _const_spec((1, GMLP_WIDTH)),
        _const_spec((LANES, LANES)),
        _const_spec((1, LANES)),
    ]
    out_shape = [
        jax.ShapeDtypeStruct((n, N_HEADS * LANES), BF16),
        jax.ShapeDtypeStruct((n, N_HEADS * LANES), BF16),
        (jax.ShapeDtypeStruct((N_HEADS * V_ROWS, n), BF16) if v_transposed
         else jax.ShapeDtypeStruct((n, N_HEADS * V_HEAD), BF16)),
        jax.ShapeDtypeStruct((n, KV_LORA), F32),
        (jax.ShapeDtypeStruct((QK_ROPE, n), F32) if v_transposed
         else jax.ShapeDtypeStruct((n, QK_ROPE), F32)),
        jax.ShapeDtypeStruct((n, GMLP_WIDTH), BF16),
    ]
    out_specs = [
        pl.BlockSpec((tm, N_HEADS * LANES), row),
        pl.BlockSpec((tm, N_HEADS * LANES), row),
        (pl.BlockSpec((N_HEADS * V_ROWS, tm), lambda i: (0, i)) if v_transposed
         else pl.BlockSpec((tm, N_HEADS * V_HEAD), row)),
        pl.BlockSpec((tm, KV_LORA), row),
        (pl.BlockSpec((QK_ROPE, tm), lambda i: (0, i)) if v_transposed else pl.BlockSpec((tm, QK_ROPE), row)),
        pl.BlockSpec((tm, GMLP_WIDTH), row),
    ]
    if want_vn:
        out_shape.append(jax.ShapeDtypeStruct((n, GMLP_WIDTH), F32))
        out_specs.append(pl.BlockSpec((tm, GMLP_WIDTH), row))
    return pl.pallas_call(
        functools.partial(_inproj_kernel, ws_mask=ws_mask, v_transposed=v_transposed),
        grid=grid, in_specs=in_specs, out_specs=out_specs, out_shape=out_shape,
        scratch_shapes=[pltpu.VMEM((d, IN_COLS_PADDED), BF16)],
        compiler_params=pltpu.CompilerParams(dimension_semantics=("arbitrary",),
                                             vmem_limit_bytes=VMEM_LIMIT_BYTES),
        name="inproj_vn" if want_vn else "inproj",
    )(x, big["w_in"], lw["w_q"], lw["gq"], tq, lw["g_kv_lora"], lw["gkr"], tk,
      lw["w_kvk"], w_v, lw["ln_g"], lw["ln_b"], ws, bs, lw["g_out_gm"],
      jnp.asarray(_Q_SEGMENTS, BF16), jnp.asarray(_Q_SEGMENT_EPS))


ATTN_STREAM_LAG = 18


def _attn_prompt_kernel(q_ref, k_ref, vt_ref, bias_ref, g_ref, o_ref, s_scr, *, tq):
    s_len = q_ref.shape[0]
    bodies = [(qi, hh) for qi in reversed(range(s_len // tq)) for hh in range(2)]
    bases = np.cumsum([0] + [qi + 1 for qi, _ in bodies])
    state = [{} for _ in bodies]

    def a_steps(bi):
        qi, hh = bodies[bi]
        cols = slice(hh * LANES, (hh + 1) * LANES)
        rows = slice(qi * tq, (qi + 1) * tq)
        st = state[bi]

        def chunk(j):
            s = _dot_nt(k_ref[j * tq:(j + 1) * tq, cols], q_ref[rows, cols])
            if j == qi:
                s = s + bias_ref[...]
            s_scr[bases[bi] + j] = s
            cm = jnp.max(s.reshape(tq // 8, 8, tq), axis=0)
            st["m_run"] = cm if j == 0 else jnp.maximum(st["m_run"], cm)

        def finish():
            st["m"] = jnp.max(st["m_run"], axis=0, keepdims=True)

        return [functools.partial(chunk, j) for j in range(qi + 1)] + [finish]

    def b_steps(bi):
        qi, hh = bodies[bi]
        st = state[bi]

        def chunk(j):
            p = jnp.exp2(s_scr[bases[bi] + j] - st["m"]).astype(BF16)
            d = _dot(vt_ref[hh * V_ROWS:(hh + 1) * V_ROWS, j * tq:(j + 1) * tq], p)
            st["acc"] = d if j == 0 else st["acc"] + d

        def finish():
            acc = st["acc"]
            o = acc[0:V_HEAD] * (1.0 / acc[V_HEAD:V_HEAD + 1])
            r = lax.rsqrt(jnp.sum(o * o, axis=0, keepdims=True) * (1.0 / V_HEAD) + EPS)
            st["out"] = o * r * g_ref[hh * V_HEAD:(hh + 1) * V_HEAD, :]
            if hh == 1:
                pair_t = jnp.concatenate([state[bi - 1]["out"], st["out"]], axis=0)
                o_ref[qi * tq:(qi + 1) * tq, :] = pair_t.T.astype(BF16)

        return [functools.partial(chunk, j) for j in range(qi + 1)] + [finish]

    a_all = [f for bi in range(len(bodies)) for f in a_steps(bi)]
    b_all = [f for bi in range(len(bodies)) for f in b_steps(bi)]
    for i in range(len(a_all) + ATTN_STREAM_LAG):
        if i < len(a_all):
            a_all[i]()
        if i >= ATTN_STREAM_LAG:
            b_all[i - ATTN_STREAM_LAG]()


def _attn_prompt(q, k, vt, bias_t, g_t, *, batch, seq, tq):
    n = q.shape[0]
    pairs = N_HEADS // 2
    nq = seq // tq
    assert ATTN_STREAM_LAG > nq + 1
    return pl.pallas_call(
        functools.partial(_attn_prompt_kernel, tq=tq),
        grid=(batch, pairs),
        in_specs=[
            pl.BlockSpec((seq, 2 * LANES), lambda b, p: (b, p)),
            pl.BlockSpec((seq, 2 * LANES), lambda b, p: (b, p)),
            pl.BlockSpec((2 * V_ROWS, seq), lambda b, p: (p, b)),
            pl.BlockSpec((tq, tq), lambda b, p: (0, 0)),
            pl.BlockSpec((2 * V_HEAD, tq), lambda b, p: (p, 0)),
        ],
        out_specs=pl.BlockSpec((seq, 2 * V_HEAD), lambda b, p: (b, p)),
        out_shape=jax.ShapeDtypeStruct((n, N_HEADS * V_HEAD), BF16),
        scratch_shapes=[pltpu.VMEM((nq * (nq + 1), tq, tq), F32)],
        compiler_params=pltpu.CompilerParams(dimension_semantics=("arbitrary", "arbitrary"),
                                             vmem_limit_bytes=VMEM_LIMIT_BYTES),
        name="attn_prompt",
    )(q, k, vt, bias_t, g_t)


SAMPLE_KEY_BLOCK = 1024


def _attn_sample_kernel(q_ref, cnew_ref, krnew_ref, ckv_ref, krt_ref, w2_ref, wkt_ref, wv_ref, bias_ref, gout_ref,
                        o_ref, kr_scr, *, kblk):
    past = ckv_ref.shape[0]
    sq = q_ref.shape[0]
    q2 = jnp.concatenate([_dot(q_ref[:, h * LANES:(h + 1) * LANES], w2_ref[h]) for h in range(N_HEADS)],
                         axis=0).astype(BF16)
    qt = q2[:, 0:KV_LORA]
    qr = q2[:, KV_LORA:]
    kr_t = jnp.concatenate([krt_ref[...].astype(BF16), jnp.zeros((LANES - QK_ROPE, past), BF16)], axis=0)
    kr_scr[...] = jnp.zeros(kr_scr.shape, F32)
    kr_scr[0:sq, 0:QK_ROPE] = krnew_ref[...]
    ones_blk = jnp.ones((kblk, LANES), BF16)

    def scores(cb, s2):
        kzt = _dot_nt(wkt_ref[...], cb)
        s1 = _dot_nt(qt, cb)
        out = []
        for h in range(N_HEADS):
            kh = kzt[h * QK_NOPE:(h + 1) * QK_NOPE]
            r = lax.rsqrt(jnp.sum(kh * kh, axis=0, keepdims=True) + QK_NOPE * EPS)
            out.append(s1[h * sq:(h + 1) * sq] * r + s2[h * sq:(h + 1) * sq])
        return jnp.concatenate(out, axis=0)

    cn = jnp.concatenate([cnew_ref[...], jnp.zeros((LANES - sq, KV_LORA), F32)], axis=0).astype(BF16)
    blocks = [(ckv_ref[j * kblk:(j + 1) * kblk, :], kr_t[:, j * kblk:(j + 1) * kblk]) for j in range(past // kblk)]

    def block_scores(j):
        if j < len(blocks):
            cb = blocks[j][0].astype(BF16)
            return scores(cb, _dot(qr, blocks[j][1])), jnp.concatenate([cb, ones_blk], axis=1)
        s_new = scores(cn, _dot_nt(qr, kr_scr[...].astype(BF16))) + bias_ref[...]
        return s_new, jnp.concatenate([cn, ones_blk[0:LANES]], axis=1)

    m = acc = None
    nxt = block_scores(0)
    for j in range(len(blocks) + 1):
        s, vals = nxt
        if j < len(blocks):
            nxt = block_scores(j + 1)
        m_blk = jnp.max(s, axis=-1, keepdims=True)
        m_new = m_blk if m is None else jnp.maximum(m, m_blk)
        d = _dot(jnp.exp2(s - m_new).astype(BF16), vals)
        acc = d if acc is None else acc * jnp.exp2(m - m_new) + d
        m = m_new
    o_lat = (acc[:, 0:KV_LORA] * (1.0 / acc[:, KV_LORA:])).astype(BF16)
    o_full = _dot(o_lat, wv_ref[...])
    head = lax.broadcasted_iota(jnp.int32, (sq, N_HEADS * V_HEAD), 1) >> 6
    o = jnp.zeros((sq, N_HEADS * V_HEAD), F32)
    for h in range(N_HEADS):
        o = jnp.where(head == h, o_full[h * sq:(h + 1) * sq], o)
    lo = lax.broadcasted_iota(jnp.int32, (sq, LANES), 1) < 64
    for c in range(0, N_HEADS * V_HEAD, LANES):
        oc = o[:, c:c + LANES]
        o_ref[:, c:c + LANES] = (oc * _half_rms_scale(oc, lo) * gout_ref[:, c:c + LANES]).astype(BF16)


def _attn_sample(q, cnew, krnew, cache_ckv, kr_cache_t, w2, wkt, w_kvv, bias, gout, *, layer, batch, seq):
    n = q.shape[0]
    past = cache_ckv.shape[2]
    assert V_HEAD == 64 and past % SAMPLE_KEY_BLOCK == 0 and seq <= LANES
    return pl.pallas_call(
        functools.partial(_attn_sample_kernel, kblk=SAMPLE_KEY_BLOCK),
        grid=(batch,),
        in_specs=[
            pl.BlockSpec((seq, N_HEADS * LANES), lambda b: (b, 0)),
            pl.BlockSpec((seq, KV_LORA), lambda b: (b, 0)),
            pl.BlockSpec((seq, QK_ROPE), lambda b: (b, 0)),
            pl.BlockSpec((None, None, past, KV_LORA), lambda b: (layer, b, 0, 0)),
            pl.BlockSpec((None, None, QK_ROPE, past), lambda b: (layer, b, 0, 0)),
            pl.BlockSpec((N_HEADS, LANES, 2 * LANES), lambda b: (0, 0, 0)),
            pl.BlockSpec((N_HEADS * QK_NOPE, KV_LORA), lambda b: (0, 0)),
            pl.BlockSpec((KV_LORA, N_HEADS * V_HEAD), lambda b: (0, 0)),
            pl.BlockSpec((N_HEADS * seq, LANES), lambda b: (0, 0)),
            pl.BlockSpec((1, N_HEADS * V_HEAD), lambda b: (0, 0)),
        ],
        out_specs=pl.BlockSpec((seq, N_HEADS * V_HEAD), lambda b: (b, 0)),
        out_shape=jax.ShapeDtypeStruct((n, N_HEADS * V_HEAD), BF16),
        scratch_shapes=[pltpu.VMEM((LANES, LANES), F32)],
        compiler_params=pltpu.CompilerParams(dimension_semantics=("arbitrary",),
                                             vmem_limit_bytes=VMEM_LIMIT_BYTES),
        name="attn_sample",
    )(q, cnew, krnew, cache_ckv, kr_cache_t, w2, wkt, w_kvv, bias, gout)


def _ffn_kernel(x_ref, oa_ref, og_ref, wo_ref, wg_ref, wu_ref, wd_ref, y_ref, *, fchunk):
    aw = oa_ref.shape[1]
    x1 = x_ref[...] + _dot(oa_ref[...], wo_ref[0:aw, :]) + _dot(og_ref[...], wo_ref[aw:, :])
    h2 = (x1 * _rsqrt_sumsq(x1, x1.shape[1])).astype(BF16)
    acc = None
    for c in range(0, wg_ref.shape[1], fchunk):
        g = _dot(h2, wg_ref[:, c:c + fchunk])
        u = _dot(h2, wu_ref[:, c:c + fchunk])
        a = (g * jax.nn.sigmoid(g) * u).astype(BF16)
        d = _dot(a, wd_ref[c:c + fchunk, :])
        acc = d if acc is None else acc + d
    y_ref[...] = x1 + acc


def _ffn(x, oa, og, big, *, layer, tm, fchunk):
    n, d = x.shape
    row = lambda i: (i, 0)
    return pl.pallas_call(
        functools.partial(_ffn_kernel, fchunk=fchunk),
        grid=(n // tm,),
        in_specs=[
            pl.BlockSpec((tm, d), row),
            pl.BlockSpec((tm, oa.shape[1]), row),
            pl.BlockSpec((tm, og.shape[1]), row),
            _layer_spec(big["w_o"], layer),
            _layer_spec(big["w_gate"], layer),
            _layer_spec(big["w_up"], layer),
            _layer_spec(big["w_down"], layer),
        ],
        out_specs=pl.BlockSpec((tm, d), row),
        out_shape=jax.ShapeDtypeStruct((n, d), F32),
        compiler_params=pltpu.CompilerParams(dimension_semantics=("arbitrary",),
                                             vmem_limit_bytes=VMEM_LIMIT_BYTES),
        name="outproj_ffn",
    )(x, oa, og, big["w_o"], big["w_gate"], big["w_up"], big["w_down"])


def _rope_tables(pos):
    inv = ROPE_THETA ** (-np.arange(HALF, dtype=np.float64) / HALF)
    ang = np.asarray(pos, np.float64)[:, None] * inv[None, :]
    cos, sin = np.cos(ang), np.sin(ang)
    n = ang.shape[0]
    tq = np.concatenate([np.ones((n, QK_NOPE)), cos, cos, -sin, sin], axis=1)
    tk = np.concatenate([cos, cos, -sin, sin, np.zeros((n, LANES - 2 * QK_ROPE))], axis=1)
    return tq.astype(np.float32), tk.astype(np.float32)


def _rope_fold_matrix():
    f = np.zeros((LANES, LANES), np.float32)
    for i in range(QK_ROPE):
        f[QK_NOPE + i, i] = 1.0
        f[QK_NOPE + QK_ROPE + i, i] = 1.0
    return f


_ROPE_FOLD = _rope_fold_matrix()


def _q_segment_matrix():
    m = np.zeros((LANES, LANES), np.float32)
    m[:QK_NOPE, :QK_NOPE] = 1.0
    m[QK_NOPE:QK_NOPE + QK_ROPE, QK_NOPE:] = 1.0
    return m


_Q_SEGMENTS = _q_segment_matrix()
_Q_SEGMENT_COUNT = np.concatenate([np.full(QK_NOPE, QK_NOPE), np.full(LANES - QK_NOPE, QK_ROPE)]).astype(np.float32)
_Q_SEGMENT_EPS = (_Q_SEGMENT_COUNT * EPS)[None, :]


def _swap_halves(g):
    return jnp.concatenate([g[..., HALF:], g[..., :HALF]], axis=-1)


def _stacked_weights(g_mix, w_in, w_o, g_ffn, w_gate, w_up, w_down):
    depth, d_model, _ = w_in.shape
    w = (jnp.swapaxes(w_in, 1, 2) * (g_mix * d_model ** 0.5)[:, None, :]).astype(BF16)
    kr0 = Q_LORA + KV_LORA
    w_in_p = jnp.concatenate(
        [w[:, :kr0 + QK_ROPE], w[:, kr0 + HALF:kr0 + QK_ROPE], w[:, kr0:kr0 + HALF],
         jnp.zeros((depth, LANES - 2 * QK_ROPE, d_model), BF16), w[:, kr0 + QK_ROPE:]], axis=1)
    g_ffn_col = (g_ffn * d_model ** 0.5)[:, :, None]
    return {"w_in": w_in_p, "w_o": w_o.astype(BF16), "w_gate": (w_gate * g_ffn_col).astype(BF16),
            "w_up": (w_up * g_ffn_col).astype(BF16), "w_down": w_down.astype(BF16)}


def _layer_weights(l, g_q_lora, w_q_up, g_qn, g_qr, g_kv_lora, g_kr, w_kv_up, g_kn,
                   gm_ln_g, gm_ln_b, g_out_att, g_out_gm):
    wq = w_q_up[l].reshape(Q_LORA, N_HEADS, QK_HEAD)
    r1, r2 = wq[..., QK_NOPE:QK_NOPE + HALF], wq[..., QK_NOPE + HALF:]
    w_q = jnp.concatenate([wq[..., :QK_NOPE], r1, r2, r2, r1], axis=-1).reshape(Q_LORA, N_HEADS * LANES)
    w_q = w_q * (g_q_lora[l] * Q_LORA ** 0.5)[:, None]
    gq = (jnp.concatenate([g_qn[l] * g_kn[l] * QK_NOPE ** 0.5, g_qr[l], _swap_halves(g_qr[l])])
          * np.sqrt(_Q_SEGMENT_COUNT) * (QK_HEAD ** -0.5 * LOG2_E))
    gkr = jnp.concatenate([g_kr[l], _swap_halves(g_kr[l]), jnp.zeros((LANES - 2 * QK_ROPE,), F32)]) * QK_ROPE ** 0.5
    wkv = w_kv_up[l].reshape(KV_LORA, N_HEADS, QK_NOPE + V_HEAD)
    w_kvk = jnp.concatenate([wkv[..., :QK_NOPE], jnp.zeros((KV_LORA, N_HEADS, LANES - QK_NOPE), F32)],
                            axis=-1).reshape(KV_LORA, N_HEADS * LANES)
    w_kvv = wkv[..., QK_NOPE:].reshape(KV_LORA, N_HEADS * V_HEAD)
    w_kvv_t = w_kvv.T
    wk_t = jnp.transpose(wkv[..., :QK_NOPE], (1, 2, 0))
    w_abs = jnp.concatenate([wk_t, jnp.zeros((N_HEADS, LANES - QK_NOPE, KV_LORA), F32)], axis=1)
    w2 = jnp.concatenate([w_abs, jnp.broadcast_to(jnp.asarray(_ROPE_FOLD), (N_HEADS, LANES, LANES))], axis=2)
    return {
        "w_q": w_q.astype(BF16), "gq": gq[None, :],
        "g_kv_lora": g_kv_lora[l][None, :] * KV_LORA ** 0.5, "gkr": gkr[None, :],
        "w_kvk": w_kvk.astype(BF16), "w_kvv": w_kvv.astype(BF16),
        "w_kvv_t": w_kvv_t.astype(BF16),
        "w2": w2.astype(BF16), "wk_t": wk_t.reshape(N_HEADS * QK_NOPE, KV_LORA).astype(BF16),
        "ln_g": gm_ln_g[l].reshape(1, GMLP_WIDTH) * GMLP_GROUP_DIM ** 0.5, "ln_b": gm_ln_b[l].reshape(1, GMLP_WIDTH),
        "g_out_att": g_out_att[l].reshape(1, N_HEADS * V_HEAD),
        "g_out_gm": g_out_gm[l].reshape(1, GMLP_WIDTH) * GMLP_GROUP_DIM ** 0.5,
    }


def _chunk_bias(qpos, kpos):
    vis = (kpos[None, :] // CHUNK) <= (qpos[:, None] // CHUNK)
    return np.where(vis, 0.0, NEG_INF).astype(np.float32)


def kernel(x_prompt, x_sample, cache_ckv, cache_krope, g_mix, w_in, g_q_lora, w_q_up, g_qn, g_qr, g_kv_lora, g_kr,
           w_kv_up, g_kn, gm_ln_g, gm_ln_b, gm_w_s, gm_b_s, g_out_att, g_out_gm, w_o, g_ffn, w_gate, w_up, w_down):
    batch, seq, d = x_prompt.shape
    dbatch, dseq, _ = x_sample.shape
    depth, _, past, _ = cache_ckv.shape
    assert seq % GMLP_CHUNK == 0 and GMLP_CHUNK % dseq == 0 and (dbatch * dseq) % GMLP_CHUNK == 0

    tq_attn = 256
    tm_in = 1024
    tm_ffn = 1024
    tm_s = dbatch * dseq

    tq_p, tk_p = (jnp.asarray(t) for t in _rope_tables(np.arange(seq)))
    tq_s, tk_s = (jnp.asarray(np.tile(t, (dbatch, 1))) for t in _rope_tables(past + np.arange(dseq)))
    bias_p = jnp.asarray(_chunk_bias(np.arange(tq_attn), np.arange(tq_attn)))
    bias_s = np.full((dseq, LANES), NEG_INF, np.float32)
    bias_s[:, :dseq] = _chunk_bias(past + np.arange(dseq), past + np.arange(dseq))
    bias_s = jnp.asarray(np.tile(bias_s, (N_HEADS, 1)))
    reps = GMLP_CHUNK // dseq

    yp = x_prompt.reshape(batch * seq, d)
    ys = x_sample.reshape(dbatch * dseq, d)
    ckv_p, kr_p, ckv_s, kr_s, gv_s = [], [], [], [], []
    big = _stacked_weights(g_mix, w_in, w_o, g_ffn, w_gate, w_up, w_down)
    kr_cache_t = jnp.swapaxes(cache_krope, 2, 3)
    for l in range(depth):
        lw = _layer_weights(l, g_q_lora, w_q_up, g_qn, g_qr, g_kv_lora, g_kr, w_kv_up, g_kn,
                            gm_ln_g, gm_ln_b, g_out_att, g_out_gm)
        ws_p = gm_w_s[l]
        bs_p = jnp.repeat(gm_b_s[l].T, GMLP_GROUP_DIM, axis=1)
        ws_s = jnp.tile(gm_w_s[l][:, :dseq, :dseq], (1, reps, reps))
        bs_s = jnp.tile(jnp.repeat(gm_b_s[l][:, :dseq].T, GMLP_GROUP_DIM, axis=1), (reps, 1))

        q, k, vt, c, r, ogm = _inproj(yp, big, lw, tq_p, tk_p, ws_p, bs_p, layer=l, tm=tm_in, table_blocks=seq // tm_in,
                                      ws_mask="causal", want_vn=False, v_transposed=True)
        g_t = jnp.broadcast_to(lw["g_out_att"].reshape(N_HEADS * V_HEAD, 1), (N_HEADS * V_HEAD, tq_attn))
        oatt = _attn_prompt(q, k, vt, bias_p.T, g_t, batch=batch, seq=seq, tq=tq_attn)
        yp = _ffn(yp, oatt, ogm, big, layer=l, tm=tm_ffn, fchunk=256)
        ckv_p.append(c.reshape(batch, seq, KV_LORA))
        kr_p.append(jnp.transpose(r.reshape(QK_ROPE, batch, seq), (1, 2, 0)))

        q, k, v, c, r, ogm, vn = _inproj(ys, big, lw, tq_s, tk_s, ws_s, bs_s, layer=l, tm=tm_s, table_blocks=1,
                                         ws_mask="blockdiag", want_vn=True, v_transposed=False)
        oatt = _attn_sample(q, c, r, cache_ckv, kr_cache_t, lw["w2"], lw["wk_t"], lw["w_kvv"], bias_s,
                            lw["g_out_att"], layer=l, batch=dbatch, seq=dseq)
        ys = _ffn(ys, oatt, ogm, big, layer=l, tm=tm_s, fchunk=256)
        ckv_s.append(c.reshape(dbatch, dseq, KV_LORA))
        kr_s.append(r.reshape(dbatch, dseq, QK_ROPE))
        gv_s.append(vn.reshape(dbatch, dseq, GMLP_WIDTH))

    return (yp.reshape(batch, seq, d), ys.reshape(dbatch, dseq, d), jnp.stack(ckv_p), jnp.stack(kr_p),
            jnp.stack(ckv_s), jnp.stack(kr_s), jnp.stack(gv_s))
```

```python
import functools

import numpy as np
import jax
import jax.numpy as jnp
from jax import lax
from jax.experimental import pallas as pl
from jax.experimental.pallas import tpu as pltpu

F32 = jnp.float32
BF16 = jnp.bfloat16

CHUNK = 64
N_HEADS = 8
QK_NOPE = 64
QK_ROPE = 32
QK_HEAD = QK_NOPE + QK_ROPE
V_HEAD = 64
Q_LORA = 256
KV_LORA = 128
ROPE_THETA = 10000.0
GMLP_GROUPS = 8
GMLP_GROUP_DIM = 64
GMLP_WIDTH = GMLP_GROUPS * GMLP_GROUP_DIM
GMLP_CHUNK = 128
EPS = 1e-6
NEG_INF = -1e30
LOG2_E = 1.4426950408889634

LANES = 128
BF16_SUBLANES = 16
V7X_VMEM_BYTES = 64 * 1024 * 1024
VMEM_LIMIT_BYTES = (V7X_VMEM_BYTES * 3) // 4

HALF = QK_ROPE // 2
V_ROWS = V_HEAD + BF16_SUBLANES
IN_COLS_PADDED = Q_LORA + KV_LORA + LANES + 2 * GMLP_WIDTH
COL_CKV = Q_LORA
COL_KR = Q_LORA + KV_LORA
COL_U = COL_KR + LANES
COL_V = COL_U + GMLP_WIDTH

_NT = (((1,), (1,)), ((), ()))


def _dot(a, b):
    return jnp.dot(a, b, preferred_element_type=F32)


def _dot_nt(a, b):
    return lax.dot_general(a, b, _NT, preferred_element_type=F32)


def _rsqrt_sumsq(x, n):
    return lax.rsqrt(jnp.sum(x * x, axis=-1, keepdims=True) + n * EPS)


def _half_sums(x, lo):
    s_lo = jnp.sum(jnp.where(lo, x, 0.0), axis=-1, keepdims=True)
    s_hi = jnp.sum(jnp.where(lo, 0.0, x), axis=-1, keepdims=True)
    return jnp.where(lo, s_lo, s_hi)


def _half_rms_scale(x, lo):
    return lax.rsqrt(_half_sums(x * x, lo) * (1.0 / 64) + EPS)


def _half_rsqrt_sumsq(x, lo):
    return lax.rsqrt(_half_sums(x * x, lo) + 64 * EPS)


def _gelu_tanh(t):
    c = 0.7978845608028654
    h = 0.5 * t
    return h + h * jnp.tanh(t * (c + (c * 0.044715) * (t * t)))


INPROJ_SUB_ROWS = 256


def _inproj_kernel(x_ref, win_ref, wq_ref, gq_ref, tq_ref, gkv_ref, gkr_ref, tk_ref,
                   wkk_ref, wkv_ref, lng_ref, lnb_ref, ws_ref, bs_ref, ggm_ref, seg_ref, epsq_ref,
                   q_out, k_out, v_out, ckv_out, kr_out, ogm_out, *rest, ws_mask, v_transposed):
    *vn_out, win_scr = rest
    tm = x_ref.shape[0]
    sub = min(tm, INPROJ_SUB_ROWS)

    @pl.when(pl.program_id(0) == 0)
    def _():
        win_scr[...] = win_ref[...].T

    lane = lax.broadcasted_iota(jnp.int32, (sub, LANES), 1)
    lo = lane < 64
    lo_c = lax.broadcasted_iota(jnp.int32, (GMLP_CHUNK, LANES), 1) < 64
    row = lax.broadcasted_iota(jnp.int32, (GMLP_CHUNK, GMLP_CHUNK), 0) // CHUNK
    col = lax.broadcasted_iota(jnp.int32, (GMLP_CHUNK, GMLP_CHUNK), 1) // CHUNK
    keep = (row >= col) if ws_mask == "causal" else (row == col)

    def project(r0):
        x = x_ref[r0:r0 + sub, :]
        h = (x * _rsqrt_sumsq(x, x.shape[1])).astype(BF16)
        return _dot(h, win_scr[...])

    def finish(z, r0):
        rows = slice(r0, r0 + sub)
        ql = z[:, 0:Q_LORA]
        qn = (ql * _rsqrt_sumsq(ql, Q_LORA)).astype(BF16)
        qz = _dot(qn, wq_ref[...])
        tq = tq_ref[rows, :] * gq_ref[...]
        for hd in range(N_HEADS):
            cols = slice(hd * LANES, (hd + 1) * LANES)
            xh = qz[:, cols]
            ss = _dot((xh * xh).astype(BF16), seg_ref[...])
            q_out[rows, cols] = (xh * lax.rsqrt(ss + epsq_ref[...]) * tq).astype(BF16)

        y0 = z[:, COL_KR:COL_KR + LANES]
        s_k = jnp.sum(jnp.where(lane < QK_ROPE, y0 * y0, 0.0), axis=-1, keepdims=True)
        y = y0 * lax.rsqrt(s_k + QK_ROPE * EPS) * (gkr_ref[...] * tk_ref[rows, :])
        r64 = pltpu.roll(y, 64, 1)
        r32 = pltpu.roll(y, 32, 1)
        r96 = pltpu.roll(y, 96, 1)
        if v_transposed:
            kr_out[:, rows] = (y + r96).T[0:QK_ROPE, :]
        else:
            kr_out[rows, :] = (y + r96)[:, 0:QK_ROPE]
        kr_hi = jnp.where(lo, 0.0, r64 + r32 + r96)

        c0 = z[:, COL_CKV:COL_CKV + KV_LORA]
        cn = c0 * _rsqrt_sumsq(c0, KV_LORA) * gkv_ref[...]
        ckv_out[rows, :] = cn
        cb = cn.astype(BF16)
        kz = _dot(cb, wkk_ref[...])
        for hd in range(N_HEADS):
            cols = slice(hd * LANES, (hd + 1) * LANES)
            xh = kz[:, cols]
            kn = xh * _rsqrt_sumsq(xh, QK_NOPE)
            k_out[rows, cols] = jnp.where(lo, kn, kr_hi).astype(BF16)
        if v_transposed:
            vt = _dot_nt(wkv_ref[...], cb).astype(BF16)
            tail = jnp.where(lax.broadcasted_iota(jnp.int32, (BF16_SUBLANES, sub), 0) == 0, 1.0, 0.0).astype(BF16)
            for hd in range(N_HEADS):
                v_out[hd * V_ROWS:hd * V_ROWS + V_HEAD, rows] = vt[hd * V_HEAD:(hd + 1) * V_HEAD]
                v_out[hd * V_ROWS + V_HEAD:(hd + 1) * V_ROWS, rows] = tail
        else:
            v_out[rows, :] = _dot(cb, wkv_ref[...]).astype(BF16)

        for p in range(GMLP_GROUPS // 2):
            sl = slice(p * LANES, (p + 1) * LANES)
            w_pair = jnp.concatenate(
                [jnp.where(keep, ws_ref[2 * p], 0.0), jnp.where(keep, ws_ref[2 * p + 1], 0.0)], axis=0).astype(BF16)
            ug = _gelu_tanh(z[:, COL_U + p * LANES:COL_U + (p + 1) * LANES])
            vg = _gelu_tanh(z[:, COL_V + p * LANES:COL_V + (p + 1) * LANES])
            xc = vg - _half_sums(vg, lo) * (1.0 / GMLP_GROUP_DIM)
            vn = xc * _half_rsqrt_sumsq(xc, lo) * lng_ref[:, sl] + lnb_ref[:, sl]
            if vn_out:
                vn_out[0][rows, sl] = vn
            vnb = vn.astype(BF16)
            for c in range(sub // GMLP_CHUNK):
                crows = slice(c * GMLP_CHUNK, (c + 1) * GMLP_CHUNK)
                res = _dot(w_pair, vnb[crows])
                mixed = jnp.where(lo_c, res[0:GMLP_CHUNK], res[GMLP_CHUNK:]) + bs_ref[:, sl]
                og = ug[crows] * mixed
                ogm_out[r0 + c * GMLP_CHUNK:r0 + (c + 1) * GMLP_CHUNK, sl] = (
                    og * _half_rsqrt_sumsq(og, lo_c) * ggm_ref[:, sl]).astype(BF16)

    starts = list(range(0, tm, sub))
    z_next = project(starts[0])
    for i, r0 in enumerate(starts):
        z_cur = z_next
        if i + 1 < len(starts):
            z_next = project(starts[i + 1])
        finish(z_cur, r0)


def _const_spec(shape):
    return pl.BlockSpec(shape, lambda *_: (0,) * len(shape), pipeline_mode=pl.Buffered(1))


def _layer_spec(stacked, layer):
    zeros = (0,) * (stacked.ndim - 1)
    return pl.BlockSpec((None,) + stacked.shape[1:], lambda *_: (layer,) + zeros, pipeline_mode=pl.Buffered(1))


def _inproj(x, big, lw, tq, tk, ws, bs, *, layer, tm, table_blocks, ws_mask, want_vn, v_transposed):
    n, d = x.shape
    w_v = lw["w_kvv_t"] if v_transposed else lw["w_kvv"]
    grid = (n // tm,)
    tbl = lambda i: (i % table_blocks, 0)
    row = lambda i: (i, 0)
    layer_args = [big["w_in"], lw["w_q"], lw["gq"]]
    layer_args2 = [lw["g_kv_lora"], lw["gkr"]]
    layer_args3 = [lw["w_kvk"], w_v, lw["ln_g"], lw["ln_b"], ws, bs, lw["g_out_gm"]]
    in_specs = (
        [pl.BlockSpec((tm, d), row)] + [_layer_spec(a, layer) for a in layer_args]
        + [pl.BlockSpec((tm, LANES), tbl)] + [_layer_spec(a, layer) for a in layer_args2]
        + [pl.BlockSpec((tm, LANES), tbl)] + [_layer_spec(a, layer) for a in layer_args3]
        + [_const_spec((LANES, LANES)), _const_spec((1, LANES))])
    out_shape = [
        jax.ShapeDtypeStruct((n, N_HEADS * LANES), BF16),
        jax.ShapeDtypeStruct((n, N_HEADS * LANES), BF16),
        (jax.ShapeDtypeStruct((N_HEADS * V_ROWS, n), BF16) if v_transposed
         else jax.ShapeDtypeStruct((n, N_HEADS * V_HEAD), BF16)),
        jax.ShapeDtypeStruct((n, KV_LORA), F32),
        (jax.ShapeDtypeStruct((QK_ROPE, n), F32) if v_transposed
         else jax.ShapeDtypeStruct((n, QK_ROPE), F32)),
        jax.ShapeDtypeStruct((n, GMLP_WIDTH), BF16),
    ]
    out_specs = [
        pl.BlockSpec((tm, N_HEADS * LANES), row),
        pl.BlockSpec((tm, N_HEADS * LANES), row),
        (pl.BlockSpec((N_HEADS * V_ROWS, tm), lambda i: (0, i)) if v_transposed
         else pl.BlockSpec((tm, N_HEADS * V_HEAD), row)),
        pl.BlockSpec((tm, KV_LORA), row),
        (pl.BlockSpec((QK_ROPE, tm), lambda i: (0, i)) if v_transposed else pl.BlockSpec((tm, QK_ROPE), row)),
        pl.BlockSpec((tm, GMLP_WIDTH), row),
    ]
    if want_vn:
        out_shape.append(jax.ShapeDtypeStruct((n, GMLP_WIDTH), F32))
        out_specs.append(pl.BlockSpec((tm, GMLP_WIDTH), row))
    return pl.pallas_call(
        functools.partial(_inproj_kernel, ws_mask=ws_mask, v_transposed=v_transposed),
        grid=grid, in_specs=in_specs, out_specs=out_specs, out_shape=out_shape,
        scratch_shapes=[pltpu.VMEM((d, IN_COLS_PADDED), BF16)],
        compiler_params=pltpu.CompilerParams(dimension_semantics=("arbitrary",),
                                             vmem_limit_bytes=VMEM_LIMIT_BYTES),
        name="inproj_vn" if want_vn else "inproj",
    )(x, *layer_args, tq, *layer_args2, tk, *layer_args3,
      jnp.asarray(_Q_SEGMENTS, BF16), jnp.asarray(_Q_SEGMENT_EPS))


ATTN_STREAM_LAG = 18


def _attn_prompt_kernel(q_ref, k_ref, vt_ref, bias_ref, g_ref, o_ref, s_scr, *, tq):
    s_len = q_ref.shape[0]
    bodies = [(qi, hh) for qi in reversed(range(s_len // tq)) for hh in range(2)]
    bases = np.cumsum([0] + [qi + 1 for qi, _ in bodies])
    state = [{} for _ in bodies]

    def a_steps(bi):
        qi, hh = bodies[bi]
        cols = slice(hh * LANES, (hh + 1) * LANES)
        rows = slice(qi * tq, (qi + 1) * tq)
        st = state[bi]

        def chunk(j):
            s = _dot_nt(k_ref[j * tq:(j + 1) * tq, cols], q_ref[rows, cols])
            if j == qi:
                s = s + bias_ref[...]
            s_scr[bases[bi] + j] = s
            cm = jnp.max(s.reshape(tq // 8, 8, tq), axis=0)
            st["m_run"] = cm if j == 0 else jnp.maximum(st["m_run"], cm)

        def finish():
            st["m"] = jnp.max(st["m_run"], axis=0, keepdims=True)

        return [functools.partial(chunk, j) for j in range(qi + 1)] + [finish]

    def b_steps(bi):
        qi, hh = bodies[bi]
        st = state[bi]

        def chunk(j):
            p = jnp.exp2(s_scr[bases[bi] + j] - st["m"]).astype(BF16)
            d = _dot(vt_ref[hh * V_ROWS:(hh + 1) * V_ROWS, j * tq:(j + 1) * tq], p)
            st["acc"] = d if j == 0 else st["acc"] + d

        def finish():
            acc = st["acc"]
            o = acc[0:V_HEAD] * (1.0 / acc[V_HEAD:V_HEAD + 1])
            r = lax.rsqrt(jnp.sum(o * o, axis=0, keepdims=True) * (1.0 / V_HEAD) + EPS)
            st["out"] = o * r * g_ref[hh * V_HEAD:(hh + 1) * V_HEAD, :]
            if hh == 1:
                pair_t = jnp.concatenate([state[bi - 1]["out"], st["out"]], axis=0)
                o_ref[qi * tq:(qi + 1) * tq, :] = pair_t.T.astype(BF16)

        return [functools.partial(chunk, j) for j in range(qi + 1)] + [finish]

    a_all = [f for bi in range(len(bodies)) for f in a_steps(bi)]
    b_all = [f for bi in range(len(bodies)) for f in b_steps(bi)]
    for i in range(len(a_all) + ATTN_STREAM_LAG):
        if i < len(a_all):
            a_all[i]()
        if i >= ATTN_STREAM_LAG:
            b_all[i - ATTN_STREAM_LAG]()


def _attn_prompt(q, k, vt, bias_t, g_t, *, layer, batch, seq, tq):
    n = q.shape[0]
    pairs = N_HEADS // 2
    nq = seq // tq
    assert ATTN_STREAM_LAG > nq + 1
    return pl.pallas_call(
        functools.partial(_attn_prompt_kernel, tq=tq),
        grid=(batch, pairs),
        in_specs=[
            pl.BlockSpec((seq, 2 * LANES), lambda b, p: (b, p)),
            pl.BlockSpec((seq, 2 * LANES), lambda b, p: (b, p)),
            pl.BlockSpec((2 * V_ROWS, seq), lambda b, p: (p, b)),
            pl.BlockSpec((tq, tq), lambda b, p: (0, 0)),
            pl.BlockSpec((None, 2 * V_HEAD, tq), lambda b, p: (layer, p, 0)),
        ],
        out_specs=pl.BlockSpec((seq, 2 * V_HEAD), lambda b, p: (b, p)),
        out_shape=jax.ShapeDtypeStruct((n, N_HEADS * V_HEAD), BF16),
        scratch_shapes=[pltpu.VMEM((nq * (nq + 1), tq, tq), F32)],
        compiler_params=pltpu.CompilerParams(dimension_semantics=("arbitrary", "arbitrary"),
                                             vmem_limit_bytes=VMEM_LIMIT_BYTES),
        name="attn_prompt",
    )(q, k, vt, bias_t, g_t)


SAMPLE_KEY_BLOCK = 1024


def _attn_sample_kernel(q_ref, cnew_ref, krnew_ref, ckv_ref, krt_ref, w2_ref, wkt_ref, wv_ref, bias_ref, gout_ref,
                        o_ref, kr_scr, *, kblk):
    past = ckv_ref.shape[0]
    sq = q_ref.shape[0]
    q2 = jnp.concatenate([_dot(q_ref[:, h * LANES:(h + 1) * LANES], w2_ref[h]) for h in range(N_HEADS)],
                         axis=0).astype(BF16)
    qt = q2[:, 0:KV_LORA]
    qr = q2[:, KV_LORA:]
    kr_t = jnp.concatenate([krt_ref[...].astype(BF16), jnp.zeros((LANES - QK_ROPE, past), BF16)], axis=0)
    kr_scr[...] = jnp.zeros(kr_scr.shape, F32)
    kr_scr[0:sq, 0:QK_ROPE] = krnew_ref[...]
    ones_blk = jnp.ones((kblk, LANES), BF16)

    def scores(cb, s2):
        kzt = _dot_nt(wkt_ref[...], cb)
        s1 = _dot_nt(qt, cb)
        out = []
        for h in range(N_HEADS):
            kh = kzt[h * QK_NOPE:(h + 1) * QK_NOPE]
            r = lax.rsqrt(jnp.sum(kh * kh, axis=0, keepdims=True) + QK_NOPE * EPS)
            out.append(s1[h * sq:(h + 1) * sq] * r + s2[h * sq:(h + 1) * sq])
        return jnp.concatenate(out, axis=0)

    cn = jnp.concatenate([cnew_ref[...], jnp.zeros((LANES - sq, KV_LORA), F32)], axis=0).astype(BF16)
    blocks = [(ckv_ref[j * kblk:(j + 1) * kblk, :], kr_t[:, j * kblk:(j + 1) * kblk]) for j in range(past // kblk)]

    def block_scores(j):
        if j < len(blocks):
            cb = blocks[j][0].astype(BF16)
            return scores(cb, _dot(qr, blocks[j][1])), jnp.concatenate([cb, ones_blk], axis=1)
        s_new = scores(cn, _dot_nt(qr, kr_scr[...].astype(BF16))) + bias_ref[...]
        return s_new, jnp.concatenate([cn, ones_blk[0:LANES]], axis=1)

    m = acc = None
    nxt = block_scores(0)
    for j in range(len(blocks) + 1):
        s, vals = nxt
        if j < len(blocks):
            nxt = block_scores(j + 1)
        m_blk = jnp.max(s, axis=-1, keepdims=True)
        m_new = m_blk if m is None else jnp.maximum(m, m_blk)
        d = _dot(jnp.exp2(s - m_new).astype(BF16), vals)
        acc = d if acc is None else acc * jnp.exp2(m - m_new) + d
        m = m_new
    o_lat = (acc[:, 0:KV_LORA] * (1.0 / acc[:, KV_LORA:])).astype(BF16)
    o_full = _dot(o_lat, wv_ref[...])
    head = lax.broadcasted_iota(jnp.int32, (sq, N_HEADS * V_HEAD), 1) >> 6
    o = jnp.zeros((sq, N_HEADS * V_HEAD), F32)
    for h in range(N_HEADS):
        o = jnp.where(head == h, o_full[h * sq:(h + 1) * sq], o)
    lo = lax.broadcasted_iota(jnp.int32, (sq, LANES), 1) < 64
    for c in range(0, N_HEADS * V_HEAD, LANES):
        oc = o[:, c:c + LANES]
        o_ref[:, c:c + LANES] = (oc * _half_rms_scale(oc, lo) * gout_ref[:, c:c + LANES]).astype(BF16)


def _attn_sample(q, cnew, krnew, cache_ckv, kr_cache_t, w2, wkt, w_kvv, bias, gout, *, layer, batch, seq):
    n = q.shape[0]
    past = cache_ckv.shape[2]
    assert V_HEAD == 64 and past % SAMPLE_KEY_BLOCK == 0 and seq <= LANES
    return pl.pallas_call(
        functools.partial(_attn_sample_kernel, kblk=SAMPLE_KEY_BLOCK),
        grid=(batch,),
        in_specs=[
            pl.BlockSpec((seq, N_HEADS * LANES), lambda b: (b, 0)),
            pl.BlockSpec((seq, KV_LORA), lambda b: (b, 0)),
            pl.BlockSpec((seq, QK_ROPE), lambda b: (b, 0)),
            pl.BlockSpec((None, None, past, KV_LORA), lambda b: (layer, b, 0, 0)),
            pl.BlockSpec((None, None, QK_ROPE, past), lambda b: (layer, b, 0, 0)),
            _layer_spec(w2, layer),
            _layer_spec(wkt, layer),
            _layer_spec(w_kvv, layer),
            pl.BlockSpec((N_HEADS * seq, LANES), lambda b: (0, 0)),
            _layer_spec(gout, layer),
        ],
        out_specs=pl.BlockSpec((seq, N_HEADS * V_HEAD), lambda b: (b, 0)),
        out_shape=jax.ShapeDtypeStruct((n, N_HEADS * V_HEAD), BF16),
        scratch_shapes=[pltpu.VMEM((LANES, LANES), F32)],
        compiler_params=pltpu.CompilerParams(dimension_semantics=("arbitrary",),
                                             vmem_limit_bytes=VMEM_LIMIT_BYTES),
        name="attn_sample",
    )(q, cnew, krnew, cache_ckv, kr_cache_t, w2, wkt, w_kvv, bias, gout)


def _ffn_kernel(x_ref, oa_ref, og_ref, wo_ref, wg_ref, wu_ref, wd_ref, y_ref, *, fchunk):
    aw = oa_ref.shape[1]
    x1 = x_ref[...] + _dot(oa_ref[...], wo_ref[0:aw, :]) + _dot(og_ref[...], wo_ref[aw:, :])
    h2 = (x1 * _rsqrt_sumsq(x1, x1.shape[1])).astype(BF16)
    acc = None
    for c in range(0, wg_ref.shape[1], fchunk):
        g = _dot(h2, wg_ref[:, c:c + fchunk])
        u = _dot(h2, wu_ref[:, c:c + fchunk])
        a = (g * jax.nn.sigmoid(g) * u).astype(BF16)
        d = _dot(a, wd_ref[c:c + fchunk, :])
        acc = d if acc is None else acc + d
    y_ref[...] = x1 + acc


def _ffn(x, oa, og, big, *, layer, tm, fchunk):
    n, d = x.shape
    row = lambda i: (i, 0)
    return pl.pallas_call(
        functools.partial(_ffn_kernel, fchunk=fchunk),
        grid=(n // tm,),
        in_specs=[
            pl.BlockSpec((tm, d), row),
            pl.BlockSpec((tm, oa.shape[1]), row),
            pl.BlockSpec((tm, og.shape[1]), row),
            _layer_spec(big["w_o"], layer),
            _layer_spec(big["w_gate"], layer),
            _layer_spec(big["w_up"], layer),
            _layer_spec(big["w_down"], layer),
        ],
        out_specs=pl.BlockSpec((tm, d), row),
        out_shape=jax.ShapeDtypeStruct((n, d), F32),
        compiler_params=pltpu.CompilerParams(dimension_semantics=("arbitrary",),
                                             vmem_limit_bytes=VMEM_LIMIT_BYTES),
        name="outproj_ffn",
    )(x, oa, og, big["w_o"], big["w_gate"], big["w_up"], big["w_down"])


def _rope_tables(pos):
    inv = ROPE_THETA ** (-np.arange(HALF, dtype=np.float64) / HALF)
    ang = np.asarray(pos, np.float64)[:, None] * inv[None, :]
    cos, sin = np.cos(ang), np.sin(ang)
    n = ang.shape[0]
    tq = np.concatenate([np.ones((n, QK_NOPE)), cos, cos, -sin, sin], axis=1)
    tk = np.concatenate([cos, cos, -sin, sin, np.zeros((n, LANES - 2 * QK_ROPE))], axis=1)
    return tq.astype(np.float32), tk.astype(np.float32)


def _rope_fold_matrix():
    f = np.zeros((LANES, LANES), np.float32)
    for i in range(QK_ROPE):
        f[QK_NOPE + i, i] = 1.0
        f[QK_NOPE + QK_ROPE + i, i] = 1.0
    return f


_ROPE_FOLD = _rope_fold_matrix()


def _q_segment_matrix():
    m = np.zeros((LANES, LANES), np.float32)
    m[:QK_NOPE, :QK_NOPE] = 1.0
    m[QK_NOPE:QK_NOPE + QK_ROPE, QK_NOPE:] = 1.0
    return m


_Q_SEGMENTS = _q_segment_matrix()
_Q_SEGMENT_COUNT = np.concatenate([np.full(QK_NOPE, QK_NOPE), np.full(LANES - QK_NOPE, QK_ROPE)]).astype(np.float32)
_Q_SEGMENT_EPS = (_Q_SEGMENT_COUNT * EPS)[None, :]


def _swap_halves(g):
    return jnp.concatenate([g[..., HALF:], g[..., :HALF]], axis=-1)


def _stacked_weights(g_mix, w_in, w_o, g_ffn, w_gate, w_up, w_down):
    depth, d_model, _ = w_in.shape
    w = (jnp.swapaxes(w_in, 1, 2) * (g_mix * d_model ** 0.5)[:, None, :]).astype(BF16)
    kr0 = Q_LORA + KV_LORA
    w_in_p = jnp.concatenate(
        [w[:, :kr0 + QK_ROPE], w[:, kr0 + HALF:kr0 + QK_ROPE], w[:, kr0:kr0 + HALF],
         jnp.zeros((depth, LANES - 2 * QK_ROPE, d_model), BF16), w[:, kr0 + QK_ROPE:]], axis=1)
    g_ffn_col = (g_ffn * d_model ** 0.5)[:, :, None]
    return {"w_in": w_in_p, "w_o": w_o.astype(BF16), "w_gate": (w_gate * g_ffn_col).astype(BF16),
            "w_up": (w_up * g_ffn_col).astype(BF16), "w_down": w_down.astype(BF16)}


def _small_weights(g_q_lora, w_q_up, g_qn, g_qr, g_kv_lora, g_kr, w_kv_up, g_kn, gm_ln_g, gm_ln_b, g_out_att,
                   g_out_gm, tq_attn):
    depth = w_q_up.shape[0]
    wq = w_q_up.reshape(depth, Q_LORA, N_HEADS, QK_HEAD)
    r1, r2 = wq[..., QK_NOPE:QK_NOPE + HALF], wq[..., QK_NOPE + HALF:]
    w_q = jnp.concatenate([wq[..., :QK_NOPE], r1, r2, r2, r1], axis=-1).reshape(depth, Q_LORA, N_HEADS * LANES)
    w_q = w_q * (g_q_lora * Q_LORA ** 0.5)[:, :, None]
    gq = (jnp.concatenate([g_qn * g_kn * QK_NOPE ** 0.5, g_qr, _swap_halves(g_qr)], axis=-1)
          * np.sqrt(_Q_SEGMENT_COUNT) * (QK_HEAD ** -0.5 * LOG2_E))
    gkr = jnp.concatenate([g_kr, _swap_halves(g_kr), jnp.zeros((depth, LANES - 2 * QK_ROPE), F32)],
                          axis=-1) * QK_ROPE ** 0.5
    wkv = w_kv_up.reshape(depth, KV_LORA, N_HEADS, QK_NOPE + V_HEAD)
    w_kvk = jnp.concatenate([wkv[..., :QK_NOPE], jnp.zeros((depth, KV_LORA, N_HEADS, LANES - QK_NOPE), F32)],
                            axis=-1).reshape(depth, KV_LORA, N_HEADS * LANES)
    w_kvv = wkv[..., QK_NOPE:].reshape(depth, KV_LORA, N_HEADS * V_HEAD)
    wk_t = jnp.transpose(wkv[..., :QK_NOPE], (0, 2, 3, 1))
    w_abs = jnp.concatenate([wk_t, jnp.zeros((depth, N_HEADS, LANES - QK_NOPE, KV_LORA), F32)], axis=2)
    w2 = jnp.concatenate([w_abs, jnp.broadcast_to(jnp.asarray(_ROPE_FOLD), (depth, N_HEADS, LANES, LANES))], axis=3)
    g_att = g_out_att.reshape(depth, N_HEADS * V_HEAD)
    return {
        "w_q": w_q.astype(BF16), "gq": gq[:, None, :],
        "g_kv_lora": g_kv_lora[:, None, :] * KV_LORA ** 0.5, "gkr": gkr[:, None, :],
        "w_kvk": w_kvk.astype(BF16), "w_kvv": w_kvv.astype(BF16),
        "w_kvv_t": jnp.swapaxes(w_kvv, 1, 2).astype(BF16),
        "w2": w2.astype(BF16), "wk_t": wk_t.reshape(depth, N_HEADS * QK_NOPE, KV_LORA).astype(BF16),
        "ln_g": gm_ln_g.reshape(depth, 1, GMLP_WIDTH) * GMLP_GROUP_DIM ** 0.5,
        "ln_b": gm_ln_b.reshape(depth, 1, GMLP_WIDTH),
        "g_out_att": g_att[:, None, :],
        "g_out_att_t": jnp.broadcast_to(g_att[:, :, None], (depth, N_HEADS * V_HEAD, tq_attn)),
        "g_out_gm": g_out_gm.reshape(depth, 1, GMLP_WIDTH) * GMLP_GROUP_DIM ** 0.5,
    }


def _chunk_bias(qpos, kpos):
    vis = (kpos[None, :] // CHUNK) <= (qpos[:, None] // CHUNK)
    return np.where(vis, 0.0, NEG_INF).astype(np.float32)


def kernel(x_prompt, x_sample, cache_ckv, cache_krope, g_mix, w_in, g_q_lora, w_q_up, g_qn, g_qr, g_kv_lora, g_kr,
           w_kv_up, g_kn, gm_ln_g, gm_ln_b, gm_w_s, gm_b_s, g_out_att, g_out_gm, w_o, g_ffn, w_gate, w_up, w_down):
    batch, seq, d = x_prompt.shape
    dbatch, dseq, _ = x_sample.shape
    depth, _, past, _ = cache_ckv.shape
    assert seq % GMLP_CHUNK == 0 and GMLP_CHUNK % dseq == 0 and (dbatch * dseq) % GMLP_CHUNK == 0

    tq_attn = 256
    tm_in = 1024
    tm_ffn = 1024
    tm_s = dbatch * dseq

    tq_p, tk_p = (jnp.asarray(t) for t in _rope_tables(np.arange(seq)))
    tq_s, tk_s = (jnp.asarray(np.tile(t, (dbatch, 1))) for t in _rope_tables(past + np.arange(dseq)))
    bias_p = jnp.asarray(_chunk_bias(np.arange(tq_attn), np.arange(tq_attn)))
    bias_s = np.full((dseq, LANES), NEG_INF, np.float32)
    bias_s[:, :dseq] = _chunk_bias(past + np.arange(dseq), past + np.arange(dseq))
    bias_s = jnp.asarray(np.tile(bias_s, (N_HEADS, 1)))
    reps = GMLP_CHUNK // dseq

    yp = x_prompt.reshape(batch * seq, d)
    ys = x_sample.reshape(dbatch * dseq, d)
    ckv_p, kr_p, ckv_s, kr_s, gv_s = [], [], [], [], []
    big = _stacked_weights(g_mix, w_in, w_o, g_ffn, w_gate, w_up, w_down)
    lw = _small_weights(g_q_lora, w_q_up, g_qn, g_qr, g_kv_lora, g_kr, w_kv_up, g_kn, gm_ln_g, gm_ln_b, g_out_att,
                        g_out_gm, tq_attn)
    kr_cache_t = jnp.swapaxes(cache_krope, 2, 3)
    ws_p = gm_w_s
    bs_p = jnp.repeat(jnp.swapaxes(gm_b_s, 1, 2), GMLP_GROUP_DIM, axis=2)
    ws_s = jnp.tile(gm_w_s[:, :, :dseq, :dseq], (1, 1, reps, reps))
    bs_s = jnp.tile(jnp.repeat(jnp.swapaxes(gm_b_s[:, :, :dseq], 1, 2), GMLP_GROUP_DIM, axis=2), (1, reps, 1))
    for l in range(depth):
        q, k, vt, c, r, ogm = _inproj(yp, big, lw, tq_p, tk_p, ws_p, bs_p, layer=l, tm=tm_in, table_blocks=seq // tm_in,
                                      ws_mask="causal", want_vn=False, v_transposed=True)
        oatt = _attn_prompt(q, k, vt, bias_p.T, lw["g_out_att_t"], layer=l, batch=batch, seq=seq, tq=tq_attn)
        yp = _ffn(yp, oatt, ogm, big, layer=l, tm=tm_ffn, fchunk=256)
        ckv_p.append(c.reshape(batch, seq, KV_LORA))
        kr_p.append(jnp.transpose(r.reshape(QK_ROPE, batch, seq), (1, 2, 0)))

        q, k, v, c, r, ogm, vn = _inproj(ys, big, lw, tq_s, tk_s, ws_s, bs_s, layer=l, tm=tm_s, table_blocks=1,
                                         ws_mask="blockdiag", want_vn=True, v_transposed=False)
        oatt = _attn_sample(q, c, r, cache_ckv, kr_cache_t, lw["w2"], lw["wk_t"], lw["w_kvv"], bias_s,
                            lw["g_out_att"], layer=l, batch=dbatch, seq=dseq)
        ys = _ffn(ys, oatt, ogm, big, layer=l, tm=tm_s, fchunk=256)
        ckv_s.append(c.reshape(dbatch, dseq, KV_LORA))
        kr_s.append(r.reshape(dbatch, dseq, QK_ROPE))
        gv_s.append(vn.reshape(dbatch, dseq, GMLP_WIDTH))

    return (yp.reshape(batch, seq, d), ys.reshape(dbatch, dseq, d), jnp.stack(ckv_p), jnp.stack(kr_p),
            jnp.stack(ckv_s), jnp.stack(kr_s), jnp.stack(gv_s))
```

```python
import functools

import numpy as np
import jax
import jax.numpy as jnp
from jax import lax
from jax.experimental import pallas as pl
from jax.experimental.pallas import tpu as pltpu

F32 = jnp.float32
BF16 = jnp.bfloat16

CHUNK = 64
N_HEADS = 8
QK_NOPE = 64
QK_ROPE = 32
QK_HEAD = QK_NOPE + QK_ROPE
V_HEAD = 64
Q_LORA = 256
KV_LORA = 128
ROPE_THETA = 10000.0
GMLP_GROUPS = 8
GMLP_GROUP_DIM = 64
GMLP_WIDTH = GMLP_GROUPS * GMLP_GROUP_DIM
GMLP_CHUNK = 128
EPS = 1e-6
NEG_INF = -1e30
LOG2_E = 1.4426950408889634

LANES = 128
BF16_SUBLANES = 16
V7X_VMEM_BYTES = 64 * 1024 * 1024
VMEM_LIMIT_BYTES = (V7X_VMEM_BYTES * 3) // 4

HALF = QK_ROPE // 2
V_ROWS = V_HEAD + BF16_SUBLANES
IN_COLS_PADDED = Q_LORA + KV_LORA + LANES + 2 * GMLP_WIDTH
COL_CKV = Q_LORA
COL_KR = Q_LORA + KV_LORA
COL_U = COL_KR + LANES
COL_V = COL_U + GMLP_WIDTH

_NT = (((1,), (1,)), ((), ()))


def _dot(a, b):
    return jnp.dot(a, b, preferred_element_type=F32)


def _dot_nt(a, b):
    return lax.dot_general(a, b, _NT, preferred_element_type=F32)


def _rsqrt_sumsq(x, n):
    return lax.rsqrt(jnp.sum(x * x, axis=-1, keepdims=True) + n * EPS)


def _half_sums(x, lo):
    s_lo = jnp.sum(jnp.where(lo, x, 0.0), axis=-1, keepdims=True)
    s_hi = jnp.sum(jnp.where(lo, 0.0, x), axis=-1, keepdims=True)
    return jnp.where(lo, s_lo, s_hi)


def _half_rms_scale(x, lo):
    return lax.rsqrt(_half_sums(x * x, lo) * (1.0 / 64) + EPS)


def _half_rsqrt_sumsq(x, lo):
    return lax.rsqrt(_half_sums(x * x, lo) + 64 * EPS)


def _gelu_tanh(t):
    c = 0.7978845608028654
    h = 0.5 * t
    return h + h * jnp.tanh(t * (c + (c * 0.044715) * (t * t)))


INPROJ_SUB_ROWS = 256


def _inproj_kernel(x_ref, win_ref, wq_ref, gq_ref, tq_ref, gkv_ref, gkr_ref, tk_ref,
                   wkk_ref, wkv_ref, lng_ref, lnb_ref, ws_ref, bs_ref, ggm_ref, seg_ref, epsq_ref,
                   q_out, k_out, v_out, ckv_out, kr_out, ogm_out, *rest, ws_mask, v_transposed):
    *vn_out, win_scr = rest
    tm = x_ref.shape[0]
    sub = min(tm, INPROJ_SUB_ROWS)

    @pl.when(pl.program_id(0) == 0)
    def _():
        win_scr[...] = win_ref[...].T

    lane = lax.broadcasted_iota(jnp.int32, (sub, LANES), 1)
    lo = lane < 64
    lo_c = lax.broadcasted_iota(jnp.int32, (GMLP_CHUNK, LANES), 1) < 64
    row = lax.broadcasted_iota(jnp.int32, (GMLP_CHUNK, GMLP_CHUNK), 0) // CHUNK
    col = lax.broadcasted_iota(jnp.int32, (GMLP_CHUNK, GMLP_CHUNK), 1) // CHUNK
    keep = (row >= col) if ws_mask == "causal" else (row == col)

    def project(r0):
        x = x_ref[r0:r0 + sub, :]
        h = (x * _rsqrt_sumsq(x, x.shape[1])).astype(BF16)
        return _dot(h, win_scr[...])

    def finish(z, r0):
        rows = slice(r0, r0 + sub)
        ql = z[:, 0:Q_LORA]
        qn = (ql * _rsqrt_sumsq(ql, Q_LORA)).astype(BF16)
        qz = _dot(qn, wq_ref[...])
        tq = tq_ref[rows, :] * gq_ref[...]
        for hd in range(N_HEADS):
            cols = slice(hd * LANES, (hd + 1) * LANES)
            xh = qz[:, cols]
            ss = _dot((xh * xh).astype(BF16), seg_ref[...])
            q_out[rows, cols] = (xh * lax.rsqrt(ss + epsq_ref[...]) * tq).astype(BF16)

        y0 = z[:, COL_KR:COL_KR + LANES]
        s_k = jnp.sum(jnp.where(lane < QK_ROPE, y0 * y0, 0.0), axis=-1, keepdims=True)
        y = y0 * lax.rsqrt(s_k + QK_ROPE * EPS) * (gkr_ref[...] * tk_ref[rows, :])
        r64 = pltpu.roll(y, 64, 1)
        r32 = pltpu.roll(y, 32, 1)
        r96 = pltpu.roll(y, 96, 1)
        if v_transposed:
            kr_out[:, rows] = (y + r96).T[0:QK_ROPE, :]
        else:
            kr_out[rows, :] = (y + r96)[:, 0:QK_ROPE]
        kr_hi = jnp.where(lo, 0.0, r64 + r32 + r96)

        c0 = z[:, COL_CKV:COL_CKV + KV_LORA]
        cn = c0 * _rsqrt_sumsq(c0, KV_LORA) * gkv_ref[...]
        ckv_out[rows, :] = cn
        cb = cn.astype(BF16)
        kz = _dot(cb, wkk_ref[...])
        for hd in range(N_HEADS):
            cols = slice(hd * LANES, (hd + 1) * LANES)
            xh = kz[:, cols]
            kn = xh * _rsqrt_sumsq(xh, QK_NOPE)
            k_out[rows, cols] = jnp.where(lo, kn, kr_hi).astype(BF16)
        if v_transposed:
            vt = _dot_nt(wkv_ref[...], cb).astype(BF16)
            tail = jnp.where(lax.broadcasted_iota(jnp.int32, (BF16_SUBLANES, sub), 0) == 0, 1.0, 0.0).astype(BF16)
            for hd in range(N_HEADS):
                v_out[hd * V_ROWS:hd * V_ROWS + V_HEAD, rows] = vt[hd * V_HEAD:(hd + 1) * V_HEAD]
                v_out[hd * V_ROWS + V_HEAD:(hd + 1) * V_ROWS, rows] = tail
        else:
            v_out[rows, :] = _dot(cb, wkv_ref[...]).astype(BF16)

        for p in range(GMLP_GROUPS // 2):
            sl = slice(p * LANES, (p + 1) * LANES)
            w_pair = jnp.concatenate(
                [jnp.where(keep, ws_ref[2 * p], 0.0), jnp.where(keep, ws_ref[2 * p + 1], 0.0)], axis=0).astype(BF16)
            ug = _gelu_tanh(z[:, COL_U + p * LANES:COL_U + (p + 1) * LANES])
            vg = _gelu_tanh(z[:, COL_V + p * LANES:COL_V + (p + 1) * LANES])
            xc = vg - _half_sums(vg, lo) * (1.0 / GMLP_GROUP_DIM)
            vn = xc * _half_rsqrt_sumsq(xc, lo) * lng_ref[:, sl] + lnb_ref[:, sl]
            if vn_out:
                vn_out[0][rows, sl] = vn
            vnb = vn.astype(BF16)
            for c in range(sub // GMLP_CHUNK):
                crows = slice(c * GMLP_CHUNK, (c + 1) * GMLP_CHUNK)
                res = _dot(w_pair, vnb[crows])
                mixed = jnp.where(lo_c, res[0:GMLP_CHUNK], res[GMLP_CHUNK:]) + bs_ref[:, sl]
                og = ug[crows] * mixed
                ogm_out[r0 + c * GMLP_CHUNK:r0 + (c + 1) * GMLP_CHUNK, sl] = (
                    og * _half_rsqrt_sumsq(og, lo_c) * ggm_ref[:, sl]).astype(BF16)

    starts = list(range(0, tm, sub))
    z_next = project(starts[0])
    for i, r0 in enumerate(starts):
        z_cur = z_next
        if i + 1 < len(starts):
            z_next = project(starts[i + 1])
        finish(z_cur, r0)


def _const_spec(shape):
    return pl.BlockSpec(shape, lambda *_: (0,) * len(shape), pipeline_mode=pl.Buffered(1))


def _layer_spec(stacked, layer):
    zeros = (0,) * (stacked.ndim - 1)
    return pl.BlockSpec((None,) + stacked.shape[1:], lambda *_: (layer,) + zeros, pipeline_mode=pl.Buffered(1))


def _inproj(x, big, lw, tq, tk, ws, bs, *, layer, tm, table_blocks, ws_mask, want_vn, v_transposed):
    n, d = x.shape
    w_v = lw["w_kvv_t"] if v_transposed else lw["w_kvv"]
    grid = (n // tm,)
    tbl = lambda i: (i % table_blocks, 0)
    row = lambda i: (i, 0)
    layer_args = [big["w_in"], lw["w_q"], lw["gq"]]
    layer_args2 = [lw["g_kv_lora"], lw["gkr"]]
    layer_args3 = [lw["w_kvk"], w_v, lw["ln_g"], lw["ln_b"], ws, bs, lw["g_out_gm"]]
    in_specs = (
        [pl.BlockSpec((tm, d), row)] + [_layer_spec(a, layer) for a in layer_args]
        + [pl.BlockSpec((tm, LANES), tbl)] + [_layer_spec(a, layer) for a in layer_args2]
        + [pl.BlockSpec((tm, LANES), tbl)] + [_layer_spec(a, layer) for a in layer_args3]
        + [_const_spec((LANES, LANES)), _const_spec((1, LANES))])
    out_shape = [
        jax.ShapeDtypeStruct((n, N_HEADS * LANES), BF16),
        jax.ShapeDtypeStruct((n, N_HEADS * LANES), BF16),
        (jax.ShapeDtypeStruct((N_HEADS * V_ROWS, n), BF16) if v_transposed
         else jax.ShapeDtypeStruct((n, N_HEADS * V_HEAD), BF16)),
        jax.ShapeDtypeStruct((n, KV_LORA), F32),
        (jax.ShapeDtypeStruct((QK_ROPE, n), F32) if v_transposed
         else jax.ShapeDtypeStruct((n, QK_ROPE), F32)),
        jax.ShapeDtypeStruct((n, GMLP_WIDTH), BF16),
    ]
    out_specs = [
        pl.BlockSpec((tm, N_HEADS * LANES), row),
        pl.BlockSpec((tm, N_HEADS * LANES), row),
        (pl.BlockSpec((N_HEADS * V_ROWS, tm), lambda i: (0, i)) if v_transposed
         else pl.BlockSpec((tm, N_HEADS * V_HEAD), row)),
        pl.BlockSpec((tm, KV_LORA), row),
        (pl.BlockSpec((QK_ROPE, tm), lambda i: (0, i)) if v_transposed else pl.BlockSpec((tm, QK_ROPE), row)),
        pl.BlockSpec((tm, GMLP_WIDTH), row),
    ]
    if want_vn:
        out_shape.append(jax.ShapeDtypeStruct((n, GMLP_WIDTH), F32))
        out_specs.append(pl.BlockSpec((tm, GMLP_WIDTH), row))
    return pl.pallas_call(
        functools.partial(_inproj_kernel, ws_mask=ws_mask, v_transposed=v_transposed),
        grid=grid, in_specs=in_specs, out_specs=out_specs, out_shape=out_shape,
        scratch_shapes=[pltpu.VMEM((d, IN_COLS_PADDED), BF16)],
        compiler_params=pltpu.CompilerParams(dimension_semantics=("arbitrary",),
                                             vmem_limit_bytes=VMEM_LIMIT_BYTES),
        name="inproj_vn" if want_vn else "inproj",
    )(x, *layer_args, tq, *layer_args2, tk, *layer_args3,
      jnp.asarray(_Q_SEGMENTS, BF16), jnp.asarray(_Q_SEGMENT_EPS))


ATTN_STREAM_LAG = 18


def _attn_prompt_kernel(q_ref, k_ref, vt_ref, bias_ref, g_ref, *rest, tq, n_cast):
    w_f32 = rest[:n_cast]
    o_ref = rest[n_cast]
    w_bf16 = rest[n_cast + 1:2 * n_cast + 1]
    s_scr = rest[2 * n_cast + 1]
    for src, dst in zip(w_f32, w_bf16):
        dst[...] = src[...].astype(BF16)

    s_len = q_ref.shape[0]
    bodies = [(qi, hh) for qi in reversed(range(s_len // tq)) for hh in range(2)]
    bases = np.cumsum([0] + [qi + 1 for qi, _ in bodies])
    state = [{} for _ in bodies]

    def a_steps(bi):
        qi, hh = bodies[bi]
        cols = slice(hh * LANES, (hh + 1) * LANES)
        rows = slice(qi * tq, (qi + 1) * tq)
        st = state[bi]

        def chunk(j):
            s = _dot_nt(k_ref[j * tq:(j + 1) * tq, cols], q_ref[rows, cols])
            if j == qi:
                s = s + bias_ref[...]
            s_scr[bases[bi] + j] = s
            cm = jnp.max(s.reshape(tq // 8, 8, tq), axis=0)
            st["m_run"] = cm if j == 0 else jnp.maximum(st["m_run"], cm)

        def finish():
            st["m"] = jnp.max(st["m_run"], axis=0, keepdims=True)

        return [functools.partial(chunk, j) for j in range(qi + 1)] + [finish]

    def b_steps(bi):
        qi, hh = bodies[bi]
        st = state[bi]

        def chunk(j):
            p = jnp.exp2(s_scr[bases[bi] + j] - st["m"]).astype(BF16)
            d = _dot(vt_ref[hh * V_ROWS:(hh + 1) * V_ROWS, j * tq:(j + 1) * tq], p)
            st["acc"] = d if j == 0 else st["acc"] + d

        def finish():
            acc = st["acc"]
            o = acc[0:V_HEAD] * (1.0 / acc[V_HEAD:V_HEAD + 1])
            r = lax.rsqrt(jnp.sum(o * o, axis=0, keepdims=True) * (1.0 / V_HEAD) + EPS)
            st["out"] = o * r * g_ref[hh * V_HEAD:(hh + 1) * V_HEAD, :]
            if hh == 1:
                pair_t = jnp.concatenate([state[bi - 1]["out"], st["out"]], axis=0)
                o_ref[qi * tq:(qi + 1) * tq, :] = pair_t.T.astype(BF16)

        return [functools.partial(chunk, j) for j in range(qi + 1)] + [finish]

    a_all = [f for bi in range(len(bodies)) for f in a_steps(bi)]
    b_all = [f for bi in range(len(bodies)) for f in b_steps(bi)]
    for i in range(len(a_all) + ATTN_STREAM_LAG):
        if i < len(a_all):
            a_all[i]()
        if i >= ATTN_STREAM_LAG:
            b_all[i - ATTN_STREAM_LAG]()


def _attn_prompt(q, k, vt, bias_t, g_t, w_f32, *, layer, batch, seq, tq):
    n = q.shape[0]
    pairs = N_HEADS // 2
    nq = seq // tq
    steps = batch * pairs
    assert ATTN_STREAM_LAG > nq + 1

    def cast_specs(w):
        share = next(s for s in (1, 2, 4, 8) if w.shape[1] % ((steps // s) * BF16_SUBLANES) == 0)
        rows = w.shape[1] // (steps // share)
        slab = lambda b, p: (b * pairs + p) // share
        return (pl.BlockSpec((None, rows, w.shape[2]), lambda b, p: (layer, slab(b, p), 0)),
                pl.BlockSpec((rows, w.shape[2]), lambda b, p: (slab(b, p), 0)))

    w_specs = [cast_specs(w) for w in w_f32]
    return pl.pallas_call(
        functools.partial(_attn_prompt_kernel, tq=tq, n_cast=len(w_f32)),
        grid=(batch, pairs),
        in_specs=[
            pl.BlockSpec((seq, 2 * LANES), lambda b, p: (b, p)),
            pl.BlockSpec((seq, 2 * LANES), lambda b, p: (b, p)),
            pl.BlockSpec((2 * V_ROWS, seq), lambda b, p: (p, b)),
            pl.BlockSpec((tq, tq), lambda b, p: (0, 0)),
            pl.BlockSpec((None, 2 * V_HEAD, tq), lambda b, p: (layer, p, 0)),
        ] + [i for i, _ in w_specs],
        out_specs=[pl.BlockSpec((seq, 2 * V_HEAD), lambda b, p: (b, p))] + [o for _, o in w_specs],
        out_shape=[jax.ShapeDtypeStruct((n, N_HEADS * V_HEAD), BF16)]
        + [jax.ShapeDtypeStruct(w.shape[1:], BF16) for w in w_f32],
        scratch_shapes=[pltpu.VMEM((nq * (nq + 1), tq, tq), F32)],
        compiler_params=pltpu.CompilerParams(dimension_semantics=("arbitrary", "arbitrary"),
                                             vmem_limit_bytes=VMEM_LIMIT_BYTES),
        name="attn_prompt",
    )(q, k, vt, bias_t, g_t, *w_f32)


SAMPLE_KEY_BLOCK = 1024


def _attn_sample_kernel(q_ref, cnew_ref, krnew_ref, ckv_ref, krt_ref, w2_ref, wkt_ref, wv_ref, bias_ref, gout_ref,
                        o_ref, kr_scr, *, kblk):
    past = ckv_ref.shape[0]
    sq = q_ref.shape[0]
    q2 = jnp.concatenate([_dot(q_ref[:, h * LANES:(h + 1) * LANES], w2_ref[h]) for h in range(N_HEADS)],
                         axis=0).astype(BF16)
    qt = q2[:, 0:KV_LORA]
    qr = q2[:, KV_LORA:]
    kr_t = jnp.concatenate([krt_ref[...].astype(BF16), jnp.zeros((LANES - QK_ROPE, past), BF16)], axis=0)
    kr_scr[...] = jnp.zeros(kr_scr.shape, F32)
    kr_scr[0:sq, 0:QK_ROPE] = krnew_ref[...]
    ones_blk = jnp.ones((kblk, LANES), BF16)

    def scores(cb, s2):
        kzt = _dot_nt(wkt_ref[...], cb)
        s1 = _dot_nt(qt, cb)
        out = []
        for h in range(N_HEADS):
            kh = kzt[h * QK_NOPE:(h + 1) * QK_NOPE]
            r = lax.rsqrt(jnp.sum(kh * kh, axis=0, keepdims=True) + QK_NOPE * EPS)
            out.append(s1[h * sq:(h + 1) * sq] * r + s2[h * sq:(h + 1) * sq])
        return jnp.concatenate(out, axis=0)

    cn = jnp.concatenate([cnew_ref[...], jnp.zeros((LANES - sq, KV_LORA), F32)], axis=0).astype(BF16)
    blocks = [(ckv_ref[j * kblk:(j + 1) * kblk, :], kr_t[:, j * kblk:(j + 1) * kblk]) for j in range(past // kblk)]

    def block_scores(j):
        if j < len(blocks):
            cb = blocks[j][0].astype(BF16)
            return scores(cb, _dot(qr, blocks[j][1])), jnp.concatenate([cb, ones_blk], axis=1)
        s_new = scores(cn, _dot_nt(qr, kr_scr[...].astype(BF16))) + bias_ref[...]
        return s_new, jnp.concatenate([cn, ones_blk[0:LANES]], axis=1)

    m = acc = None
    nxt = block_scores(0)
    for j in range(len(blocks) + 1):
        s, vals = nxt
        if j < len(blocks):
            nxt = block_scores(j + 1)
        m_blk = jnp.max(s, axis=-1, keepdims=True)
        m_new = m_blk if m is None else jnp.maximum(m, m_blk)
        d = _dot(jnp.exp2(s - m_new).astype(BF16), vals)
        acc = d if acc is None else acc * jnp.exp2(m - m_new) + d
        m = m_new
    o_lat = (acc[:, 0:KV_LORA] * (1.0 / acc[:, KV_LORA:])).astype(BF16)
    o_full = _dot(o_lat, wv_ref[...])
    head = lax.broadcasted_iota(jnp.int32, (sq, N_HEADS * V_HEAD), 1) >> 6
    o = jnp.zeros((sq, N_HEADS * V_HEAD), F32)
    for h in range(N_HEADS):
        o = jnp.where(head == h, o_full[h * sq:(h + 1) * sq], o)
    lo = lax.broadcasted_iota(jnp.int32, (sq, LANES), 1) < 64
    for c in range(0, N_HEADS * V_HEAD, LANES):
        oc = o[:, c:c + LANES]
        o_ref[:, c:c + LANES] = (oc * _half_rms_scale(oc, lo) * gout_ref[:, c:c + LANES]).astype(BF16)


def _attn_sample(q, cnew, krnew, cache_ckv, kr_cache_t, w2, wkt, w_kvv, bias, gout, *, layer, batch, seq):
    n = q.shape[0]
    past = cache_ckv.shape[2]
    assert V_HEAD == 64 and past % SAMPLE_KEY_BLOCK == 0 and seq <= LANES
    return pl.pallas_call(
        functools.partial(_attn_sample_kernel, kblk=SAMPLE_KEY_BLOCK),
        grid=(batch,),
        in_specs=[
            pl.BlockSpec((seq, N_HEADS * LANES), lambda b: (b, 0)),
            pl.BlockSpec((seq, KV_LORA), lambda b: (b, 0)),
            pl.BlockSpec((seq, QK_ROPE), lambda b: (b, 0)),
            pl.BlockSpec((None, None, past, KV_LORA), lambda b: (layer, b, 0, 0)),
            pl.BlockSpec((None, None, QK_ROPE, past), lambda b: (layer, b, 0, 0)),
            _layer_spec(w2, layer),
            _layer_spec(wkt, layer),
            _layer_spec(w_kvv, layer),
            pl.BlockSpec((N_HEADS * seq, LANES), lambda b: (0, 0)),
            _layer_spec(gout, layer),
        ],
        out_specs=pl.BlockSpec((seq, N_HEADS * V_HEAD), lambda b: (b, 0)),
        out_shape=jax.ShapeDtypeStruct((n, N_HEADS * V_HEAD), BF16),
        scratch_shapes=[pltpu.VMEM((LANES, LANES), F32)],
        compiler_params=pltpu.CompilerParams(dimension_semantics=("arbitrary",),
                                             vmem_limit_bytes=VMEM_LIMIT_BYTES),
        name="attn_sample",
    )(q, cnew, krnew, cache_ckv, kr_cache_t, w2, wkt, w_kvv, bias, gout)


def _ffn_kernel(x_ref, oa_ref, og_ref, wo_ref, gffn_ref, wg_ref, wu_ref, wd_ref, y_ref, *, fchunk):
    aw = oa_ref.shape[1]
    x1 = x_ref[...] + _dot(oa_ref[...], wo_ref[0:aw, :]) + _dot(og_ref[...], wo_ref[aw:, :])
    h2 = (x1 * _rsqrt_sumsq(x1, x1.shape[1]) * gffn_ref[...]).astype(BF16)
    acc = None
    for c in range(0, wg_ref.shape[1], fchunk):
        g = _dot(h2, wg_ref[:, c:c + fchunk])
        u = _dot(h2, wu_ref[:, c:c + fchunk])
        a = (g * jax.nn.sigmoid(g) * u).astype(BF16)
        d = _dot(a, wd_ref[c:c + fchunk, :])
        acc = d if acc is None else acc + d
    y_ref[...] = x1 + acc


def _ffn(x, oa, og, g_ffn, w_o, w_gate, w_up, w_down, *, layer, tm, fchunk):
    n, d = x.shape
    row = lambda i: (i, 0)
    return pl.pallas_call(
        functools.partial(_ffn_kernel, fchunk=fchunk),
        grid=(n // tm,),
        in_specs=[
            pl.BlockSpec((tm, d), row),
            pl.BlockSpec((tm, oa.shape[1]), row),
            pl.BlockSpec((tm, og.shape[1]), row),
            _const_spec(w_o.shape),
            _layer_spec(g_ffn, layer),
            _const_spec(w_gate.shape),
            _const_spec(w_up.shape),
            _const_spec(w_down.shape),
        ],
        out_specs=pl.BlockSpec((tm, d), row),
        out_shape=jax.ShapeDtypeStruct((n, d), F32),
        compiler_params=pltpu.CompilerParams(dimension_semantics=("arbitrary",),
                                             vmem_limit_bytes=VMEM_LIMIT_BYTES),
        name="outproj_ffn",
    )(x, oa, og, w_o, g_ffn, w_gate, w_up, w_down)


def _rope_tables(pos):
    inv = ROPE_THETA ** (-np.arange(HALF, dtype=np.float64) / HALF)
    ang = np.asarray(pos, np.float64)[:, None] * inv[None, :]
    cos, sin = np.cos(ang), np.sin(ang)
    n = ang.shape[0]
    tq = np.concatenate([np.ones((n, QK_NOPE)), cos, cos, -sin, sin], axis=1)
    tk = np.concatenate([cos, cos, -sin, sin, np.zeros((n, LANES - 2 * QK_ROPE))], axis=1)
    return tq.astype(np.float32), tk.astype(np.float32)


def _rope_fold_matrix():
    f = np.zeros((LANES, LANES), np.float32)
    for i in range(QK_ROPE):
        f[QK_NOPE + i, i] = 1.0
        f[QK_NOPE + QK_ROPE + i, i] = 1.0
    return f


_ROPE_FOLD = _rope_fold_matrix()


def _q_segment_matrix():
    m = np.zeros((LANES, LANES), np.float32)
    m[:QK_NOPE, :QK_NOPE] = 1.0
    m[QK_NOPE:QK_NOPE + QK_ROPE, QK_NOPE:] = 1.0
    return m


_Q_SEGMENTS = _q_segment_matrix()
_Q_SEGMENT_COUNT = np.concatenate([np.full(QK_NOPE, QK_NOPE), np.full(LANES - QK_NOPE, QK_ROPE)]).astype(np.float32)
_Q_SEGMENT_EPS = (_Q_SEGMENT_COUNT * EPS)[None, :]


def _swap_halves(g):
    return jnp.concatenate([g[..., HALF:], g[..., :HALF]], axis=-1)


def _stacked_weights(g_mix, w_in):
    depth, d_model, _ = w_in.shape
    w = (jnp.swapaxes(w_in, 1, 2) * (g_mix * d_model ** 0.5)[:, None, :]).astype(BF16)
    kr0 = Q_LORA + KV_LORA
    w_in_p = jnp.concatenate(
        [w[:, :kr0 + QK_ROPE], w[:, kr0 + HALF:kr0 + QK_ROPE], w[:, kr0:kr0 + HALF],
         jnp.zeros((depth, LANES - 2 * QK_ROPE, d_model), BF16), w[:, kr0 + QK_ROPE:]], axis=1)
    return {"w_in": w_in_p}


def _small_weights(g_q_lora, w_q_up, g_qn, g_qr, g_kv_lora, g_kr, w_kv_up, g_kn, gm_ln_g, gm_ln_b, g_out_att,
                   g_out_gm, tq_attn):
    depth = w_q_up.shape[0]
    wq = w_q_up.reshape(depth, Q_LORA, N_HEADS, QK_HEAD)
    r1, r2 = wq[..., QK_NOPE:QK_NOPE + HALF], wq[..., QK_NOPE + HALF:]
    w_q = jnp.concatenate([wq[..., :QK_NOPE], r1, r2, r2, r1], axis=-1).reshape(depth, Q_LORA, N_HEADS * LANES)
    w_q = w_q * (g_q_lora * Q_LORA ** 0.5)[:, :, None]
    gq = (jnp.concatenate([g_qn * g_kn * QK_NOPE ** 0.5, g_qr, _swap_halves(g_qr)], axis=-1)
          * np.sqrt(_Q_SEGMENT_COUNT) * (QK_HEAD ** -0.5 * LOG2_E))
    gkr = jnp.concatenate([g_kr, _swap_halves(g_kr), jnp.zeros((depth, LANES - 2 * QK_ROPE), F32)],
                          axis=-1) * QK_ROPE ** 0.5
    wkv = w_kv_up.reshape(depth, KV_LORA, N_HEADS, QK_NOPE + V_HEAD)
    w_kvk = jnp.concatenate([wkv[..., :QK_NOPE], jnp.zeros((depth, KV_LORA, N_HEADS, LANES - QK_NOPE), F32)],
                            axis=-1).reshape(depth, KV_LORA, N_HEADS * LANES)
    w_kvv = wkv[..., QK_NOPE:].reshape(depth, KV_LORA, N_HEADS * V_HEAD)
    wk_t = jnp.transpose(wkv[..., :QK_NOPE], (0, 2, 3, 1))
    w_abs = jnp.concatenate([wk_t, jnp.zeros((depth, N_HEADS, LANES - QK_NOPE, KV_LORA), F32)], axis=2)
    w2 = jnp.concatenate([w_abs, jnp.broadcast_to(jnp.asarray(_ROPE_FOLD), (depth, N_HEADS, LANES, LANES))], axis=3)
    g_att = g_out_att.reshape(depth, N_HEADS * V_HEAD)
    return {
        "w_q": w_q.astype(BF16), "gq": gq[:, None, :],
        "g_kv_lora": g_kv_lora[:, None, :] * KV_LORA ** 0.5, "gkr": gkr[:, None, :],
        "w_kvk": w_kvk.astype(BF16), "w_kvv": w_kvv.astype(BF16),
        "w_kvv_t": jnp.swapaxes(w_kvv, 1, 2).astype(BF16),
        "w2": w2.astype(BF16), "wk_t": wk_t.reshape(depth, N_HEADS * QK_NOPE, KV_LORA).astype(BF16),
        "ln_g": gm_ln_g.reshape(depth, 1, GMLP_WIDTH) * GMLP_GROUP_DIM ** 0.5,
        "ln_b": gm_ln_b.reshape(depth, 1, GMLP_WIDTH),
        "g_out_att": g_att[:, None, :],
        "g_out_att_t": jnp.broadcast_to(g_att[:, :, None], (depth, N_HEADS * V_HEAD, tq_attn)),
        "g_out_gm": g_out_gm.reshape(depth, 1, GMLP_WIDTH) * GMLP_GROUP_DIM ** 0.5,
    }


def _chunk_bias(qpos, kpos):
    vis = (kpos[None, :] // CHUNK) <= (qpos[:, None] // CHUNK)
    return np.where(vis, 0.0, NEG_INF).astype(np.float32)


def kernel(x_prompt, x_sample, cache_ckv, cache_krope, g_mix, w_in, g_q_lora, w_q_up, g_qn, g_qr, g_kv_lora, g_kr,
           w_kv_up, g_kn, gm_ln_g, gm_ln_b, gm_w_s, gm_b_s, g_out_att, g_out_gm, w_o, g_ffn, w_gate, w_up, w_down):
    batch, seq, d = x_prompt.shape
    dbatch, dseq, _ = x_sample.shape
    depth, _, past, _ = cache_ckv.shape
    assert seq % GMLP_CHUNK == 0 and GMLP_CHUNK % dseq == 0 and (dbatch * dseq) % GMLP_CHUNK == 0

    tq_attn = 256
    tm_in = 1024
    tm_ffn = 1024
    tm_s = dbatch * dseq

    tq_p, tk_p = (jnp.asarray(t) for t in _rope_tables(np.arange(seq)))
    tq_s, tk_s = (jnp.asarray(np.tile(t, (dbatch, 1))) for t in _rope_tables(past + np.arange(dseq)))
    bias_p = jnp.asarray(_chunk_bias(np.arange(tq_attn), np.arange(tq_attn)))
    bias_s = np.full((dseq, LANES), NEG_INF, np.float32)
    bias_s[:, :dseq] = _chunk_bias(past + np.arange(dseq), past + np.arange(dseq))
    bias_s = jnp.asarray(np.tile(bias_s, (N_HEADS, 1)))
    reps = GMLP_CHUNK // dseq

    yp = x_prompt.reshape(batch * seq, d)
    ys = x_sample.reshape(dbatch * dseq, d)
    ckv_p, kr_p, ckv_s, kr_s, gv_s = [], [], [], [], []
    big = _stacked_weights(g_mix, w_in)
    g_ffn_s = (g_ffn * d ** 0.5)[:, None, :]
    ffn_w_f32 = [w_o, w_gate, w_up, w_down]
    lw = _small_weights(g_q_lora, w_q_up, g_qn, g_qr, g_kv_lora, g_kr, w_kv_up, g_kn, gm_ln_g, gm_ln_b, g_out_att,
                        g_out_gm, tq_attn)
    kr_cache_t = jnp.swapaxes(cache_krope, 2, 3)
    ws_p = gm_w_s
    bs_p = jnp.repeat(jnp.swapaxes(gm_b_s, 1, 2), GMLP_GROUP_DIM, axis=2)
    ws_s = jnp.tile(gm_w_s[:, :, :dseq, :dseq], (1, 1, reps, reps))
    bs_s = jnp.tile(jnp.repeat(jnp.swapaxes(gm_b_s[:, :, :dseq], 1, 2), GMLP_GROUP_DIM, axis=2), (1, reps, 1))
    for l in range(depth):
        q, k, vt, c, r, ogm = _inproj(yp, big, lw, tq_p, tk_p, ws_p, bs_p, layer=l, tm=tm_in, table_blocks=seq // tm_in,
                                      ws_mask="causal", want_vn=False, v_transposed=True)
        oatt, wo_b, wg_b, wu_b, wd_b = _attn_prompt(q, k, vt, bias_p.T, lw["g_out_att_t"], ffn_w_f32, layer=l,
                                                    batch=batch, seq=seq, tq=tq_attn)
        ffn_w = (g_ffn_s, wo_b, wg_b, wu_b, wd_b)
        yp = _ffn(yp, oatt, ogm, *ffn_w, layer=l, tm=tm_ffn, fchunk=256)
        ckv_p.append(c.reshape(batch, seq, KV_LORA))
        kr_p.append(jnp.transpose(r.reshape(QK_ROPE, batch, seq), (1, 2, 0)))

        q, k, v, c, r, ogm, vn = _inproj(ys, big, lw, tq_s, tk_s, ws_s, bs_s, layer=l, tm=tm_s, table_blocks=1,
                                         ws_mask="blockdiag", want_vn=True, v_transposed=False)
        oatt = _attn_sample(q, c, r, cache_ckv, kr_cache_t, lw["w2"], lw["wk_t"], lw["w_kvv"], bias_s,
                            lw["g_out_att"], layer=l, batch=dbatch, seq=dseq)
        ys = _ffn(ys, oatt, ogm, *ffn_w, layer=l, tm=tm_s, fchunk=256)
        ckv_s.append(c.reshape(dbatch, dseq, KV_LORA))
        kr_s.append(r.reshape(dbatch, dseq, QK_ROPE))
        gv_s.append(vn.reshape(dbatch, dseq, GMLP_WIDTH))

    return (yp.reshape(batch, seq, d), ys.reshape(dbatch, dseq, d), jnp.stack(ckv_p), jnp.stack(kr_p),
            jnp.stack(ckv_s), jnp.stack(kr_s), jnp.stack(gv_s))
```

```python
import functools

import numpy as np
import jax
import jax.numpy as jnp
from jax import lax
from jax.experimental import pallas as pl
from jax.experimental.pallas import tpu as pltpu

F32 = jnp.float32
BF16 = jnp.bfloat16

CHUNK = 64
N_HEADS = 8
QK_NOPE = 64
QK_ROPE = 32
QK_HEAD = QK_NOPE + QK_ROPE
V_HEAD = 64
Q_LORA = 256
KV_LORA = 128
ROPE_THETA = 10000.0
GMLP_GROUPS = 8
GMLP_GROUP_DIM = 64
GMLP_WIDTH = GMLP_GROUPS * GMLP_GROUP_DIM
GMLP_CHUNK = 128
EPS = 1e-6
NEG_INF = -1e30
LOG2_E = 1.4426950408889634

LANES = 128
BF16_SUBLANES = 16
V7X_VMEM_BYTES = 64 * 1024 * 1024
VMEM_LIMIT_BYTES = (V7X_VMEM_BYTES * 3) // 4

HALF = QK_ROPE // 2
V_ROWS = V_HEAD + BF16_SUBLANES
IN_COLS_PADDED = Q_LORA + KV_LORA + LANES + 2 * GMLP_WIDTH
COL_CKV = Q_LORA
COL_KR = Q_LORA + KV_LORA
COL_U = COL_KR + LANES
COL_V = COL_U + GMLP_WIDTH

_NT = (((1,), (1,)), ((), ()))


def _dot(a, b):
    return jnp.dot(a, b, preferred_element_type=F32)


def _dot_nt(a, b):
    return lax.dot_general(a, b, _NT, preferred_element_type=F32)


def _rsqrt_sumsq(x, n):
    return lax.rsqrt(jnp.sum(x * x, axis=-1, keepdims=True) + n * EPS)


def _half_sums(x, lo):
    s_lo = jnp.sum(jnp.where(lo, x, 0.0), axis=-1, keepdims=True)
    s_hi = jnp.sum(jnp.where(lo, 0.0, x), axis=-1, keepdims=True)
    return jnp.where(lo, s_lo, s_hi)


def _half_rms_scale(x, lo):
    return lax.rsqrt(_half_sums(x * x, lo) * (1.0 / 64) + EPS)


def _half_rsqrt_sumsq(x, lo):
    return lax.rsqrt(_half_sums(x * x, lo) + 64 * EPS)


def _gelu_tanh(t):
    c = 0.7978845608028654
    h = 0.5 * t
    return h + h * jnp.tanh(t * (c + (c * 0.044715) * (t * t)))


INPROJ_SUB_ROWS = 256


def _inproj_kernel(x_ref, win_ref, wq_ref, gq_ref, tq_ref, gkv_ref, gkr_ref, tk_ref,
                   wkk_ref, wkv_ref, lng_ref, lnb_ref, ws_ref, bs_ref, ggm_ref, seg_ref, epsq_ref,
                   q_out, k_out, v_out, ckv_out, kr_out, ogm_out, *rest, ws_mask, v_transposed):
    *vn_out, win_scr = rest
    tm = x_ref.shape[0]
    sub = min(tm, INPROJ_SUB_ROWS)

    @pl.when(pl.program_id(0) == 0)
    def _():
        win_scr[...] = win_ref[...].T

    lane = lax.broadcasted_iota(jnp.int32, (sub, LANES), 1)
    lo = lane < 64
    lo_c = lax.broadcasted_iota(jnp.int32, (GMLP_CHUNK, LANES), 1) < 64
    row = lax.broadcasted_iota(jnp.int32, (GMLP_CHUNK, GMLP_CHUNK), 0) // CHUNK
    col = lax.broadcasted_iota(jnp.int32, (GMLP_CHUNK, GMLP_CHUNK), 1) // CHUNK
    keep = (row >= col) if ws_mask == "causal" else (row == col)

    def project(r0):
        x = x_ref[r0:r0 + sub, :]
        h = (x * _rsqrt_sumsq(x, x.shape[1])).astype(BF16)
        return _dot(h, win_scr[...])

    def finish(z, r0):
        rows = slice(r0, r0 + sub)
        ql = z[:, 0:Q_LORA]
        qn = (ql * _rsqrt_sumsq(ql, Q_LORA)).astype(BF16)
        qz = _dot(qn, wq_ref[...])
        tq = tq_ref[rows, :] * gq_ref[...]
        for hd in range(N_HEADS):
            cols = slice(hd * LANES, (hd + 1) * LANES)
            xh = qz[:, cols]
            ss = _dot((xh * xh).astype(BF16), seg_ref[...])
            q_out[rows, cols] = (xh * lax.rsqrt(ss + epsq_ref[...]) * tq).astype(BF16)

        y0 = z[:, COL_KR:COL_KR + LANES]
        s_k = jnp.sum(jnp.where(lane < QK_ROPE, y0 * y0, 0.0), axis=-1, keepdims=True)
        y = y0 * lax.rsqrt(s_k + QK_ROPE * EPS) * (gkr_ref[...] * tk_ref[rows, :])
        r64 = pltpu.roll(y, 64, 1)
        r32 = pltpu.roll(y, 32, 1)
        r96 = pltpu.roll(y, 96, 1)
        if v_transposed:
            kr_out[:, rows] = (y + r96).T[0:QK_ROPE, :]
        else:
            kr_out[rows, :] = (y + r96)[:, 0:QK_ROPE]
        kr_hi = jnp.where(lo, 0.0, r64 + r32 + r96)

        c0 = z[:, COL_CKV:COL_CKV + KV_LORA]
        cn = c0 * _rsqrt_sumsq(c0, KV_LORA) * gkv_ref[...]
        ckv_out[rows, :] = cn
        cb = cn.astype(BF16)
        kz = _dot(cb, wkk_ref[...])
        for hd in range(N_HEADS):
            cols = slice(hd * LANES, (hd + 1) * LANES)
            xh = kz[:, cols]
            kn = xh * _rsqrt_sumsq(xh, QK_NOPE)
            k_out[rows, cols] = jnp.where(lo, kn, kr_hi).astype(BF16)
        if v_transposed:
            vt = _dot_nt(wkv_ref[...], cb).astype(BF16)
            tail = jnp.where(lax.broadcasted_iota(jnp.int32, (BF16_SUBLANES, sub), 0) == 0, 1.0, 0.0).astype(BF16)
            for hd in range(N_HEADS):
                v_out[hd * V_ROWS:hd * V_ROWS + V_HEAD, rows] = vt[hd * V_HEAD:(hd + 1) * V_HEAD]
                v_out[hd * V_ROWS + V_HEAD:(hd + 1) * V_ROWS, rows] = tail
        else:
            v_out[rows, :] = _dot(cb, wkv_ref[...]).astype(BF16)

        for p in range(GMLP_GROUPS // 2):
            sl = slice(p * LANES, (p + 1) * LANES)
            w_pair = jnp.concatenate(
                [jnp.where(keep, ws_ref[2 * p], 0.0), jnp.where(keep, ws_ref[2 * p + 1], 0.0)], axis=0).astype(BF16)
            ug = _gelu_tanh(z[:, COL_U + p * LANES:COL_U + (p + 1) * LANES])
            vg = _gelu_tanh(z[:, COL_V + p * LANES:COL_V + (p + 1) * LANES])
            xc = vg - _half_sums(vg, lo) * (1.0 / GMLP_GROUP_DIM)
            vn = xc * _half_rsqrt_sumsq(xc, lo) * lng_ref[:, sl] + lnb_ref[:, sl]
            if vn_out:
                vn_out[0][rows, sl] = vn
            vnb = vn.astype(BF16)
            for c in range(sub // GMLP_CHUNK):
                crows = slice(c * GMLP_CHUNK, (c + 1) * GMLP_CHUNK)
                res = _dot(w_pair, vnb[crows])
                mixed = jnp.where(lo_c, res[0:GMLP_CHUNK], res[GMLP_CHUNK:]) + bs_ref[:, sl]
                og = ug[crows] * mixed
                ogm_out[r0 + c * GMLP_CHUNK:r0 + (c + 1) * GMLP_CHUNK, sl] = (
                    og * _half_rsqrt_sumsq(og, lo_c) * ggm_ref[:, sl]).astype(BF16)

    starts = list(range(0, tm, sub))
    z_next = project(starts[0])
    for i, r0 in enumerate(starts):
        z_cur = z_next
        if i + 1 < len(starts):
            z_next = project(starts[i + 1])
        finish(z_cur, r0)


def _const_spec(shape):
    return pl.BlockSpec(shape, lambda *_: (0,) * len(shape), pipeline_mode=pl.Buffered(1))


def _layer_spec(stacked, layer):
    zeros = (0,) * (stacked.ndim - 1)
    return pl.BlockSpec((None,) + stacked.shape[1:], lambda *_: (layer,) + zeros, pipeline_mode=pl.Buffered(1))


def _inproj(x, big, lw, tq, tk, ws, bs, *, layer, tm, table_blocks, ws_mask, want_vn, v_transposed):
    n, d = x.shape
    w_v = lw["w_kvv_t"] if v_transposed else lw["w_kvv"]
    grid = (n // tm,)
    tbl = lambda i: (i % table_blocks, 0)
    row = lambda i: (i, 0)
    layer_args = [big["w_in"], lw["w_q"], lw["gq"]]
    layer_args2 = [lw["g_kv_lora"], lw["gkr"]]
    layer_args3 = [lw["w_kvk"], w_v, lw["ln_g"], lw["ln_b"], ws, bs, lw["g_out_gm"]]
    in_specs = (
        [pl.BlockSpec((tm, d), row)] + [_layer_spec(a, layer) for a in layer_args]
        + [pl.BlockSpec((tm, LANES), tbl)] + [_layer_spec(a, layer) for a in layer_args2]
        + [pl.BlockSpec((tm, LANES), tbl)] + [_layer_spec(a, layer) for a in layer_args3]
        + [_const_spec((LANES, LANES)), _const_spec((1, LANES))])
    out_shape = [
        jax.ShapeDtypeStruct((n, N_HEADS * LANES), BF16),
        jax.ShapeDtypeStruct((n, N_HEADS * LANES), BF16),
        (jax.ShapeDtypeStruct((N_HEADS * V_ROWS, n), BF16) if v_transposed
         else jax.ShapeDtypeStruct((n, N_HEADS * V_HEAD), BF16)),
        jax.ShapeDtypeStruct((n, KV_LORA), F32),
        (jax.ShapeDtypeStruct((QK_ROPE, n), F32) if v_transposed
         else jax.ShapeDtypeStruct((n, QK_ROPE), F32)),
        jax.ShapeDtypeStruct((n, GMLP_WIDTH), BF16),
    ]
    out_specs = [
        pl.BlockSpec((tm, N_HEADS * LANES), row),
        pl.BlockSpec((tm, N_HEADS * LANES), row),
        (pl.BlockSpec((N_HEADS * V_ROWS, tm), lambda i: (0, i)) if v_transposed
         else pl.BlockSpec((tm, N_HEADS * V_HEAD), row)),
        pl.BlockSpec((tm, KV_LORA), row),
        (pl.BlockSpec((QK_ROPE, tm), lambda i: (0, i)) if v_transposed else pl.BlockSpec((tm, QK_ROPE), row)),
        pl.BlockSpec((tm, GMLP_WIDTH), row),
    ]
    if want_vn:
        out_shape.append(jax.ShapeDtypeStruct((n, GMLP_WIDTH), F32))
        out_specs.append(pl.BlockSpec((tm, GMLP_WIDTH), row))
    return pl.pallas_call(
        functools.partial(_inproj_kernel, ws_mask=ws_mask, v_transposed=v_transposed),
        grid=grid, in_specs=in_specs, out_specs=out_specs, out_shape=out_shape,
        scratch_shapes=[pltpu.VMEM((d, IN_COLS_PADDED), BF16)],
        compiler_params=pltpu.CompilerParams(dimension_semantics=("arbitrary",),
                                             vmem_limit_bytes=VMEM_LIMIT_BYTES),
        name="inproj_vn" if want_vn else "inproj",
    )(x, *layer_args, tq, *layer_args2, tk, *layer_args3,
      jnp.asarray(_Q_SEGMENTS, BF16), jnp.asarray(_Q_SEGMENT_EPS))


ATTN_STREAM_LAG = 18


def _attn_prompt_kernel(q_ref, k_ref, vt_ref, bias_ref, g_ref, *rest, tq, n_cast):
    w_f32 = rest[:n_cast]
    o_ref = rest[n_cast]
    w_bf16 = rest[n_cast + 1:2 * n_cast + 1]
    s_scr = rest[2 * n_cast + 1]
    for src, dst in zip(w_f32, w_bf16):
        dst[...] = src[...].astype(BF16)

    s_len = q_ref.shape[0]
    bodies = [(qi, hh) for qi in reversed(range(s_len // tq)) for hh in range(2)]
    bases = np.cumsum([0] + [qi + 1 for qi, _ in bodies])
    state = [{} for _ in bodies]

    def a_steps(bi):
        qi, hh = bodies[bi]
        cols = slice(hh * LANES, (hh + 1) * LANES)
        rows = slice(qi * tq, (qi + 1) * tq)
        st = state[bi]

        def chunk(j):
            s = _dot_nt(k_ref[j * tq:(j + 1) * tq, cols], q_ref[rows, cols])
            if j == qi:
                s = s + bias_ref[...]
            s_scr[bases[bi] + j] = s
            cm = jnp.max(s.reshape(tq // 8, 8, tq), axis=0)
            st["m_run"] = cm if j == 0 else jnp.maximum(st["m_run"], cm)

        def finish():
            st["m"] = jnp.max(st["m_run"], axis=0, keepdims=True)

        return [functools.partial(chunk, j) for j in range(qi + 1)] + [finish]

    def b_steps(bi):
        qi, hh = bodies[bi]
        st = state[bi]

        def chunk(j):
            p = jnp.exp2(s_scr[bases[bi] + j] - st["m"]).astype(BF16)
            d = _dot(vt_ref[hh * V_ROWS:(hh + 1) * V_ROWS, j * tq:(j + 1) * tq], p)
            st["acc"] = d if j == 0 else st["acc"] + d

        def finish():
            acc = st["acc"]
            o = acc[0:V_HEAD] * (1.0 / acc[V_HEAD:V_HEAD + 1])
            r = lax.rsqrt(jnp.sum(o * o, axis=0, keepdims=True) * (1.0 / V_HEAD) + EPS)
            st["out"] = o * r * g_ref[hh * V_HEAD:(hh + 1) * V_HEAD, :]
            if hh == 1:
                pair_t = jnp.concatenate([state[bi - 1]["out"], st["out"]], axis=0)
                o_ref[qi * tq:(qi + 1) * tq, :] = pair_t.T.astype(BF16)

        return [functools.partial(chunk, j) for j in range(qi + 1)] + [finish]

    a_all = [f for bi in range(len(bodies)) for f in a_steps(bi)]
    b_all = [f for bi in range(len(bodies)) for f in b_steps(bi)]
    for i in range(len(a_all) + ATTN_STREAM_LAG):
        if i < len(a_all):
            a_all[i]()
        if i >= ATTN_STREAM_LAG:
            b_all[i - ATTN_STREAM_LAG]()


def _attn_prompt(q, k, vt, bias_t, g_t, w_f32, *, layer, batch, seq, tq):
    n = q.shape[0]
    pairs = N_HEADS // 2
    nq = seq // tq
    steps = batch * pairs
    assert ATTN_STREAM_LAG > nq + 1

    def cast_specs(w):
        share = next(s for s in (1, 2, 4, 8) if w.shape[1] % ((steps // s) * BF16_SUBLANES) == 0)
        rows = w.shape[1] // (steps // share)
        slab = lambda b, p: (b * pairs + p) // share
        return (pl.BlockSpec((None, rows, w.shape[2]), lambda b, p: (layer, slab(b, p), 0)),
                pl.BlockSpec((rows, w.shape[2]), lambda b, p: (slab(b, p), 0)))

    w_specs = [cast_specs(w) for w in w_f32]
    return pl.pallas_call(
        functools.partial(_attn_prompt_kernel, tq=tq, n_cast=len(w_f32)),
        grid=(batch, pairs),
        in_specs=[
            pl.BlockSpec((seq, 2 * LANES), lambda b, p: (b, p)),
            pl.BlockSpec((seq, 2 * LANES), lambda b, p: (b, p)),
            pl.BlockSpec((2 * V_ROWS, seq), lambda b, p: (p, b)),
            pl.BlockSpec((tq, tq), lambda b, p: (0, 0)),
            pl.BlockSpec((None, 2 * V_HEAD, tq), lambda b, p: (layer, p, 0)),
        ] + [i for i, _ in w_specs],
        out_specs=[pl.BlockSpec((seq, 2 * V_HEAD), lambda b, p: (b, p))] + [o for _, o in w_specs],
        out_shape=[jax.ShapeDtypeStruct((n, N_HEADS * V_HEAD), BF16)]
        + [jax.ShapeDtypeStruct(w.shape[1:], BF16) for w in w_f32],
        scratch_shapes=[pltpu.VMEM((nq * (nq + 1), tq, tq), F32)],
        compiler_params=pltpu.CompilerParams(dimension_semantics=("arbitrary", "arbitrary"),
                                             vmem_limit_bytes=VMEM_LIMIT_BYTES),
        name="attn_prompt",
    )(q, k, vt, bias_t, g_t, *w_f32)


SAMPLE_KEY_BLOCK = 1024


def _attn_sample_kernel(q_ref, cnew_ref, krnew_ref, ckv_ref, krt_ref, w2_ref, wkt_ref, wv_ref, bias_ref, gout_ref,
                        o_ref, kr_scr, *, kblk):
    past = ckv_ref.shape[0]
    sq = q_ref.shape[0]
    q2 = jnp.concatenate([_dot(q_ref[:, h * LANES:(h + 1) * LANES], w2_ref[h]) for h in range(N_HEADS)],
                         axis=0).astype(BF16)
    qt = q2[:, 0:KV_LORA]
    qr = q2[:, KV_LORA:]
    kr_t = jnp.concatenate([krt_ref[...].astype(BF16), jnp.zeros((LANES - QK_ROPE, past), BF16)], axis=0)
    kr_scr[...] = jnp.zeros(kr_scr.shape, F32)
    kr_scr[0:sq, 0:QK_ROPE] = krnew_ref[...]
    ones_blk = jnp.ones((kblk, LANES), BF16)

    def scores(cb, s2):
        kzt = _dot_nt(wkt_ref[...], cb)
        s1 = _dot_nt(qt, cb)
        out = []
        for h in range(N_HEADS):
            kh = kzt[h * QK_NOPE:(h + 1) * QK_NOPE]
            r = lax.rsqrt(jnp.sum(kh * kh, axis=0, keepdims=True) + QK_NOPE * EPS)
            out.append(s1[h * sq:(h + 1) * sq] * r + s2[h * sq:(h + 1) * sq])
        return jnp.concatenate(out, axis=0)

    cn = jnp.concatenate([cnew_ref[...], jnp.zeros((LANES - sq, KV_LORA), F32)], axis=0).astype(BF16)
    blocks = [(ckv_ref[j * kblk:(j + 1) * kblk, :], kr_t[:, j * kblk:(j + 1) * kblk]) for j in range(past // kblk)]

    def block_scores(j):
        if j < len(blocks):
            cb = blocks[j][0].astype(BF16)
            return scores(cb, _dot(qr, blocks[j][1])), jnp.concatenate([cb, ones_blk], axis=1)
        s_new = scores(cn, _dot_nt(qr, kr_scr[...].astype(BF16))) + bias_ref[...]
        return s_new, jnp.concatenate([cn, ones_blk[0:LANES]], axis=1)

    m = acc = None
    nxt = block_scores(0)
    for j in range(len(blocks) + 1):
        s, vals = nxt
        if j < len(blocks):
            nxt = block_scores(j + 1)
        m_blk = jnp.max(s, axis=-1, keepdims=True)
        m_new = m_blk if m is None else jnp.maximum(m, m_blk)
        d = _dot(jnp.exp2(s - m_new).astype(BF16), vals)
        acc = d if acc is None else acc * jnp.exp2(m - m_new) + d
        m = m_new
    o_lat = (acc[:, 0:KV_LORA] * (1.0 / acc[:, KV_LORA:])).astype(BF16)
    o_full = _dot(o_lat, wv_ref[...])
    head = lax.broadcasted_iota(jnp.int32, (sq, N_HEADS * V_HEAD), 1) >> 6
    o = jnp.zeros((sq, N_HEADS * V_HEAD), F32)
    for h in range(N_HEADS):
        o = jnp.where(head == h, o_full[h * sq:(h + 1) * sq], o)
    lo = lax.broadcasted_iota(jnp.int32, (sq, LANES), 1) < 64
    for c in range(0, N_HEADS * V_HEAD, LANES):
        oc = o[:, c:c + LANES]
        o_ref[:, c:c + LANES] = (oc * _half_rms_scale(oc, lo) * gout_ref[:, c:c + LANES]).astype(BF16)


def _attn_sample(q, cnew, krnew, cache_ckv, kr_cache_t, w2, wkt, w_kvv, bias, gout, *, layer, batch, seq):
    n = q.shape[0]
    past = cache_ckv.shape[2]
    assert V_HEAD == 64 and past % SAMPLE_KEY_BLOCK == 0 and seq <= LANES
    return pl.pallas_call(
        functools.partial(_attn_sample_kernel, kblk=SAMPLE_KEY_BLOCK),
        grid=(batch,),
        in_specs=[
            pl.BlockSpec((seq, N_HEADS * LANES), lambda b: (b, 0)),
            pl.BlockSpec((seq, KV_LORA), lambda b: (b, 0)),
            pl.BlockSpec((seq, QK_ROPE), lambda b: (b, 0)),
            pl.BlockSpec((None, None, past, KV_LORA), lambda b: (layer, b, 0, 0)),
            pl.BlockSpec((None, None, QK_ROPE, past), lambda b: (layer, b, 0, 0)),
            _layer_spec(w2, layer),
            _layer_spec(wkt, layer),
            _layer_spec(w_kvv, layer),
            pl.BlockSpec((N_HEADS * seq, LANES), lambda b: (0, 0)),
            _layer_spec(gout, layer),
        ],
        out_specs=pl.BlockSpec((seq, N_HEADS * V_HEAD), lambda b: (b, 0)),
        out_shape=jax.ShapeDtypeStruct((n, N_HEADS * V_HEAD), BF16),
        scratch_shapes=[pltpu.VMEM((LANES, LANES), F32)],
        compiler_params=pltpu.CompilerParams(dimension_semantics=("arbitrary",),
                                             vmem_limit_bytes=VMEM_LIMIT_BYTES),
        name="attn_sample",
    )(q, cnew, krnew, cache_ckv, kr_cache_t, w2, wkt, w_kvv, bias, gout)


def _ffn_kernel(x_ref, oa_ref, og_ref, wo_ref, gffn_ref, wg_ref, wu_ref, wd_ref, y_ref, *, fchunk):
    x1 = x_ref[...] + _dot(jnp.concatenate([oa_ref[...], og_ref[...]], axis=1), wo_ref[...])
    h2 = (x1 * _rsqrt_sumsq(x1, x1.shape[1]) * gffn_ref[...]).astype(BF16)
    acc = None
    for c in range(0, wg_ref.shape[1], fchunk):
        g = _dot(h2, wg_ref[:, c:c + fchunk])
        u = _dot(h2, wu_ref[:, c:c + fchunk])
        a = (g * jax.nn.sigmoid(g) * u).astype(BF16)
        d = _dot(a, wd_ref[c:c + fchunk, :])
        acc = d if acc is None else acc + d
    y_ref[...] = x1 + acc


def _ffn(x, oa, og, g_ffn, w_o, w_gate, w_up, w_down, *, layer, tm, fchunk):
    n, d = x.shape
    row = lambda i: (i, 0)
    return pl.pallas_call(
        functools.partial(_ffn_kernel, fchunk=fchunk),
        grid=(n // tm,),
        in_specs=[
            pl.BlockSpec((tm, d), row),
            pl.BlockSpec((tm, oa.shape[1]), row),
            pl.BlockSpec((tm, og.shape[1]), row),
            _const_spec(w_o.shape),
            _layer_spec(g_ffn, layer),
            _const_spec(w_gate.shape),
            _const_spec(w_up.shape),
            _const_spec(w_down.shape),
        ],
        out_specs=pl.BlockSpec((tm, d), row),
        out_shape=jax.ShapeDtypeStruct((n, d), F32),
        compiler_params=pltpu.CompilerParams(dimension_semantics=("arbitrary",),
                                             vmem_limit_bytes=VMEM_LIMIT_BYTES),
        name="outproj_ffn",
    )(x, oa, og, w_o, g_ffn, w_gate, w_up, w_down)


def _rope_tables(pos):
    inv = ROPE_THETA ** (-np.arange(HALF, dtype=np.float64) / HALF)
    ang = np.asarray(pos, np.float64)[:, None] * inv[None, :]
    cos, sin = np.cos(ang), np.sin(ang)
    n = ang.shape[0]
    tq = np.concatenate([np.ones((n, QK_NOPE)), cos, cos, -sin, sin], axis=1)
    tk = np.concatenate([cos, cos, -sin, sin, np.zeros((n, LANES - 2 * QK_ROPE))], axis=1)
    return tq.astype(np.float32), tk.astype(np.float32)


def _rope_fold_matrix():
    f = np.zeros((LANES, LANES), np.float32)
    for i in range(QK_ROPE):
        f[QK_NOPE + i, i] = 1.0
        f[QK_NOPE + QK_ROPE + i, i] = 1.0
    return f


_ROPE_FOLD = _rope_fold_matrix()


def _q_segment_matrix():
    m = np.zeros((LANES, LANES), np.float32)
    m[:QK_NOPE, :QK_NOPE] = 1.0
    m[QK_NOPE:QK_NOPE + QK_ROPE, QK_NOPE:] = 1.0
    return m


_Q_SEGMENTS = _q_segment_matrix()
_Q_SEGMENT_COUNT = np.concatenate([np.full(QK_NOPE, QK_NOPE), np.full(LANES - QK_NOPE, QK_ROPE)]).astype(np.float32)
_Q_SEGMENT_EPS = (_Q_SEGMENT_COUNT * EPS)[None, :]


def _swap_halves(g):
    return jnp.concatenate([g[..., HALF:], g[..., :HALF]], axis=-1)


def _stacked_weights(g_mix, w_in):
    depth, d_model, _ = w_in.shape
    w = (jnp.swapaxes(w_in, 1, 2) * (g_mix * d_model ** 0.5)[:, None, :]).astype(BF16)
    kr0 = Q_LORA + KV_LORA
    w_in_p = jnp.concatenate(
        [w[:, :kr0 + QK_ROPE], w[:, kr0 + HALF:kr0 + QK_ROPE], w[:, kr0:kr0 + HALF],
         jnp.zeros((depth, LANES - 2 * QK_ROPE, d_model), BF16), w[:, kr0 + QK_ROPE:]], axis=1)
    return {"w_in": w_in_p}


def _small_weights(g_q_lora, w_q_up, g_qn, g_qr, g_kv_lora, g_kr, w_kv_up, g_kn, gm_ln_g, gm_ln_b, g_out_att,
                   g_out_gm, tq_attn):
    depth = w_q_up.shape[0]
    wq = w_q_up.reshape(depth, Q_LORA, N_HEADS, QK_HEAD)
    r1, r2 = wq[..., QK_NOPE:QK_NOPE + HALF], wq[..., QK_NOPE + HALF:]
    w_q = jnp.concatenate([wq[..., :QK_NOPE], r1, r2, r2, r1], axis=-1).reshape(depth, Q_LORA, N_HEADS * LANES)
    w_q = w_q * (g_q_lora * Q_LORA ** 0.5)[:, :, None]
    gq = (jnp.concatenate([g_qn * g_kn * QK_NOPE ** 0.5, g_qr, _swap_halves(g_qr)], axis=-1)
          * np.sqrt(_Q_SEGMENT_COUNT) * (QK_HEAD ** -0.5 * LOG2_E))
    gkr = jnp.concatenate([g_kr, _swap_halves(g_kr), jnp.zeros((depth, LANES - 2 * QK_ROPE), F32)],
                          axis=-1) * QK_ROPE ** 0.5
    wkv = w_kv_up.reshape(depth, KV_LORA, N_HEADS, QK_NOPE + V_HEAD)
    w_kvk = jnp.concatenate([wkv[..., :QK_NOPE], jnp.zeros((depth, KV_LORA, N_HEADS, LANES - QK_NOPE), F32)],
                            axis=-1).reshape(depth, KV_LORA, N_HEADS * LANES)
    w_kvv = wkv[..., QK_NOPE:].reshape(depth, KV_LORA, N_HEADS * V_HEAD)
    wk_t = jnp.transpose(wkv[..., :QK_NOPE], (0, 2, 3, 1))
    w_abs = jnp.concatenate([wk_t, jnp.zeros((depth, N_HEADS, LANES - QK_NOPE, KV_LORA), F32)], axis=2)
    w2 = jnp.concatenate([w_abs, jnp.broadcast_to(jnp.asarray(_ROPE_FOLD), (depth, N_HEADS, LANES, LANES))], axis=3)
    g_att = g_out_att.reshape(depth, N_HEADS * V_HEAD)
    return {
        "w_q": w_q.astype(BF16), "gq": gq[:, None, :],
        "g_kv_lora": g_kv_lora[:, None, :] * KV_LORA ** 0.5, "gkr": gkr[:, None, :],
        "w_kvk": w_kvk.astype(BF16), "w_kvv": w_kvv.astype(BF16),
        "w_kvv_t": jnp.swapaxes(w_kvv, 1, 2).astype(BF16),
        "w2": w2.astype(BF16), "wk_t": wk_t.reshape(depth, N_HEADS * QK_NOPE, KV_LORA).astype(BF16),
        "ln_g": gm_ln_g.reshape(depth, 1, GMLP_WIDTH) * GMLP_GROUP_DIM ** 0.5,
        "ln_b": gm_ln_b.reshape(depth, 1, GMLP_WIDTH),
        "g_out_att": g_att[:, None, :],
        "g_out_att_t": jnp.broadcast_to(g_att[:, :, None], (depth, N_HEADS * V_HEAD, tq_attn)),
        "g_out_gm": g_out_gm.reshape(depth, 1, GMLP_WIDTH) * GMLP_GROUP_DIM ** 0.5,
    }


def _chunk_bias(qpos, kpos):
    vis = (kpos[None, :] // CHUNK) <= (qpos[:, None] // CHUNK)
    return np.where(vis, 0.0, NEG_INF).astype(np.float32)


def kernel(x_prompt, x_sample, cache_ckv, cache_krope, g_mix, w_in, g_q_lora, w_q_up, g_qn, g_qr, g_kv_lora, g_kr,
           w_kv_up, g_kn, gm_ln_g, gm_ln_b, gm_w_s, gm_b_s, g_out_att, g_out_gm, w_o, g_ffn, w_gate, w_up, w_down):
    batch, seq, d = x_prompt.shape
    dbatch, dseq, _ = x_sample.shape
    depth, _, past, _ = cache_ckv.shape
    assert seq % GMLP_CHUNK == 0 and GMLP_CHUNK % dseq == 0 and (dbatch * dseq) % GMLP_CHUNK == 0

    tq_attn = 256
    tm_in = 1024
    tm_ffn = 1024
    tm_s = dbatch * dseq

    tq_p, tk_p = (jnp.asarray(t) for t in _rope_tables(np.arange(seq)))
    tq_s, tk_s = (jnp.asarray(np.tile(t, (dbatch, 1))) for t in _rope_tables(past + np.arange(dseq)))
    bias_p = jnp.asarray(_chunk_bias(np.arange(tq_attn), np.arange(tq_attn)))
    bias_s = np.full((dseq, LANES), NEG_INF, np.float32)
    bias_s[:, :dseq] = _chunk_bias(past + np.arange(dseq), past + np.arange(dseq))
    bias_s = jnp.asarray(np.tile(bias_s, (N_HEADS, 1)))
    reps = GMLP_CHUNK // dseq

    yp = x_prompt.reshape(batch * seq, d)
    ys = x_sample.reshape(dbatch * dseq, d)
    ckv_p, kr_p, ckv_s, kr_s, gv_s = [], [], [], [], []
    big = _stacked_weights(g_mix, w_in)
    g_ffn_s = (g_ffn * d ** 0.5)[:, None, :]
    ffn_w_f32 = [w_o, w_gate, w_up, w_down]
    lw = _small_weights(g_q_lora, w_q_up, g_qn, g_qr, g_kv_lora, g_kr, w_kv_up, g_kn, gm_ln_g, gm_ln_b, g_out_att,
                        g_out_gm, tq_attn)
    kr_cache_t = jnp.swapaxes(cache_krope, 2, 3)
    ws_p = gm_w_s
    bs_p = jnp.repeat(jnp.swapaxes(gm_b_s, 1, 2), GMLP_GROUP_DIM, axis=2)
    ws_s = jnp.tile(gm_w_s[:, :, :dseq, :dseq], (1, 1, reps, reps))
    bs_s = jnp.tile(jnp.repeat(jnp.swapaxes(gm_b_s[:, :, :dseq], 1, 2), GMLP_GROUP_DIM, axis=2), (1, reps, 1))
    for l in range(depth):
        q, k, vt, c, r, ogm = _inproj(yp, big, lw, tq_p, tk_p, ws_p, bs_p, layer=l, tm=tm_in, table_blocks=seq // tm_in,
                                      ws_mask="causal", want_vn=False, v_transposed=True)
        oatt, wo_b, wg_b, wu_b, wd_b = _attn_prompt(q, k, vt, bias_p.T, lw["g_out_att_t"], ffn_w_f32, layer=l,
                                                    batch=batch, seq=seq, tq=tq_attn)
        ffn_w = (g_ffn_s, wo_b, wg_b, wu_b, wd_b)
        yp = _ffn(yp, oatt, ogm, *ffn_w, layer=l, tm=tm_ffn, fchunk=256)
        ckv_p.append(c.reshape(batch, seq, KV_LORA))
        kr_p.append(jnp.transpose(r.reshape(QK_ROPE, batch, seq), (1, 2, 0)))

        q, k, v, c, r, ogm, vn = _inproj(ys, big, lw, tq_s, tk_s, ws_s, bs_s, layer=l, tm=tm_s, table_blocks=1,
                                         ws_mask="blockdiag", want_vn=True, v_transposed=False)
        oatt = _attn_sample(q, c, r, cache_ckv, kr_cache_t, lw["w2"], lw["wk_t"], lw["w_kvv"], bias_s,
                            lw["g_out_att"], layer=l, batch=dbatch, seq=dseq)
        ys = _ffn(ys, oatt, ogm, *ffn_w, layer=l, tm=tm_s, fchunk=256)
        ckv_s.append(c.reshape(dbatch, dseq, KV_LORA))
        kr_s.append(r.reshape(dbatch, dseq, QK_ROPE))
        gv_s.append(vn.reshape(dbatch, dseq, GMLP_WIDTH))

    return (yp.reshape(batch, seq, d), ys.reshape(dbatch, dseq, d), jnp.stack(ckv_p), jnp.stack(kr_p),
            jnp.stack(ckv_s), jnp.stack(kr_s), jnp.stack(gv_s))
```

```python
import functools

import numpy as np
import jax
import jax.numpy as jnp
from jax import lax
from jax.experimental import pallas as pl
from jax.experimental.pallas import tpu as pltpu

F32 = jnp.float32
BF16 = jnp.bfloat16

CHUNK = 64
N_HEADS = 8
QK_NOPE = 64
QK_ROPE = 32
QK_HEAD = QK_NOPE + QK_ROPE
V_HEAD = 64
Q_LORA = 256
KV_LORA = 128
ROPE_THETA = 10000.0
GMLP_GROUPS = 8
GMLP_GROUP_DIM = 64
GMLP_WIDTH = GMLP_GROUPS * GMLP_GROUP_DIM
GMLP_CHUNK = 128
EPS = 1e-6
NEG_INF = -1e30
LOG2_E = 1.4426950408889634

LANES = 128
BF16_SUBLANES = 16
V7X_VMEM_BYTES = 64 * 1024 * 1024
VMEM_LIMIT_BYTES = (V7X_VMEM_BYTES * 3) // 4

HALF = QK_ROPE // 2
V_ROWS = V_HEAD + BF16_SUBLANES
IN_COLS_PADDED = Q_LORA + KV_LORA + LANES + 2 * GMLP_WIDTH
COL_CKV = Q_LORA
COL_KR = Q_LORA + KV_LORA
COL_U = COL_KR + LANES
COL_V = COL_U + GMLP_WIDTH

_NT = (((1,), (1,)), ((), ()))


def _dot(a, b):
    return jnp.dot(a, b, preferred_element_type=F32)


def _dot_nt(a, b):
    return lax.dot_general(a, b, _NT, preferred_element_type=F32)


def _rsqrt_sumsq(x, n):
    return lax.rsqrt(jnp.sum(x * x, axis=-1, keepdims=True) + n * EPS)


def _half_sums(x, lo):
    s_lo = jnp.sum(jnp.where(lo, x, 0.0), axis=-1, keepdims=True)
    s_hi = jnp.sum(jnp.where(lo, 0.0, x), axis=-1, keepdims=True)
    return jnp.where(lo, s_lo, s_hi)


def _half_rms_scale(x, lo):
    return lax.rsqrt(_half_sums(x * x, lo) * (1.0 / 64) + EPS)


def _half_rsqrt_sumsq(x, lo):
    return lax.rsqrt(_half_sums(x * x, lo) + 64 * EPS)


def _gelu_tanh(t):
    c = 0.7978845608028654
    h = 0.5 * t
    return h + h * jnp.tanh(t * (c + (c * 0.044715) * (t * t)))


INPROJ_SUB_ROWS = 256


def _inproj_kernel(x_ref, win_ref, wq_ref, gq_ref, tq_ref, gkv_ref, gkr_ref, tk_ref,
                   wkk_ref, wkv_ref, lng_ref, lnb_ref, ws_ref, bs_ref, ggm_ref, seg_ref, epsq_ref,
                   *rest, ws_mask, v_transposed):
    if v_transposed:
        q_out, k_out, v_out, ckv_out, kr_out, ogm_out, *vn_out, win_scr = rest
    else:
        q_out, ckv_out, kr_out, ogm_out, *vn_out, win_scr = rest
    tm = x_ref.shape[0]
    sub = min(tm, INPROJ_SUB_ROWS)

    @pl.when(pl.program_id(0) == 0)
    def _():
        win_scr[...] = win_ref[...].T

    lane = lax.broadcasted_iota(jnp.int32, (sub, LANES), 1)
    lo = lane < 64
    lo_c = lax.broadcasted_iota(jnp.int32, (GMLP_CHUNK, LANES), 1) < 64
    row = lax.broadcasted_iota(jnp.int32, (GMLP_CHUNK, GMLP_CHUNK), 0) // CHUNK
    col = lax.broadcasted_iota(jnp.int32, (GMLP_CHUNK, GMLP_CHUNK), 1) // CHUNK
    keep = (row >= col) if ws_mask == "causal" else (row == col)

    def project(r0):
        x = x_ref[r0:r0 + sub, :]
        h = (x * _rsqrt_sumsq(x, x.shape[1])).astype(BF16)
        return _dot(h, win_scr[...])

    def finish(z, r0):
        rows = slice(r0, r0 + sub)
        ql = z[:, 0:Q_LORA]
        qn = (ql * _rsqrt_sumsq(ql, Q_LORA)).astype(BF16)
        qz = _dot(qn, wq_ref[...])
        tq = tq_ref[rows, :] * gq_ref[...]
        for hd in range(N_HEADS):
            cols = slice(hd * LANES, (hd + 1) * LANES)
            xh = qz[:, cols]
            ss = _dot((xh * xh).astype(BF16), seg_ref[...])
            q_out[rows, cols] = (xh * lax.rsqrt(ss + epsq_ref[...]) * tq).astype(BF16)

        y0 = z[:, COL_KR:COL_KR + LANES]
        s_k = jnp.sum(jnp.where(lane < QK_ROPE, y0 * y0, 0.0), axis=-1, keepdims=True)
        y = y0 * lax.rsqrt(s_k + QK_ROPE * EPS) * (gkr_ref[...] * tk_ref[rows, :])
        r64 = pltpu.roll(y, 64, 1)
        r32 = pltpu.roll(y, 32, 1)
        r96 = pltpu.roll(y, 96, 1)
        if v_transposed:
            kr_out[:, rows] = (y + r96).T[0:QK_ROPE, :]
        else:
            kr_out[rows, :] = (y + r96)[:, 0:QK_ROPE]
        kr_hi = jnp.where(lo, 0.0, r64 + r32 + r96)

        c0 = z[:, COL_CKV:COL_CKV + KV_LORA]
        cn = c0 * _rsqrt_sumsq(c0, KV_LORA) * gkv_ref[...]
        ckv_out[rows, :] = cn
        cb = cn.astype(BF16)
        if v_transposed:
            kz = _dot(cb, wkk_ref[...])
            for hd in range(N_HEADS):
                cols = slice(hd * LANES, (hd + 1) * LANES)
                xh = kz[:, cols]
                kn = xh * _rsqrt_sumsq(xh, QK_NOPE)
                k_out[rows, cols] = jnp.where(lo, kn, kr_hi).astype(BF16)
            vt = _dot_nt(wkv_ref[...], cb).astype(BF16)
            tail = jnp.where(lax.broadcasted_iota(jnp.int32, (BF16_SUBLANES, sub), 0) == 0, 1.0, 0.0).astype(BF16)
            for hd in range(N_HEADS):
                v_out[hd * V_ROWS:hd * V_ROWS + V_HEAD, rows] = vt[hd * V_HEAD:(hd + 1) * V_HEAD]
                v_out[hd * V_ROWS + V_HEAD:(hd + 1) * V_ROWS, rows] = tail

        for p in range(GMLP_GROUPS // 2):
            sl = slice(p * LANES, (p + 1) * LANES)
            w_pair = jnp.concatenate(
                [jnp.where(keep, ws_ref[2 * p], 0.0), jnp.where(keep, ws_ref[2 * p + 1], 0.0)], axis=0).astype(BF16)
            ug = _gelu_tanh(z[:, COL_U + p * LANES:COL_U + (p + 1) * LANES])
            vg = _gelu_tanh(z[:, COL_V + p * LANES:COL_V + (p + 1) * LANES])
            xc = vg - _half_sums(vg, lo) * (1.0 / GMLP_GROUP_DIM)
            vn = xc * _half_rsqrt_sumsq(xc, lo) * lng_ref[:, sl] + lnb_ref[:, sl]
            if vn_out:
                vn_out[0][rows, sl] = vn
            vnb = vn.astype(BF16)
            for c in range(sub // GMLP_CHUNK):
                crows = slice(c * GMLP_CHUNK, (c + 1) * GMLP_CHUNK)
                res = _dot(w_pair, vnb[crows])
                mixed = jnp.where(lo_c, res[0:GMLP_CHUNK], res[GMLP_CHUNK:]) + bs_ref[:, sl]
                og = ug[crows] * mixed
                ogm_out[r0 + c * GMLP_CHUNK:r0 + (c + 1) * GMLP_CHUNK, sl] = (
                    og * _half_rsqrt_sumsq(og, lo_c) * ggm_ref[:, sl]).astype(BF16)

    starts = list(range(0, tm, sub))
    z_next = project(starts[0])
    for i, r0 in enumerate(starts):
        z_cur = z_next
        if i + 1 < len(starts):
            z_next = project(starts[i + 1])
        finish(z_cur, r0)


def _const_spec(shape):
    return pl.BlockSpec(shape, lambda *_: (0,) * len(shape), pipeline_mode=pl.Buffered(1))


def _layer_spec(stacked, layer):
    zeros = (0,) * (stacked.ndim - 1)
    return pl.BlockSpec((None,) + stacked.shape[1:], lambda *_: (layer,) + zeros, pipeline_mode=pl.Buffered(1))


def _inproj(x, big, lw, tq, tk, ws, bs, *, layer, tm, table_blocks, ws_mask, want_vn, v_transposed):
    n, d = x.shape
    w_v = lw["w_kvv_t"] if v_transposed else lw["w_kvv"]
    grid = (n // tm,)
    tbl = lambda i: (i % table_blocks, 0)
    row = lambda i: (i, 0)
    layer_args = [big["w_in"], lw["w_q"], lw["gq"]]
    layer_args2 = [lw["g_kv_lora"], lw["gkr"]]
    layer_args3 = [lw["w_kvk"], w_v, lw["ln_g"], lw["ln_b"], ws, bs, lw["g_out_gm"]]
    in_specs = (
        [pl.BlockSpec((tm, d), row)] + [_layer_spec(a, layer) for a in layer_args]
        + [pl.BlockSpec((tm, LANES), tbl)] + [_layer_spec(a, layer) for a in layer_args2]
        + [pl.BlockSpec((tm, LANES), tbl)] + [_layer_spec(a, layer) for a in layer_args3]
        + [_const_spec((LANES, LANES)), _const_spec((1, LANES))])
    kv_shapes = [jax.ShapeDtypeStruct((n, N_HEADS * LANES), BF16),
                 jax.ShapeDtypeStruct((N_HEADS * V_ROWS, n), BF16)]
    kv_specs = [pl.BlockSpec((tm, N_HEADS * LANES), row), pl.BlockSpec((N_HEADS * V_ROWS, tm), lambda i: (0, i))]
    out_shape = [
        jax.ShapeDtypeStruct((n, N_HEADS * LANES), BF16),
        *(kv_shapes if v_transposed else []),
        jax.ShapeDtypeStruct((n, KV_LORA), F32),
        (jax.ShapeDtypeStruct((QK_ROPE, n), F32) if v_transposed
         else jax.ShapeDtypeStruct((n, QK_ROPE), F32)),
        jax.ShapeDtypeStruct((n, GMLP_WIDTH), BF16),
    ]
    out_specs = [
        pl.BlockSpec((tm, N_HEADS * LANES), row),
        *(kv_specs if v_transposed else []),
        pl.BlockSpec((tm, KV_LORA), row),
        (pl.BlockSpec((QK_ROPE, tm), lambda i: (0, i)) if v_transposed else pl.BlockSpec((tm, QK_ROPE), row)),
        pl.BlockSpec((tm, GMLP_WIDTH), row),
    ]
    if want_vn:
        out_shape.append(jax.ShapeDtypeStruct((n, GMLP_WIDTH), F32))
        out_specs.append(pl.BlockSpec((tm, GMLP_WIDTH), row))
    return pl.pallas_call(
        functools.partial(_inproj_kernel, ws_mask=ws_mask, v_transposed=v_transposed),
        grid=grid, in_specs=in_specs, out_specs=out_specs, out_shape=out_shape,
        scratch_shapes=[pltpu.VMEM((d, IN_COLS_PADDED), BF16)],
        compiler_params=pltpu.CompilerParams(dimension_semantics=("arbitrary",),
                                             vmem_limit_bytes=VMEM_LIMIT_BYTES),
        name="inproj_vn" if want_vn else "inproj",
    )(x, *layer_args, tq, *layer_args2, tk, *layer_args3,
      jnp.asarray(_Q_SEGMENTS, BF16), jnp.asarray(_Q_SEGMENT_EPS))


ATTN_STREAM_LAG = 18


def _attn_prompt_kernel(q_ref, k_ref, vt_ref, bias_ref, g_ref, *rest, tq, n_cast):
    w_f32 = rest[:n_cast]
    o_ref = rest[n_cast]
    w_bf16 = rest[n_cast + 1:2 * n_cast + 1]
    s_scr = rest[2 * n_cast + 1]
    for src, dst in zip(w_f32, w_bf16):
        dst[...] = src[...].astype(BF16)

    s_len = q_ref.shape[0]
    bodies = [(qi, hh) for qi in reversed(range(s_len // tq)) for hh in range(2)]
    bases = np.cumsum([0] + [qi + 1 for qi, _ in bodies])
    state = [{} for _ in bodies]

    def a_steps(bi):
        qi, hh = bodies[bi]
        cols = slice(hh * LANES, (hh + 1) * LANES)
        rows = slice(qi * tq, (qi + 1) * tq)
        st = state[bi]

        def chunk(j):
            s = _dot_nt(k_ref[j * tq:(j + 1) * tq, cols], q_ref[rows, cols])
            if j == qi:
                s = s + bias_ref[...]
            s_scr[bases[bi] + j] = s
            cm = jnp.max(s.reshape(tq // 8, 8, tq), axis=0)
            st["m_run"] = cm if j == 0 else jnp.maximum(st["m_run"], cm)

        def finish():
            st["m"] = jnp.max(st["m_run"], axis=0, keepdims=True)

        return [functools.partial(chunk, j) for j in range(qi + 1)] + [finish]

    def b_steps(bi):
        qi, hh = bodies[bi]
        st = state[bi]

        def chunk(j):
            p = jnp.exp2(s_scr[bases[bi] + j] - st["m"]).astype(BF16)
            d = _dot(vt_ref[hh * V_ROWS:(hh + 1) * V_ROWS, j * tq:(j + 1) * tq], p)
            st["acc"] = d if j == 0 else st["acc"] + d

        def finish():
            acc = st["acc"]
            o = acc[0:V_HEAD] * (1.0 / acc[V_HEAD:V_HEAD + 1])
            r = lax.rsqrt(jnp.sum(o * o, axis=0, keepdims=True) * (1.0 / V_HEAD) + EPS)
            st["out"] = o * r * g_ref[hh * V_HEAD:(hh + 1) * V_HEAD, :]
            if hh == 1:
                pair_t = jnp.concatenate([state[bi - 1]["out"], st["out"]], axis=0)
                o_ref[qi * tq:(qi + 1) * tq, :] = pair_t.T.astype(BF16)

        return [functools.partial(chunk, j) for j in range(qi + 1)] + [finish]

    a_all = [f for bi in range(len(bodies)) for f in a_steps(bi)]
    b_all = [f for bi in range(len(bodies)) for f in b_steps(bi)]
    for i in range(len(a_all) + ATTN_STREAM_LAG):
        if i < len(a_all):
            a_all[i]()
        if i >= ATTN_STREAM_LAG:
            b_all[i - ATTN_STREAM_LAG]()


def _attn_prompt(q, k, vt, bias_t, g_t, w_f32, *, layer, batch, seq, tq):
    n = q.shape[0]
    pairs = N_HEADS // 2
    nq = seq // tq
    steps = batch * pairs
    assert ATTN_STREAM_LAG > nq + 1

    def cast_specs(w):
        share = next(s for s in (1, 2, 4, 8) if w.shape[1] % ((steps // s) * BF16_SUBLANES) == 0)
        rows = w.shape[1] // (steps // share)
        slab = lambda b, p: (b * pairs + p) // share
        return (pl.BlockSpec((None, rows, w.shape[2]), lambda b, p: (layer, slab(b, p), 0)),
                pl.BlockSpec((rows, w.shape[2]), lambda b, p: (slab(b, p), 0)))

    w_specs = [cast_specs(w) for w in w_f32]
    return pl.pallas_call(
        functools.partial(_attn_prompt_kernel, tq=tq, n_cast=len(w_f32)),
        grid=(batch, pairs),
        in_specs=[
            pl.BlockSpec((seq, 2 * LANES), lambda b, p: (b, p)),
            pl.BlockSpec((seq, 2 * LANES), lambda b, p: (b, p)),
            pl.BlockSpec((2 * V_ROWS, seq), lambda b, p: (p, b)),
            pl.BlockSpec((tq, tq), lambda b, p: (0, 0)),
            pl.BlockSpec((None, 2 * V_HEAD, tq), lambda b, p: (layer, p, 0)),
        ] + [i for i, _ in w_specs],
        out_specs=[pl.BlockSpec((seq, 2 * V_HEAD), lambda b, p: (b, p))] + [o for _, o in w_specs],
        out_shape=[jax.ShapeDtypeStruct((n, N_HEADS * V_HEAD), BF16)]
        + [jax.ShapeDtypeStruct(w.shape[1:], BF16) for w in w_f32],
        scratch_shapes=[pltpu.VMEM((nq * (nq + 1), tq, tq), F32)],
        compiler_params=pltpu.CompilerParams(dimension_semantics=("arbitrary", "arbitrary"),
                                             vmem_limit_bytes=VMEM_LIMIT_BYTES),
        name="attn_prompt",
    )(q, k, vt, bias_t, g_t, *w_f32)


SAMPLE_KEY_BLOCK = 1024


def _attn_sample_kernel(q_ref, cnew_ref, krnew_ref, ckv_ref, krt_ref, w2_ref, wkt_ref, wv_ref, bias_ref, gout_ref,
                        o_ref, kr_scr, *, kblk):
    past = ckv_ref.shape[0]
    sq = q_ref.shape[0]
    q2 = jnp.concatenate([_dot(q_ref[:, h * LANES:(h + 1) * LANES], w2_ref[h]) for h in range(N_HEADS)],
                         axis=0).astype(BF16)
    qt = q2[:, 0:KV_LORA]
    qr = q2[:, KV_LORA:]
    kr_t = jnp.concatenate([krt_ref[...].astype(BF16), jnp.zeros((LANES - QK_ROPE, past), BF16)], axis=0)
    kr_scr[...] = jnp.zeros(kr_scr.shape, F32)
    kr_scr[0:sq, 0:QK_ROPE] = krnew_ref[...]
    ones_blk = jnp.ones((kblk, LANES), BF16)

    def scores(cb, s2):
        kzt = _dot_nt(wkt_ref[...], cb)
        s1 = _dot_nt(qt, cb)
        out = []
        for h in range(N_HEADS):
            kh = kzt[h * QK_NOPE:(h + 1) * QK_NOPE]
            r = lax.rsqrt(jnp.sum(kh * kh, axis=0, keepdims=True) + QK_NOPE * EPS)
            out.append(s1[h * sq:(h + 1) * sq] * r + s2[h * sq:(h + 1) * sq])
        return jnp.concatenate(out, axis=0)

    cn = jnp.concatenate([cnew_ref[...], jnp.zeros((LANES - sq, KV_LORA), F32)], axis=0).astype(BF16)
    blocks = [(ckv_ref[j * kblk:(j + 1) * kblk, :], kr_t[:, j * kblk:(j + 1) * kblk]) for j in range(past // kblk)]

    def block_scores(j):
        if j < len(blocks):
            cb = blocks[j][0].astype(BF16)
            return scores(cb, _dot(qr, blocks[j][1])), jnp.concatenate([cb, ones_blk], axis=1)
        s_new = scores(cn, _dot_nt(qr, kr_scr[...].astype(BF16))) + bias_ref[...]
        return s_new, jnp.concatenate([cn, ones_blk[0:LANES]], axis=1)

    m = acc = None
    nxt = block_scores(0)
    for j in range(len(blocks) + 1):
        s, vals = nxt
        if j < len(blocks):
            nxt = block_scores(j + 1)
        m_blk = jnp.max(s, axis=-1, keepdims=True)
        m_new = m_blk if m is None else jnp.maximum(m, m_blk)
        d = _dot(jnp.exp2(s - m_new).astype(BF16), vals)
        acc = d if acc is None else acc * jnp.exp2(m - m_new) + d
        m = m_new
    o_lat = (acc[:, 0:KV_LORA] * (1.0 / acc[:, KV_LORA:])).astype(BF16)
    o_full = _dot(o_lat, wv_ref[...])
    head = lax.broadcasted_iota(jnp.int32, (sq, N_HEADS * V_HEAD), 1) >> 6
    o = jnp.zeros((sq, N_HEADS * V_HEAD), F32)
    for h in range(N_HEADS):
        o = jnp.where(head == h, o_full[h * sq:(h + 1) * sq], o)
    lo = lax.broadcasted_iota(jnp.int32, (sq, LANES), 1) < 64
    for c in range(0, N_HEADS * V_HEAD, LANES):
        oc = o[:, c:c + LANES]
        o_ref[:, c:c + LANES] = (oc * _half_rms_scale(oc, lo) * gout_ref[:, c:c + LANES]).astype(BF16)


def _attn_sample(q, cnew, krnew, cache_ckv, kr_cache_t, w2, wkt, w_kvv, bias, gout, *, layer, batch, seq):
    n = q.shape[0]
    past = cache_ckv.shape[2]
    assert V_HEAD == 64 and past % SAMPLE_KEY_BLOCK == 0 and seq <= LANES
    return pl.pallas_call(
        functools.partial(_attn_sample_kernel, kblk=SAMPLE_KEY_BLOCK),
        grid=(batch,),
        in_specs=[
            pl.BlockSpec((seq, N_HEADS * LANES), lambda b: (b, 0)),
            pl.BlockSpec((seq, KV_LORA), lambda b: (b, 0)),
            pl.BlockSpec((seq, QK_ROPE), lambda b: (b, 0)),
            pl.BlockSpec((None, None, past, KV_LORA), lambda b: (layer, b, 0, 0)),
            pl.BlockSpec((None, None, QK_ROPE, past), lambda b: (layer, b, 0, 0)),
            _layer_spec(w2, layer),
            _layer_spec(wkt, layer),
            _layer_spec(w_kvv, layer),
            pl.BlockSpec((N_HEADS * seq, LANES), lambda b: (0, 0)),
            _layer_spec(gout, layer),
        ],
        out_specs=pl.BlockSpec((seq, N_HEADS * V_HEAD), lambda b: (b, 0)),
        out_shape=jax.ShapeDtypeStruct((n, N_HEADS * V_HEAD), BF16),
        scratch_shapes=[pltpu.VMEM((LANES, LANES), F32)],
        compiler_params=pltpu.CompilerParams(dimension_semantics=("arbitrary",),
                                             vmem_limit_bytes=VMEM_LIMIT_BYTES),
        name="attn_sample",
    )(q, cnew, krnew, cache_ckv, kr_cache_t, w2, wkt, w_kvv, bias, gout)


def _ffn_kernel(x_ref, oa_ref, og_ref, wo_ref, gffn_ref, wg_ref, wu_ref, wd_ref, y_ref, *, fchunk):
    x1 = x_ref[...] + _dot(jnp.concatenate([oa_ref[...], og_ref[...]], axis=1), wo_ref[...])
    h2 = (x1 * _rsqrt_sumsq(x1, x1.shape[1]) * gffn_ref[...]).astype(BF16)
    acc = None
    for c in range(0, wg_ref.shape[1], fchunk):
        g = _dot(h2, wg_ref[:, c:c + fchunk])
        u = _dot(h2, wu_ref[:, c:c + fchunk])
        a = (g * jax.nn.sigmoid(g) * u).astype(BF16)
        d = _dot(a, wd_ref[c:c + fchunk, :])
        acc = d if acc is None else acc + d
    y_ref[...] = x1 + acc


def _ffn(x, oa, og, g_ffn, w_o, w_gate, w_up, w_down, *, layer, tm, fchunk):
    n, d = x.shape
    row = lambda i: (i, 0)
    return pl.pallas_call(
        functools.partial(_ffn_kernel, fchunk=fchunk),
        grid=(n // tm,),
        in_specs=[
            pl.BlockSpec((tm, d), row),
            pl.BlockSpec((tm, oa.shape[1]), row),
            pl.BlockSpec((tm, og.shape[1]), row),
            _const_spec(w_o.shape),
            _layer_spec(g_ffn, layer),
            _const_spec(w_gate.shape),
            _const_spec(w_up.shape),
            _const_spec(w_down.shape),
        ],
        out_specs=pl.BlockSpec((tm, d), row),
        out_shape=jax.ShapeDtypeStruct((n, d), F32),
        compiler_params=pltpu.CompilerParams(dimension_semantics=("arbitrary",),
                                             vmem_limit_bytes=VMEM_LIMIT_BYTES),
        name="outproj_ffn",
    )(x, oa, og, w_o, g_ffn, w_gate, w_up, w_down)


def _rope_tables(pos):
    inv = ROPE_THETA ** (-np.arange(HALF, dtype=np.float64) / HALF)
    ang = np.asarray(pos, np.float64)[:, None] * inv[None, :]
    cos, sin = np.cos(ang), np.sin(ang)
    n = ang.shape[0]
    tq = np.concatenate([np.ones((n, QK_NOPE)), cos, cos, -sin, sin], axis=1)
    tk = np.concatenate([cos, cos, -sin, sin, np.zeros((n, LANES - 2 * QK_ROPE))], axis=1)
    return tq.astype(np.float32), tk.astype(np.float32)


def _rope_fold_matrix():
    f = np.zeros((LANES, LANES), np.float32)
    for i in range(QK_ROPE):
        f[QK_NOPE + i, i] = 1.0
        f[QK_NOPE + QK_ROPE + i, i] = 1.0
    return f


_ROPE_FOLD = _rope_fold_matrix()


def _q_segment_matrix():
    m = np.zeros((LANES, LANES), np.float32)
    m[:QK_NOPE, :QK_NOPE] = 1.0
    m[QK_NOPE:QK_NOPE + QK_ROPE, QK_NOPE:] = 1.0
    return m


_Q_SEGMENTS = _q_segment_matrix()
_Q_SEGMENT_COUNT = np.concatenate([np.full(QK_NOPE, QK_NOPE), np.full(LANES - QK_NOPE, QK_ROPE)]).astype(np.float32)
_Q_SEGMENT_EPS = (_Q_SEGMENT_COUNT * EPS)[None, :]


def _swap_halves(g):
    return jnp.concatenate([g[..., HALF:], g[..., :HALF]], axis=-1)


def _stacked_weights(g_mix, w_in):
    depth, d_model, _ = w_in.shape
    w = (jnp.swapaxes(w_in, 1, 2) * (g_mix * d_model ** 0.5)[:, None, :]).astype(BF16)
    kr0 = Q_LORA + KV_LORA
    w_in_p = jnp.concatenate(
        [w[:, :kr0 + QK_ROPE], w[:, kr0 + HALF:kr0 + QK_ROPE], w[:, kr0:kr0 + HALF],
         jnp.zeros((depth, LANES - 2 * QK_ROPE, d_model), BF16), w[:, kr0 + QK_ROPE:]], axis=1)
    return {"w_in": w_in_p}


def _small_weights(g_q_lora, w_q_up, g_qn, g_qr, g_kv_lora, g_kr, w_kv_up, g_kn, gm_ln_g, gm_ln_b, g_out_att,
                   g_out_gm, tq_attn):
    depth = w_q_up.shape[0]
    wq = w_q_up.reshape(depth, Q_LORA, N_HEADS, QK_HEAD)
    r1, r2 = wq[..., QK_NOPE:QK_NOPE + HALF], wq[..., QK_NOPE + HALF:]
    w_q = jnp.concatenate([wq[..., :QK_NOPE], r1, r2, r2, r1], axis=-1).reshape(depth, Q_LORA, N_HEADS * LANES)
    w_q = w_q * (g_q_lora * Q_LORA ** 0.5)[:, :, None]
    gq = (jnp.concatenate([g_qn * g_kn * QK_NOPE ** 0.5, g_qr, _swap_halves(g_qr)], axis=-1)
          * np.sqrt(_Q_SEGMENT_COUNT) * (QK_HEAD ** -0.5 * LOG2_E))
    gkr = jnp.concatenate([g_kr, _swap_halves(g_kr), jnp.zeros((depth, LANES - 2 * QK_ROPE), F32)],
                          axis=-1) * QK_ROPE ** 0.5
    wkv = w_kv_up.reshape(depth, KV_LORA, N_HEADS, QK_NOPE + V_HEAD)
    w_kvk = jnp.concatenate([wkv[..., :QK_NOPE], jnp.zeros((depth, KV_LORA, N_HEADS, LANES - QK_NOPE), F32)],
                            axis=-1).reshape(depth, KV_LORA, N_HEADS * LANES)
    w_kvv = wkv[..., QK_NOPE:].reshape(depth, KV_LORA, N_HEADS * V_HEAD)
    wk_t = jnp.transpose(wkv[..., :QK_NOPE], (0, 2, 3, 1))
    w_abs = jnp.concatenate([wk_t, jnp.zeros((depth, N_HEADS, LANES - QK_NOPE, KV_LORA), F32)], axis=2)
    w2 = jnp.concatenate([w_abs, jnp.broadcast_to(jnp.asarray(_ROPE_FOLD), (depth, N_HEADS, LANES, LANES))], axis=3)
    g_att = g_out_att.reshape(depth, N_HEADS * V_HEAD)
    return {
        "w_q": w_q.astype(BF16), "gq": gq[:, None, :],
        "g_kv_lora": g_kv_lora[:, None, :] * KV_LORA ** 0.5, "gkr": gkr[:, None, :],
        "w_kvk": w_kvk.astype(BF16), "w_kvv": w_kvv.astype(BF16),
        "w_kvv_t": jnp.swapaxes(w_kvv, 1, 2).astype(BF16),
        "w2": w2.astype(BF16), "wk_t": wk_t.reshape(depth, N_HEADS * QK_NOPE, KV_LORA).astype(BF16),
        "ln_g": gm_ln_g.reshape(depth, 1, GMLP_WIDTH) * GMLP_GROUP_DIM ** 0.5,
        "ln_b": gm_ln_b.reshape(depth, 1, GMLP_WIDTH),
        "g_out_att": g_att[:, None, :],
        "g_out_att_t": jnp.broadcast_to(g_att[:, :, None], (depth, N_HEADS * V_HEAD, tq_attn)),
        "g_out_gm": g_out_gm.reshape(depth, 1, GMLP_WIDTH) * GMLP_GROUP_DIM ** 0.5,
    }


def _chunk_bias(qpos, kpos):
    vis = (kpos[None, :] // CHUNK) <= (qpos[:, None] // CHUNK)
    return np.where(vis, 0.0, NEG_INF).astype(np.float32)


def kernel(x_prompt, x_sample, cache_ckv, cache_krope, g_mix, w_in, g_q_lora, w_q_up, g_qn, g_qr, g_kv_lora, g_kr,
           w_kv_up, g_kn, gm_ln_g, gm_ln_b, gm_w_s, gm_b_s, g_out_att, g_out_gm, w_o, g_ffn, w_gate, w_up, w_down):
    batch, seq, d = x_prompt.shape
    dbatch, dseq, _ = x_sample.shape
    depth, _, past, _ = cache_ckv.shape
    assert seq % GMLP_CHUNK == 0 and GMLP_CHUNK % dseq == 0 and (dbatch * dseq) % GMLP_CHUNK == 0

    tq_attn = 256
    tm_in = 1024
    tm_ffn = 1024
    tm_s = dbatch * dseq

    tq_p, tk_p = (jnp.asarray(t) for t in _rope_tables(np.arange(seq)))
    tq_s, tk_s = (jnp.asarray(np.tile(t, (dbatch, 1))) for t in _rope_tables(past + np.arange(dseq)))
    bias_p = jnp.asarray(_chunk_bias(np.arange(tq_attn), np.arange(tq_attn)))
    bias_s = np.full((dseq, LANES), NEG_INF, np.float32)
    bias_s[:, :dseq] = _chunk_bias(past + np.arange(dseq), past + np.arange(dseq))
    bias_s = jnp.asarray(np.tile(bias_s, (N_HEADS, 1)))
    reps = GMLP_CHUNK // dseq

    yp = x_prompt.reshape(batch * seq, d)
    ys = x_sample.reshape(dbatch * dseq, d)
    ckv_p, kr_p, ckv_s, kr_s, gv_s = [], [], [], [], []
    big = _stacked_weights(g_mix, w_in)
    g_ffn_s = (g_ffn * d ** 0.5)[:, None, :]
    ffn_w_f32 = [w_o, w_gate, w_up, w_down]
    lw = _small_weights(g_q_lora, w_q_up, g_qn, g_qr, g_kv_lora, g_kr, w_kv_up, g_kn, gm_ln_g, gm_ln_b, g_out_att,
                        g_out_gm, tq_attn)
    kr_cache_t = jnp.swapaxes(cache_krope, 2, 3)
    ws_p = gm_w_s
    bs_p = jnp.repeat(jnp.swapaxes(gm_b_s, 1, 2), GMLP_GROUP_DIM, axis=2)
    ws_s = jnp.tile(gm_w_s[:, :, :dseq, :dseq], (1, 1, reps, reps))
    bs_s = jnp.tile(jnp.repeat(jnp.swapaxes(gm_b_s[:, :, :dseq], 1, 2), GMLP_GROUP_DIM, axis=2), (1, reps, 1))
    for l in range(depth):
        q, k, vt, c, r, ogm = _inproj(yp, big, lw, tq_p, tk_p, ws_p, bs_p, layer=l, tm=tm_in, table_blocks=seq // tm_in,
                                      ws_mask="causal", want_vn=False, v_transposed=True)
        oatt, wo_b, wg_b, wu_b, wd_b = _attn_prompt(q, k, vt, bias_p.T, lw["g_out_att_t"], ffn_w_f32, layer=l,
                                                    batch=batch, seq=seq, tq=tq_attn)
        ffn_w = (g_ffn_s, wo_b, wg_b, wu_b, wd_b)
        yp = _ffn(yp, oatt, ogm, *ffn_w, layer=l, tm=tm_ffn, fchunk=256)
        ckv_p.append(c.reshape(batch, seq, KV_LORA))
        kr_p.append(jnp.transpose(r.reshape(QK_ROPE, batch, seq), (1, 2, 0)))

        q, c, r, ogm, vn = _inproj(ys, big, lw, tq_s, tk_s, ws_s, bs_s, layer=l, tm=tm_s, table_blocks=1,
                                         ws_mask="blockdiag", want_vn=True, v_transposed=False)
        oatt = _attn_sample(q, c, r, cache_ckv, kr_cache_t, lw["w2"], lw["wk_t"], lw["w_kvv"], bias_s,
                            lw["g_out_att"], layer=l, batch=dbatch, seq=dseq)
        ys = _ffn(ys, oatt, ogm, *ffn_w, layer=l, tm=tm_s, fchunk=256)
        ckv_s.append(c.reshape(dbatch, dseq, KV_LORA))
        kr_s.append(r.reshape(dbatch, dseq, QK_ROPE))
        gv_s.append(vn.reshape(dbatch, dseq, GMLP_WIDTH))

    return (yp.reshape(batch, seq, d), ys.reshape(dbatch, dseq, d), jnp.stack(ckv_p), jnp.stack(kr_p),
            jnp.stack(ckv_s), jnp.stack(kr_s), jnp.stack(gv_s))
```
